```python
import jax, jax.numpy as jnp
from jax import lax
import numpy as np

D_MODEL = 1024
BATCH = 8
SEQ = 2048
DEPTH = 2
DEC_BATCH = 128
DEC_SEQ = 4
PAST_LEN = 16384
PAGE_SIZE = 128

N_META = 16
GLA_HEADS = 4
GLA_DK = D_MODEL // 2
GLA_DV = D_MODEL
GLA_DKH = GLA_DK // GLA_HEADS
GLA_DVH = GLA_DV // GLA_HEADS
GATE_RANK = 16
GATE_TAU = 16.0
GLA_CHUNK = 64
POOL_WIDTH = D_MODEL // 2
POOL_WINDOWS = (2, 4, 8, 16)
POOL_GROUPS = len(POOL_WINDOWS)
POOL_GW = POOL_WIDTH // POOL_GROUPS
POOL_BUF = max(POOL_WINDOWS) - 1
N_BRANCH = 2
EPS = 1e-6
IN_SIZES = (GLA_DK, GLA_DK, GLA_DV, GLA_DV, POOL_WIDTH, POOL_WIDTH, GATE_RANK, N_BRANCH * D_MODEL)
IN_OFFSETS = tuple(int(o) for o in np.cumsum(IN_SIZES)[:-1])
IN_COLS = int(sum(IN_SIZES))

kernel_name = "gla_pool_hybrid_step"


def rmsnorm(x, g):
    xf = x.astype(jnp.float32)
    y = xf * lax.rsqrt(jnp.mean(xf * xf, axis=-1, keepdims=True) + EPS)
    return (y * g.astype(jnp.float32)).astype(x.dtype)


def gla_chunk_step(S, chunk):
    q, k, v, la = chunk
    qf = q.astype(jnp.float32)
    kf = k.astype(jnp.float32)
    vf = v.astype(jnp.float32)
    b = jnp.cumsum(la, axis=2)
    C = q.shape[2]
    causal = jnp.tril(jnp.ones((C, C), dtype=bool))
    diff = b[:, :, :, None, :] - b[:, :, None, :, :]
    decay = jnp.exp(jnp.where(causal[None, None, :, :, None], diff, -jnp.inf))
    A = jnp.einsum('bhid,bhjd,bhijd->bhij', qf, kf, decay)
    o = jnp.einsum('bhij,bhjv->bhiv', A, vf) + jnp.einsum('bhid,bhdv->bhiv', qf * jnp.exp(b), S)
    bC = b[:, :, -1:, :]
    kd = kf * jnp.exp(bC - b)
    S_new = jnp.exp(bC[:, :, 0, :])[..., None] * S + jnp.einsum('bhjd,bhjv->bhdv', kd, vf)
    return S_new, o


def gla_scan(q, k, v, la, S0, chunk):
    B, L, H, _ = q.shape
    DV = v.shape[-1]
    n = L // chunk

    def to_chunks(t):
        return t.reshape(B, n, chunk, H, t.shape[-1]).transpose(1, 0, 3, 2, 4)

    S, o = lax.scan(gla_chunk_step, S0.astype(jnp.float32),
                    (to_chunks(q), to_chunks(k), to_chunks(v), to_chunks(la)))
    o = o.transpose(1, 0, 3, 2, 4).reshape(B, L, H, DV)
    return o, S


def pool_mix(u_ext, n_prev, pos0, w_pool, scale):
    B, T, C = u_ext.shape
    L = T - n_prev
    uf = u_ext.astype(jnp.float32)
    cs = jnp.concatenate([jnp.zeros((B, 1, C), jnp.float32), jnp.cumsum(uf, axis=1)], axis=1)
    idx = n_prev + np.arange(L)
    pos = pos0 + np.arange(L)
    hi = cs[:, idx + 1]
    u_new = uf[:, n_prev:]
    outs = []
    for g, w in enumerate(POOL_WINDOWS):
        sl = slice(g * POOL_GW, (g + 1) * POOL_GW)
        lo = cs[:, np.maximum(idx + 1 - w, 0), sl]
        cnt = jnp.asarray(np.minimum(w, pos + 1), jnp.float32)[None, :, None]
        z = (hi[..., sl] - lo) / cnt - u_new[..., sl]
        outs.append(jnp.einsum('blc,cd->bld', z, w_pool[g].astype(jnp.float32)))
    return jnp.concatenate(outs, axis=-1) * scale.astype(jnp.float32)


def run_layer(h, prompt, S0, buf, norm_g, w_in, w_alpha, b_alpha, gla_gain, w_a,
              pool_w, pool_scale, w_b, b_merge, w_out):
    B, L, _ = h.shape
    dt = h.dtype
    xn = rmsnorm(h, norm_g)
    z = xn @ w_in
    q, k, v, ga, u, gb, alow, mg = jnp.split(z, IN_OFFSETS, axis=-1)
    q = q.reshape(B, L, GLA_HEADS, GLA_DKH) * (GLA_DKH ** -0.5)
    k = k.reshape(B, L, GLA_HEADS, GLA_DKH)
    v = v.reshape(B, L, GLA_HEADS, GLA_DVH)
    la = jax.nn.log_sigmoid(alow.astype(jnp.float32) @ w_alpha.astype(jnp.float32)
                            + b_alpha.astype(jnp.float32)) / GATE_TAU
    la = la.reshape(B, L, GLA_HEADS, GLA_DKH)
    if prompt:
        S_meta = jnp.zeros((B, GLA_HEADS, GLA_DKH, GLA_DVH), jnp.float32)
        o_m, S1 = gla_scan(q[:, :N_META], k[:, :N_META], v[:, :N_META], la[:, :N_META], S_meta, N_META)
        o_r, S_new = gla_scan(q[:, N_META:], k[:, N_META:], v[:, N_META:], la[:, N_META:], S1, GLA_CHUNK)
        o = jnp.concatenate([o_m, o_r], axis=1)
        y_pool = pool_mix(u, 0, 0, pool_w, pool_scale)
        buf_new = u[:, -POOL_BUF:]
    else:
        o, S_new = gla_scan(q, k, v, la, S0, L)
        u_ext = jnp.concatenate([buf.astype(u.dtype), u], axis=1)
        y_pool = pool_mix(u_ext, POOL_BUF, PAST_LEN, pool_w, pool_scale)
        buf_new = u_ext[:, -POOL_BUF:]
    o = o * lax.rsqrt(jnp.mean(o * o, axis=-1, keepdims=True) + EPS)
    o = o * gla_gain.astype(jnp.float32).reshape(GLA_HEADS, GLA_DVH)
    o = o.reshape(B, L, GLA_DV) * jax.nn.silu(ga.astype(jnp.float32))
    ya = o.astype(dt) @ w_a
    yb = (y_pool * jax.nn.silu(gb.astype(jnp.float32))).astype(dt) @ w_b
    gates = jax.nn.sigmoid(mg.astype(jnp.float32) + b_merge.astype(jnp.float32))
    s_a, s_b = jnp.split(gates, N_BRANCH, axis=-1)
    merged = (s_a * ya.astype(jnp.float32) + s_b * yb.astype(jnp.float32)).astype(dt)
    return h + merged @ w_out, S_new, buf_new


def setup_inputs(seed: int = 0) -> dict:
    key = jax.random.key(seed)
    ks = jax.random.split(key, 18)
    f32 = jnp.float32
    nrm = lambda k, s, sc: jax.random.normal(k, s, f32) * sc
    return {
        "x_prompt": nrm(ks[0], (BATCH, SEQ, D_MODEL), 1.0),
        "x_sample": nrm(ks[1], (DEC_BATCH, DEC_SEQ, D_MODEL), 1.0),
        "state_gla": nrm(ks[2], (DEPTH, DEC_BATCH, GLA_HEADS, GLA_DKH, GLA_DVH), 1.0),
        "state_pool": nrm(ks[3], (DEPTH, DEC_BATCH, POOL_BUF, POOL_WIDTH), 1.0),
        "meta_tokens": nrm(ks[4], (N_META, D_MODEL), 1.0),
        "norm_g": 1.0 + nrm(ks[5], (DEPTH, D_MODEL), 0.02),
        "w_in": nrm(ks[6], (DEPTH, D_MODEL, IN_COLS), D_MODEL ** -0.5),
        "w_alpha": nrm(ks[7], (DEPTH, GATE_RANK, GLA_DK), GATE_RANK ** -0.5),
        "b_alpha": 2.0 + nrm(ks[8], (DEPTH, GLA_DK), 0.5),
        "gla_gain": 1.0 + nrm(ks[9], (DEPTH, GLA_DV), 0.02),
        "w_a": nrm(ks[10], (DEPTH, GLA_DV, D_MODEL), GLA_DV ** -0.5),
        "pool_w": nrm(ks[11], (DEPTH, POOL_GROUPS, POOL_GW, POOL_GW), POOL_GW ** -0.5),
        "pool_scale": 1.0 + nrm(ks[12], (DEPTH, POOL_WIDTH), 0.02),
        "w_b": nrm(ks[13], (DEPTH, POOL_WIDTH, D_MODEL), POOL_WIDTH ** -0.5),
        "b_merge": nrm(ks[14], (DEPTH, N_BRANCH * D_MODEL), 0.02),
        "w_out": nrm(ks[15], (DEPTH, D_MODEL, D_MODEL), D_MODEL ** -0.5),
        "final_norm_g": 1.0 + nrm(ks[16], (D_MODEL,), 0.02),
    }


def reference(x_prompt, x_sample, state_gla, state_pool, meta_tokens, norm_g, w_in, w_alpha,
              b_alpha, gla_gain, w_a, pool_w, pool_scale, w_b, b_merge, w_out, final_norm_g):
    B = x_prompt.shape[0]
    meta = jnp.broadcast_to(meta_tokens[None].astype(x_prompt.dtype), (B, N_META, D_MODEL))
    hp = jnp.concatenate([meta, x_prompt], axis=1)
    hs = x_sample
    sg_p, sp_p, sg_s, sp_s = [], [], [], []
    for l in range(DEPTH):
        lw = (norm_g[l], w_in[l], w_alpha[l], b_alpha[l], gla_gain[l], w_a[l],
              pool_w[l], pool_scale[l], w_b[l], b_merge[l], w_out[l])
        hp, S_p, buf_p = run_layer(hp, True, None, None, *lw)
        hs, S_s, buf_s = run_layer(hs, False, state_gla[l], state_pool[l], *lw)
        sg_p.append(S_p)
        sp_p.append(buf_p)
        sg_s.append(S_s)
        sp_s.append(buf_s)
    y_prompt = rmsnorm(hp[:, N_META:], final_norm_g)
    y_sample = rmsnorm(hs, final_norm_g)
    return (y_prompt, y_sample, jnp.stack(sg_p), jnp.stack(sp_p), jnp.stack(sg_s), jnp.stack(sp_s))
```

```python
import functools

import numpy as np
import jax
import jax.numpy as jnp
from jax import lax
from jax.experimental import pallas as pl
from jax.experimental.pallas import tpu as pltpu

F32 = jnp.float32
BF16 = jnp.bfloat16

D_MODEL = 1024
N_META = 16
HEADS = 4
DKH = 128
DVH = 256
DK = HEADS * DKH
DV = HEADS * DVH
GATE_RANK = 16
GATE_TAU = 16.0
POOL_W = 512
POOL_WINDOWS = (2, 4, 8, 16)
POOL_GW = POOL_W // len(POOL_WINDOWS)
POOL_BUF = max(POOL_WINDOWS) - 1
POOL_PAD = POOL_BUF + 1
PAST_LEN = 16384
EPS = 1e-6
IN_SIZES = (DK, DK, DV, DV, POOL_W, POOL_W, GATE_RANK, 2 * D_MODEL)
IN_OFFS = tuple(int(o) for o in np.cumsum((0,) + IN_SIZES))

LANES = 128
SUBLANES = 8
VMEM_LIMIT_BYTES = 56 * 1024 * 1024

SEQ_BLOCK = 256
SEQ_CHUNK = 64
SAMPLE_SUB = 4


def _mm(a, b):
    return jnp.dot(a, b, preferred_element_type=F32)


def _mm_nt(a, b):
    return lax.dot_general(a, b, (((1,), (1,)), ((), ())), preferred_element_type=F32)


def _mm_tn(a, b):
    return lax.dot_general(a, b, (((0,), (0,)), ((), ())), preferred_element_type=F32)


def _sigmoid(x):
    return 1.0 / (1.0 + jnp.exp(-x))


def _silu(x):
    return x * _sigmoid(x)


def _log_sigmoid(x):
    return jnp.minimum(x, 0.0) - jnp.log1p(jnp.exp(-jnp.abs(x)))


def _rmsnorm(x, g):
    ms = jnp.mean(x * x, axis=-1, keepdims=True)
    return x * lax.rsqrt(ms + EPS) * g


def _row_to_col(row, eye):
    return jnp.sum(eye * row, axis=1, keepdims=True)


def _eye(n):
    r = lax.broadcasted_iota(jnp.int32, (n, n), 0)
    c = lax.broadcasted_iota(jnp.int32, (n, n), 1)
    return (r == c).astype(F32)


W_NAMES = ("norm_g", "w_q", "w_k", "w_v", "w_ga", "w_u", "w_gb", "w_al", "w_mg", "w_alpha",
           "b_alpha", "gla_gain", "w_a", "pool_w", "pool_scale", "w_b", "b_merge", "w_out",
           "final_g")


def _dense_in(x, w):
    xn = _rmsnorm(x, w["norm_g"][...]).astype(BF16)
    q = _mm(xn, w["w_q"][...]) * (DKH ** -0.5)
    k = _mm(xn, w["w_k"][...])
    v = _mm(xn, w["w_v"][...])
    u = _mm(xn, w["w_u"][...])
    alow = _mm(xn, w["w_al"][...])
    a = _mm(alow.astype(BF16), w["w_alpha"][...]) + w["b_alpha"][...]
    la = _log_sigmoid(a) * (1.0 / GATE_TAU)
    return xn, q, k, v, la, u


def _dense_out(x, xn, o, z, w, is_last):
    ga = _mm(xn, w["w_ga"][...])
    gain = w["gla_gain"][...]
    parts = []
    for h in range(HEADS):
        oh = o[:, h * DVH:(h + 1) * DVH]
        ms = jnp.mean(oh * oh, axis=-1, keepdims=True)
        parts.append(oh * lax.rsqrt(ms + EPS) * gain[:, h * DVH:(h + 1) * DVH])
    on = jnp.concatenate(parts, axis=-1) * _silu(ga)
    ya = _mm(on.astype(BF16), w["w_a"][...])

    gb = _mm(xn, w["w_gb"][...])
    yp = []
    for g in range(len(POOL_WINDOWS)):
        zg = z[:, g * POOL_GW:(g + 1) * POOL_GW].astype(BF16)
        yp.append(_mm(zg, w["pool_w"][g]))
    y_pool = jnp.concatenate(yp, axis=-1) * w["pool_scale"][...]
    yb = _mm((y_pool * _silu(gb)).astype(BF16), w["w_b"][...])

    mg = _mm(xn, w["w_mg"][...]) + w["b_merge"][...]
    gates = _sigmoid(mg)
    merged = gates[:, :D_MODEL] * ya + gates[:, D_MODEL:] * yb
    h_new = x + _mm(merged.astype(BF16), w["w_out"][...])
    y = _rmsnorm(h_new, w["final_g"][...]) if is_last else None
    return h_new, y


def _window_sums(e):
    p1 = e + pltpu.roll(e, 1, 0)
    t = p1[:, POOL_GW:]
    p2 = t + pltpu.roll(t, 2, 0)
    t = p2[:, POOL_GW:]
    p3 = t + pltpu.roll(t, 4, 0)
    t = p3[:, POOL_GW:]
    p4 = t + pltpu.roll(t, 8, 0)
    return jnp.concatenate([p1[:, :POOL_GW], p2[:, :POOL_GW], p3[:, :POOL_GW], p4], axis=-1)


def _split_hi_lo(x):
    hi = x.astype(BF16)
    lo = (x - hi.astype(F32)).astype(BF16)
    return hi, lo


def _seq_kernel(*refs, tb, chunk, is_last, pow2_counts):
    n_w = len(W_NAMES)
    h_ref, s0_ref, buf0_ref, cnt_ref = refs[:4]
    w = dict(zip(W_NAMES, refs[4:4 + n_w]))
    outs = refs[4 + n_w:]
    if is_last:
        hout_ref, y_ref, s_ref, bufn_ref = outs[:4]
        scr = outs[4:]
    else:
        hout_ref, s_ref, bufn_ref = outs[:3]
        y_ref = None
        scr = outs[3:]
    o_s, ext_s = scr

    t = pl.program_id(1)

    @pl.when(t == 0)
    def _():
        s_ref[0] = s0_ref[0]
        ext_s[0:POOL_PAD, :] = buf0_ref[0]

    x = h_ref[0]
    xn, q, k, v, la, u = _dense_in(x, w)
    vb = v.astype(BF16)

    n_chunks = tb // chunk
    ri = lax.broadcasted_iota(jnp.int32, (chunk, chunk), 0)
    ci = lax.broadcasted_iota(jnp.int32, (chunk, chunk), 1)
    causal = ri >= ci
    tril = causal.astype(BF16)
    eye = _eye(DKH)
    for c in range(n_chunks):
        rows = slice(c * chunk, (c + 1) * chunk)
        la_hi, la_lo = _split_hi_lo(la[rows])
        b = _mm(tril, la_hi) + _mm(tril, la_lo)
        b_end = b[chunk - 1:chunk, :]
        qd = q[rows] * jnp.exp(b)
        kinv = k[rows] * jnp.exp(-b)
        kd = k[rows] * jnp.exp(b_end - b)
        dec = jnp.exp(b_end)
        for h in range(HEADS):
            ks = slice(h * DKH, (h + 1) * DKH)
            vs = slice(h * DVH, (h + 1) * DVH)
            s_old = s_ref[0, h]
            qh = qd[:, ks].astype(BF16)
            a = _mm_nt(qh, kinv[:, ks].astype(BF16))
            a = jnp.where(causal, a, 0.0)
            vh = vb[rows, vs]
            o_s[rows, vs] = _mm(a.astype(BF16), vh) + _mm(qh, s_old.astype(BF16))
            ds = _mm_tn(kd[:, ks].astype(BF16), vh)
            s_ref[0, h] = _row_to_col(dec[:, ks], eye) * s_old + ds

    ext_s[POOL_PAD:POOL_PAD + tb, :] = u
    sums = _window_sums(ext_s[...])[POOL_PAD:, :]
    if pow2_counts:
        z = sums * cnt_ref[...] - u
    else:
        z = sums / cnt_ref[...] - u
    bufn_ref[0] = ext_s[tb:tb + POOL_PAD, :]
    ext_s[0:POOL_PAD, :] = ext_s[tb:tb + POOL_PAD, :]

    h_new, y = _dense_out(x, xn, o_s[...], z, w, is_last)
    hout_ref[0] = h_new
    if is_last:
        y_ref[0] = y


def _const_spec(shape):
    nd = len(shape)
    return pl.BlockSpec(shape, lambda *_: (0,) * nd, pipeline_mode=pl.Buffered(1))


def _seq_layer(h, s0, buf0, cnt, wl, *, is_last, pow2_counts):
    bn, t_len, _ = h.shape
    tb = min(SEQ_BLOCK, t_len)
    chunk = min(SEQ_CHUNK, tb)
    assert t_len % tb == 0 and tb % chunk == 0
    shared = s0.shape[0] == 1
    bsel = (lambda b, t: (0, 0, 0, 0)) if shared else (lambda b, t: (b, 0, 0, 0))
    bsel3 = (lambda b, t: (0, 0, 0)) if shared else (lambda b, t: (b, 0, 0))
    weights = [wl[n] for n in W_NAMES]
    in_specs = [
        pl.BlockSpec((1, tb, D_MODEL), lambda b, t: (b, t, 0)),
        pl.BlockSpec((1, HEADS, DKH, DVH), bsel),
        pl.BlockSpec((1, POOL_PAD, POOL_W), bsel3),
        _const_spec(cnt.shape),
    ] + [_const_spec(a.shape) for a in weights]
    h_spec = pl.BlockSpec((1, tb, D_MODEL), lambda b, t: (b, t, 0))
    out_shape = [jax.ShapeDtypeStruct(h.shape, F32)]
    out_specs = [h_spec]
    if is_last:
        out_shape.append(jax.ShapeDtypeStruct(h.shape, F32))
        out_specs.append(h_spec)
    out_shape += [jax.ShapeDtypeStruct((bn, HEADS, DKH, DVH), F32),
                  jax.ShapeDtypeStruct((bn, POOL_PAD, POOL_W), F32)]
    out_specs += [pl.BlockSpec((1, HEADS, DKH, DVH), lambda b, t: (b, 0, 0, 0)),
                  pl.BlockSpec((1, POOL_PAD, POOL_W), lambda b, t: (b, 0, 0))]
    kern = functools.partial(_seq_kernel, tb=tb, chunk=chunk, is_last=is_last,
                             pow2_counts=pow2_counts)
    return pl.pallas_call(
        kern,
        grid=(bn, t_len // tb),
        in_specs=in_specs,
        out_specs=out_specs,
        out_shape=out_shape,
        scratch_shapes=[pltpu.VMEM((tb, DV), F32), pltpu.VMEM((tb + POOL_PAD, POOL_W), F32)],
        compiler_params=pltpu.CompilerParams(
            dimension_semantics=("arbitrary", "arbitrary"), vmem_limit_bytes=VMEM_LIMIT_BYTES),
        name="seq_layer",
    )(h, s0, buf0, cnt, *weights)


def _step_kernel(*refs, n_tok, sub, is_last):
    n_w = len(W_NAMES)
    h_ref, s0_ref, buf0_ref, cnt_ref = refs[:4]
    w = dict(zip(W_NAMES, refs[4:4 + n_w]))
    outs = refs[4 + n_w:]
    if is_last:
        hout_ref, y_ref, s_ref, bufn_ref = outs[:4]
        scr = outs[4:]
    else:
        hout_ref, s_ref, bufn_ref = outs[:3]
        y_ref = None
        scr = outs[3:]
    xn_s, q_s, k_s, v_s, la_s, u_s, o_s, z_s = scr

    j = pl.program_id(0)
    nj = pl.num_programs(0)
    pair = SUBLANES // n_tok
    assert pair == 2 and sub % pair == 0

    @pl.when(j == 0)
    def _():
        xn, q, k, v, la, u = _dense_in(h_ref[...], w)
        xn_s[...] = xn
        q_s[...] = q
        k_s[...] = k
        v_s[...] = v
        la_s[...] = la
        u_s[...] = u

    row = lax.broadcasted_iota(jnp.int32, (SUBLANES, 1), 0)
    tok = row % n_tok
    first = row < n_tok
    ri = lax.broadcasted_iota(jnp.int32, (SUBLANES, SUBLANES), 0)
    ci = lax.broadcasted_iota(jnp.int32, (SUBLANES, SUBLANES), 1)
    amask = (ri >= ci) & ((ri // n_tok) == (ci // n_tok))
    eye = _eye(DKH)
    inv_cnt = cnt_ref[...]

    for p in range(sub // pair):
        r0 = pl.multiple_of((j * (sub // pair) + p) * SUBLANES, SUBLANES)
        rows = pl.ds(r0, SUBLANES)
        la = la_s[rows, :]
        b = la + jnp.where(tok >= 1, pltpu.roll(la, 1, 0), 0.0)
        b = b + jnp.where(tok >= 2, pltpu.roll(b, 2, 0), 0.0)
        b_end0 = b[n_tok - 1:n_tok, :]
        b_end1 = b[2 * n_tok - 1:2 * n_tok, :]
        b_end = jnp.where(first, b_end0, b_end1)
        q = q_s[rows, :]
        k = k_s[rows, :]
        v = v_s[rows, :]
        qd = q * jnp.exp(b)
        kinv = k * jnp.exp(-b)
        kd = k * jnp.exp(b_end - b)
        kd0 = jnp.where(first, kd, 0.0)
        kd1 = jnp.where(first, 0.0, kd)
        dec0 = jnp.exp(b_end0)
        dec1 = jnp.exp(b_end1)
        o_parts = []
        for h in range(HEADS):
            ks = slice(h * DKH, (h + 1) * DKH)
            vs = slice(h * DVH, (h + 1) * DVH)
            qh = qd[:, ks]
            vh = v[:, vs]
            a = jnp.where(amask, _mm_nt(qh, kinv[:, ks]), 0.0)
            s_a = s0_ref[pair * p, h]
            s_b = s0_ref[pair * p + 1, h]
            o = _mm(a, vh) + jnp.where(first, _mm(qh, s_a), _mm(qh, s_b))
            o_parts.append(o)
            s_ref[pair * p, h] = _row_to_col(dec0[:, ks], eye) * s_a + _mm_tn(kd0[:, ks], vh)
            s_ref[pair * p + 1, h] = _row_to_col(dec1[:, ks], eye) * s_b + _mm_tn(kd1[:, ks], vh)
        o_s[rows, :] = jnp.concatenate(o_parts, axis=-1)

        u = u_s[rows, :]
        e0 = jnp.concatenate([buf0_ref[pair * p], u], axis=0)
        e1 = jnp.concatenate([buf0_ref[pair * p + 1], pltpu.roll(u, n_tok, 0)], axis=0)
        n_ext = POOL_PAD + SUBLANES
        w0 = _window_sums(e0)[POOL_PAD:, :]
        w1 = _window_sums(e1)[POOL_PAD:, :]
        sums = jnp.where(first, w0, pltpu.roll(w1, n_tok, 0))
        z_s[rows, :] = sums * inv_cnt - u
        bufn_ref[pair * p] = pltpu.roll(e0, n_ext - n_tok, 0)[0:POOL_PAD, :]
        bufn_ref[pair * p + 1] = pltpu.roll(e1, n_ext - n_tok, 0)[0:POOL_PAD, :]

    @pl.when(j == nj - 1)
    def _():
        h_new, y = _dense_out(h_ref[...], xn_s[...], o_s[...], z_s[...], w, is_last)
        hout_ref[...] = h_new
        if is_last:
            y_ref[...] = y


def _step_layer(h, s0, buf0, cnt, wl, *, n_tok, is_last):
    rows, _ = h.shape
    bn = rows // n_tok
    sub = SAMPLE_SUB
    assert bn % sub == 0
    weights = [wl[n] for n in W_NAMES]
    in_specs = [
        pl.BlockSpec((rows, D_MODEL), lambda j: (0, 0)),
        pl.BlockSpec((sub, HEADS, DKH, DVH), lambda j: (j, 0, 0, 0)),
        pl.BlockSpec((sub, POOL_PAD, POOL_W), lambda j: (j, 0, 0)),
        _const_spec(cnt.shape),
    ] + [_const_spec(a.shape) for a in weights]
    h_spec = pl.BlockSpec((rows, D_MODEL), lambda j: (0, 0))
    out_shape = [jax.ShapeDtypeStruct(h.shape, F32)]
    out_specs = [h_spec]
    if is_last:
        out_shape.append(jax.ShapeDtypeStruct(h.shape, F32))
        out_specs.append(h_spec)
    out_shape += [jax.ShapeDtypeStruct((bn, HEADS, DKH, DVH), F32),
                  jax.ShapeDtypeStruct((bn, POOL_PAD, POOL_W), F32)]
    out_specs += [pl.BlockSpec((sub, HEADS, DKH, DVH), lambda j: (j, 0, 0, 0)),
                  pl.BlockSpec((sub, POOL_PAD, POOL_W), lambda j: (j, 0, 0))]
    kern = functools.partial(_step_kernel, n_tok=n_tok, sub=sub, is_last=is_last)
    scratch = [pltpu.VMEM((rows, D_MODEL), BF16), pltpu.VMEM((rows, DK), F32),
               pltpu.VMEM((rows, DK), F32), pltpu.VMEM((rows, DV), F32),
               pltpu.VMEM((rows, DK), F32), pltpu.VMEM((rows, POOL_W), F32),
               pltpu.VMEM((rows, DV), F32), pltpu.VMEM((rows, POOL_W), F32)]
    return pl.pallas_call(
        kern,
        grid=(bn // sub,),
        in_specs=in_specs,
        out_specs=out_specs,
        out_shape=out_shape,
        scratch_shapes=scratch,
        compiler_params=pltpu.CompilerParams(
            dimension_semantics=("arbitrary",), vmem_limit_bytes=VMEM_LIMIT_BYTES),
        name="step_layer",
    )(h, s0, buf0, cnt, *weights)


def _layer_weights(l, norm_g, w_in, w_alpha, b_alpha, gla_gain, w_a, pool_w, pool_scale, w_b,
                   b_merge, w_out, final_norm_g):
    wi = w_in[l].astype(BF16)
    seg = [wi[:, IN_OFFS[i]:IN_OFFS[i + 1]] for i in range(len(IN_SIZES))]
    pad = LANES - GATE_RANK
    return {
        "norm_g": norm_g[l][None, :],
        "w_q": seg[0], "w_k": seg[1], "w_v": seg[2], "w_ga": seg[3], "w_u": seg[4],
        "w_gb": seg[5],
        "w_al": jnp.pad(seg[6], ((0, 0), (0, pad))),
        "w_mg": seg[7],
        "w_alpha": jnp.pad(w_alpha[l].astype(BF16), ((0, pad), (0, 0))),
        "b_alpha": b_alpha[l][None, :],
        "gla_gain": gla_gain[l][None, :],
        "w_a": w_a[l].astype(BF16),
        "pool_w": pool_w[l].astype(BF16),
        "pool_scale": pool_scale[l][None, :],
        "w_b": w_b[l].astype(BF16),
        "b_merge": b_merge[l][None, :],
        "w_out": w_out[l].astype(BF16),
        "final_g": final_norm_g[None, :],
    }


def _window_row():
    return np.repeat(np.asarray(POOL_WINDOWS, np.float32), POOL_GW)[None, :]


def kernel(x_prompt, x_sample, state_gla, state_pool, meta_tokens, norm_g, w_in, w_alpha, b_alpha,
           gla_gain, w_a, pool_w, pool_scale, w_b, b_merge, w_out, final_norm_g):
    depth = w_in.shape[0]
    dec_b, dec_t, _ = x_sample.shape
    wrow = _window_row()
    cnt_meta = jnp.asarray(np.minimum(wrow, np.arange(1, N_META + 1, dtype=np.float32)[:, None]))
    inv_w = jnp.asarray(1.0 / wrow)

    h_meta = meta_tokens[None].astype(F32)
    h_p = x_prompt
    h_s = x_sample.reshape(dec_b * dec_t, D_MODEL)
    zero_s = jnp.zeros((1, HEADS, DKH, DVH), F32)
    zero_buf = jnp.zeros((1, POOL_PAD, POOL_W), F32)
    buf_s = jnp.pad(state_pool, ((0, 0), (0, 0), (POOL_PAD - POOL_BUF, 0), (0, 0)))

    sg_p, sp_p, sg_s, sp_s = [], [], [], []
    y_p = y_s = None
    for l in range(depth):
        last = l == depth - 1
        wl = _layer_weights(l, norm_g, w_in, w_alpha, b_alpha, gla_gain, w_a, pool_w, pool_scale,
                            w_b, b_merge, w_out, final_norm_g)
        m_out = _seq_layer(h_meta, zero_s, zero_buf, cnt_meta, wl, is_last=False,
                           pow2_counts=False)
        h_meta, s_meta, buf_meta = m_out
        p_out = _seq_layer(h_p, s_meta, buf_meta, inv_w, wl, is_last=last, pow2_counts=True)
        s_out = _step_layer(h_s, state_gla[l], buf_s[l], inv_w, wl, n_tok=dec_t, is_last=last)
        if last:
            h_p, y_p, s_p, b_p = p_out
            h_s, y_s, s_s, b_s = s_out
        else:
            h_p, s_p, b_p = p_out
            h_s, s_s, b_s = s_out
        sg_p.append(s_p)
        sp_p.append(b_p[:, POOL_PAD - POOL_BUF:, :])
        sg_s.append(s_s)
        sp_s.append(b_s[:, POOL_PAD - POOL_BUF:, :])
    return (y_p, y_s.reshape(dec_b, dec_t, D_MODEL), jnp.stack(sg_p), jnp.stack(sp_p),
            jnp.stack(sg_s), jnp.stack(sp_s))
```

```python
import functools

import numpy as np
import jax
import jax.numpy as jnp
from jax import lax
from jax.experimental import pallas as pl
from jax.experimental.pallas import tpu as pltpu

F32 = jnp.float32
BF16 = jnp.bfloat16

D_MODEL = 1024
N_META = 16
HEADS = 4
DKH = 128
DVH = 256
DK = HEADS * DKH
DV = HEADS * DVH
GATE_RANK = 16
GATE_TAU = 16.0
POOL_W = 512
POOL_WINDOWS = (2, 4, 8, 16)
POOL_GW = POOL_W // len(POOL_WINDOWS)
POOL_BUF = max(POOL_WINDOWS) - 1
POOL_PAD = POOL_BUF + 1
PAST_LEN = 16384
EPS = 1e-6
IN_SIZES = (DK, DK, DV, DV, POOL_W, POOL_W, GATE_RANK, 2 * D_MODEL)
IN_OFFS = tuple(int(o) for o in np.cumsum((0,) + IN_SIZES))

LANES = 128
SUBLANES = 8
VMEM_LIMIT_BYTES = 56 * 1024 * 1024

SEQ_BLOCK = 256
SEQ_CHUNK = 64
SAMPLE_BLOCK = 32
SAMPLE_SUB = 4


def _mm(a, b):
    return jnp.dot(a, b, preferred_element_type=F32)


def _mm_nt(a, b):
    return lax.dot_general(a, b, (((1,), (1,)), ((), ())), preferred_element_type=F32)


def _mm_tn(a, b):
    return lax.dot_general(a, b, (((0,), (0,)), ((), ())), preferred_element_type=F32)


def _sigmoid(x):
    return 1.0 / (1.0 + jnp.exp(-x))


def _silu(x):
    return x * _sigmoid(x)


def _log_sigmoid(x):
    return jnp.minimum(x, 0.0) - jnp.log1p(jnp.exp(-jnp.abs(x)))


def _rmsnorm(x, g):
    ms = jnp.mean(x * x, axis=-1, keepdims=True)
    return x * lax.rsqrt(ms + EPS) * g


def _row_to_col(row, eye):
    return jnp.sum(eye * row, axis=1, keepdims=True)


def _eye(n):
    r = lax.broadcasted_iota(jnp.int32, (n, n), 0)
    c = lax.broadcasted_iota(jnp.int32, (n, n), 1)
    return (r == c).astype(F32)


W_NAMES = ("norm_g", "w_q", "w_k", "w_v", "w_ga", "w_u", "w_gb", "w_al", "w_mg", "w_alpha",
           "b_alpha", "gla_gain", "w_a", "pool_w", "pool_scale", "w_b", "b_merge", "w_out",
           "final_g")


def _dense_in(x, w):
    xn = _rmsnorm(x, w["norm_g"][...]).astype(BF16)
    q = _mm(xn, w["w_q"][...]) * (DKH ** -0.5)
    k = _mm(xn, w["w_k"][...])
    v = _mm(xn, w["w_v"][...])
    u = _mm(xn, w["w_u"][...])
    alow = _mm(xn, w["w_al"][...])
    a = _mm(alow.astype(BF16), w["w_alpha"][...]) + w["b_alpha"][...]
    la = _log_sigmoid(a) * (1.0 / GATE_TAU)
    return xn, q, k, v, la, u


def _dense_out(x, xn, o, z, w, is_last, pre=None):
    if pre is None:
        pre = (_mm(xn, w["w_ga"][...]), _mm(xn, w["w_gb"][...]), _mm(xn, w["w_mg"][...]))
    ga, gb, mg = pre
    gain = w["gla_gain"][...]
    parts = []
    for h in range(HEADS):
        oh = o[:, h * DVH:(h + 1) * DVH]
        ms = jnp.mean(oh * oh, axis=-1, keepdims=True)
        parts.append(oh * lax.rsqrt(ms + EPS) * gain[:, h * DVH:(h + 1) * DVH])
    on = jnp.concatenate(parts, axis=-1) * _silu(ga)
    ya = _mm(on.astype(BF16), w["w_a"][...])

    yp = []
    for g in range(len(POOL_WINDOWS)):
        zg = z[:, g * POOL_GW:(g + 1) * POOL_GW].astype(BF16)
        yp.append(_mm(zg, w["pool_w"][g]))
    y_pool = jnp.concatenate(yp, axis=-1) * w["pool_scale"][...]
    yb = _mm((y_pool * _silu(gb)).astype(BF16), w["w_b"][...])

    gates = _sigmoid(mg + w["b_merge"][...])
    merged = gates[:, :D_MODEL] * ya + gates[:, D_MODEL:] * yb
    h_new = x + _mm(merged.astype(BF16), w["w_out"][...])
    y = _rmsnorm(h_new, w["final_g"][...]) if is_last else None
    return h_new, y


def _window_sums(e):
    p1 = e + pltpu.roll(e, 1, 0)
    t = p1[:, POOL_GW:]
    p2 = t + pltpu.roll(t, 2, 0)
    t = p2[:, POOL_GW:]
    p3 = t + pltpu.roll(t, 4, 0)
    t = p3[:, POOL_GW:]
    p4 = t + pltpu.roll(t, 8, 0)
    return jnp.concatenate([p1[:, :POOL_GW], p2[:, :POOL_GW], p3[:, :POOL_GW], p4], axis=-1)


def _split_hi_lo(x):
    hi = x.astype(BF16)
    lo = (x - hi.astype(F32)).astype(BF16)
    return hi, lo


def _seq_kernel(*refs, tb, chunk, is_last, pow2_counts):
    n_w = len(W_NAMES)
    h_ref, s0_ref, buf0_ref, cnt_ref = refs[:4]
    w = dict(zip(W_NAMES, refs[4:4 + n_w]))
    hy_ref, s_ref, bufn_ref, o_s, ext_s = refs[4 + n_w:]

    t = pl.program_id(1)

    @pl.when(t == 0)
    def _():
        s_ref[0] = s0_ref[0]
        ext_s[0:POOL_PAD, :] = buf0_ref[0]

    x = h_ref[0]
    xn, q, k, v, la, u = _dense_in(x, w)
    vb = v.astype(BF16)

    n_chunks = tb // chunk
    ri = lax.broadcasted_iota(jnp.int32, (tb, tb), 0)
    ci = lax.broadcasted_iota(jnp.int32, (tb, tb), 1)
    tril = ((ri >= ci) & ((ri // chunk) == (ci // chunk))).astype(BF16)
    causal = (lax.broadcasted_iota(jnp.int32, (chunk, chunk), 0)
              >= lax.broadcasted_iota(jnp.int32, (chunk, chunk), 1))
    eye = _eye(DKH)
    la_hi, la_lo = _split_hi_lo(la)
    b = _mm(tril, la_hi) + _mm(tril, la_lo)
    decs = [jnp.exp(b[(c + 1) * chunk - 1:(c + 1) * chunk, :]) for c in range(n_chunks)]
    to_end = jnp.concatenate(
        [b[(c + 1) * chunk - 1:(c + 1) * chunk, :] - b[c * chunk:(c + 1) * chunk, :]
         for c in range(n_chunks)], axis=0)
    qd = (q * jnp.exp(b)).astype(BF16)
    kinv = (k * jnp.exp(-b)).astype(BF16)
    kd = (k * jnp.exp(to_end)).astype(BF16)
    chunks = [(c, h) for c in range(n_chunks) for h in range(HEADS)]

    def sl(c, h):
        return (slice(c * chunk, (c + 1) * chunk), slice(h * DKH, (h + 1) * DKH),
                slice(h * DVH, (h + 1) * DVH))

    a_m = {}
    for c, h in chunks:
        rows, ks, _ = sl(c, h)
        a = _mm_nt(qd[rows, ks], kinv[rows, ks])
        a_m[c, h] = jnp.where(causal, a, 0.0).astype(BF16)
    ga = _mm(xn, w["w_ga"][...])
    o_in, d_s, dcol = {}, {}, {}
    for c, h in chunks:
        rows, ks, vs = sl(c, h)
        o_in[c, h] = _mm(a_m[c, h], vb[rows, vs])
        d_s[c, h] = _mm_tn(kd[rows, ks], vb[rows, vs])
        dcol[c, h] = _row_to_col(decs[c][:, ks], eye)
    gb = _mm(xn, w["w_gb"][...])

    s_cur = [s_ref[0, h] for h in range(HEADS)]
    mg_parts = []
    mg_cols = 2 * D_MODEL // n_chunks
    for c in range(n_chunks):
        for h in range(HEADS):
            rows, ks, vs = sl(c, h)
            o_s[rows, vs] = o_in[c, h] + _mm(qd[rows, ks], s_cur[h].astype(BF16))
            s_cur[h] = dcol[c, h] * s_cur[h] + d_s[c, h]
        cs = slice(c * mg_cols, (c + 1) * mg_cols)
        mg_parts.append(_mm(xn, w["w_mg"][:, cs]))
    for h in range(HEADS):
        s_ref[0, h] = s_cur[h]
    mg = jnp.concatenate(mg_parts, axis=-1)

    ext_s[POOL_PAD:POOL_PAD + tb, :] = u
    sums = _window_sums(ext_s[...])[POOL_PAD:, :]
    if pow2_counts:
        z = sums * cnt_ref[...] - u
    else:
        z = sums / cnt_ref[...] - u
    bufn_ref[0] = ext_s[tb:tb + POOL_PAD, :]
    ext_s[0:POOL_PAD, :] = ext_s[tb:tb + POOL_PAD, :]

    h_new, y = _dense_out(x, xn, o_s[...], z, w, is_last, pre=(ga, gb, mg))
    hy_ref[0] = y if is_last else h_new


def _const_spec(shape):
    nd = len(shape)
    return pl.BlockSpec(shape, lambda *_: (0,) * nd, pipeline_mode=pl.Buffered(1))


def _seq_layer(h, s0, buf0, cnt, wl, *, is_last, pow2_counts):
    bn, t_len, _ = h.shape
    tb = min(SEQ_BLOCK, t_len)
    chunk = min(SEQ_CHUNK, tb)
    assert t_len % tb == 0 and tb % chunk == 0
    shared = s0.shape[0] == 1
    bsel = (lambda b, t: (0, 0, 0, 0)) if shared else (lambda b, t: (b, 0, 0, 0))
    bsel3 = (lambda b, t: (0, 0, 0)) if shared else (lambda b, t: (b, 0, 0))
    weights = [wl[n] for n in W_NAMES]
    in_specs = [
        pl.BlockSpec((1, tb, D_MODEL), lambda b, t: (b, t, 0)),
        pl.BlockSpec((1, HEADS, DKH, DVH), bsel),
        pl.BlockSpec((1, POOL_PAD, POOL_W), bsel3),
        _const_spec(cnt.shape),
    ] + [_const_spec(a.shape) for a in weights]
    out_shape = [jax.ShapeDtypeStruct(h.shape, F32),
                 jax.ShapeDtypeStruct((bn, HEADS, DKH, DVH), F32),
                 jax.ShapeDtypeStruct((bn, POOL_PAD, POOL_W), F32)]
    out_specs = [pl.BlockSpec((1, tb, D_MODEL), lambda b, t: (b, t, 0)),
                 pl.BlockSpec((1, HEADS, DKH, DVH), lambda b, t: (b, 0, 0, 0)),
                 pl.BlockSpec((1, POOL_PAD, POOL_W), lambda b, t: (b, 0, 0))]
    kern = functools.partial(_seq_kernel, tb=tb, chunk=chunk, is_last=is_last,
                             pow2_counts=pow2_counts)
    return pl.pallas_call(
        kern,
        grid=(bn, t_len // tb),
        in_specs=in_specs,
        out_specs=out_specs,
        out_shape=out_shape,
        scratch_shapes=[pltpu.VMEM((tb, DV), F32), pltpu.VMEM((tb + POOL_PAD, POOL_W), F32)],
        compiler_params=pltpu.CompilerParams(
            dimension_semantics=("arbitrary", "arbitrary"), vmem_limit_bytes=VMEM_LIMIT_BYTES),
        name="seq_layer",
    )(h, s0, buf0, cnt, *weights)


def _step_kernel(*refs, n_tok, sub, layer, is_last):
    n_w = len(W_NAMES)
    n_in = 5 if layer else 4
    h_ref, s0_ref, buf0_ref, cnt_ref = refs[:4]
    sprev_ref = refs[4] if layer else None
    w = dict(zip(W_NAMES, refs[n_in:n_in + n_w]))
    hy_ref, sall_ref, bufn_ref = refs[n_in + n_w:n_in + n_w + 3]
    xn_s, q_s, k_s, v_s, la_s, u_s, o_s, z_s = refs[n_in + n_w + 3:]

    j = pl.program_id(1)
    nj = pl.num_programs(1)
    pair = SUBLANES // n_tok
    assert pair == 2 and sub % pair == 0
    if layer:
        sall_ref[0:layer] = sprev_ref[...]
    s_ref = sall_ref.at[layer]

    @pl.when(j == 0)
    def _():
        xn, q, k, v, la, u = _dense_in(h_ref[...], w)
        xn_s[...] = xn
        q_s[...] = q
        k_s[...] = k
        v_s[...] = v
        la_s[...] = la
        u_s[...] = u

    row = lax.broadcasted_iota(jnp.int32, (SUBLANES, 1), 0)
    tok = row % n_tok
    first = row < n_tok
    ri = lax.broadcasted_iota(jnp.int32, (SUBLANES, SUBLANES), 0)
    ci = lax.broadcasted_iota(jnp.int32, (SUBLANES, SUBLANES), 1)
    amask = (ri >= ci) & ((ri // n_tok) == (ci // n_tok))
    eye = _eye(DKH)
    inv_cnt = cnt_ref[...]

    for p in range(sub // pair):
        r0 = pl.multiple_of((j * (sub // pair) + p) * SUBLANES, SUBLANES)
        rows = pl.ds(r0, SUBLANES)
        la = la_s[rows, :]
        b = la + jnp.where(tok >= 1, pltpu.roll(la, 1, 0), 0.0)
        b = b + jnp.where(tok >= 2, pltpu.roll(b, 2, 0), 0.0)
        b_end0 = b[n_tok - 1:n_tok, :]
        b_end1 = b[2 * n_tok - 1:2 * n_tok, :]
        b_end = jnp.where(first, b_end0, b_end1)
        q = q_s[rows, :]
        k = k_s[rows, :]
        v = v_s[rows, :]
        qd = q * jnp.exp(b)
        kinv = k * jnp.exp(-b)
        kd = k * jnp.exp(b_end - b)
        kd0 = jnp.where(first, kd, 0.0)
        kd1 = jnp.where(first, 0.0, kd)
        dec0 = jnp.exp(b_end0)
        dec1 = jnp.exp(b_end1)
        o_parts = []
        for h in range(HEADS):
            ks = slice(h * DKH, (h + 1) * DKH)
            vs = slice(h * DVH, (h + 1) * DVH)
            qh = qd[:, ks]
            vh = v[:, vs]
            a = jnp.where(amask, _mm_nt(qh, kinv[:, ks]), 0.0)
            s_a = s0_ref[pair * p, h]
            s_b = s0_ref[pair * p + 1, h]
            o = _mm(a, vh) + jnp.where(first, _mm(qh, s_a), _mm(qh, s_b))
            o_parts.append(o)
            s_ref[pair * p, h] = _row_to_col(dec0[:, ks], eye) * s_a + _mm_tn(kd0[:, ks], vh)
            s_ref[pair * p + 1, h] = _row_to_col(dec1[:, ks], eye) * s_b + _mm_tn(kd1[:, ks], vh)
        o_s[rows, :] = jnp.concatenate(o_parts, axis=-1)

        u = u_s[rows, :]
        e0 = jnp.concatenate([buf0_ref[pair * p], u], axis=0)
        e1 = jnp.concatenate([buf0_ref[pair * p + 1], pltpu.roll(u, n_tok, 0)], axis=0)
        n_ext = POOL_PAD + SUBLANES
        w0 = _window_sums(e0)[POOL_PAD:, :]
        w1 = _window_sums(e1)[POOL_PAD:, :]
        sums = jnp.where(first, w0, pltpu.roll(w1, n_tok, 0))
        z_s[rows, :] = sums * inv_cnt - u
        bufn_ref[pair * p] = pltpu.roll(e0, n_ext - n_tok, 0)[0:POOL_PAD, :]
        bufn_ref[pair * p + 1] = pltpu.roll(e1, n_ext - n_tok, 0)[0:POOL_PAD, :]

    @pl.when(j == nj - 1)
    def _():
        h_new, y = _dense_out(h_ref[...], xn_s[...], o_s[...], z_s[...], w, is_last)
        hy_ref[...] = y if is_last else h_new


def _step_layer(h, s_all, s_prev, buf0, cnt, wl, *, n_tok, layer, is_last):
    rows, _ = h.shape
    bn = rows // n_tok
    sub, blk = SAMPLE_SUB, SAMPLE_BLOCK
    assert bn % blk == 0 and blk % sub == 0
    n_inner = blk // sub
    brows = blk * n_tok
    weights = [wl[n] for n in W_NAMES]
    in_specs = [
        pl.BlockSpec((brows, D_MODEL), lambda i, j: (i, 0)),
        pl.BlockSpec((None, sub, HEADS, DKH, DVH), lambda i, j: (layer, i * n_inner + j, 0, 0, 0)),
        pl.BlockSpec((sub, POOL_PAD, POOL_W), lambda i, j: (i * n_inner + j, 0, 0)),
        _const_spec(cnt.shape),
    ]
    args = [h, s_all, buf0, cnt]
    if layer:
        in_specs.append(pl.BlockSpec((layer, sub, HEADS, DKH, DVH),
                                     lambda i, j: (0, i * n_inner + j, 0, 0, 0)))
        args.append(s_prev)
    in_specs += [_const_spec(a.shape) for a in weights]
    out_shape = [jax.ShapeDtypeStruct(h.shape, F32),
                 jax.ShapeDtypeStruct((layer + 1, bn, HEADS, DKH, DVH), F32),
                 jax.ShapeDtypeStruct((bn, POOL_PAD, POOL_W), F32)]
    out_specs = [pl.BlockSpec((brows, D_MODEL), lambda i, j: (i, 0)),
                 pl.BlockSpec((layer + 1, sub, HEADS, DKH, DVH),
                              lambda i, j: (0, i * n_inner + j, 0, 0, 0)),
                 pl.BlockSpec((sub, POOL_PAD, POOL_W), lambda i, j: (i * n_inner + j, 0, 0))]
    kern = functools.partial(_step_kernel, n_tok=n_tok, sub=sub, layer=layer, is_last=is_last)
    scratch = [pltpu.VMEM((brows, D_MODEL), BF16), pltpu.VMEM((brows, DK), F32),
               pltpu.VMEM((brows, DK), F32), pltpu.VMEM((brows, DV), F32),
               pltpu.VMEM((brows, DK), F32), pltpu.VMEM((brows, POOL_W), F32),
               pltpu.VMEM((brows, DV), F32), pltpu.VMEM((brows, POOL_W), F32)]
    return pl.pallas_call(
        kern,
        grid=(bn // blk, n_inner),
        in_specs=in_specs,
        out_specs=out_specs,
        out_shape=out_shape,
        scratch_shapes=scratch,
        compiler_params=pltpu.CompilerParams(
            dimension_semantics=("arbitrary", "arbitrary"), vmem_limit_bytes=VMEM_LIMIT_BYTES),
        name="step_layer",
    )(*args, *weights)


def _layer_weights(l, norm_g, w_in, w_alpha, b_alpha, gla_gain, w_a, pool_w, pool_scale, w_b,
                   b_merge, w_out, final_norm_g):
    wi = w_in[l].astype(BF16)
    seg = [wi[:, IN_OFFS[i]:IN_OFFS[i + 1]] for i in range(len(IN_SIZES))]
    pad = LANES - GATE_RANK
    return {
        "norm_g": norm_g[l][None, :],
        "w_q": seg[0], "w_k": seg[1], "w_v": seg[2], "w_ga": seg[3], "w_u": seg[4],
        "w_gb": seg[5],
        "w_al": jnp.pad(seg[6], ((0, 0), (0, pad))),
        "w_mg": seg[7],
        "w_alpha": jnp.pad(w_alpha[l].astype(BF16), ((0, pad), (0, 0))),
        "b_alpha": b_alpha[l][None, :],
        "gla_gain": gla_gain[l][None, :],
        "w_a": w_a[l].astype(BF16),
        "pool_w": pool_w[l].astype(BF16),
        "pool_scale": pool_scale[l][None, :],
        "w_b": w_b[l].astype(BF16),
        "b_merge": b_merge[l][None, :],
        "w_out": w_out[l].astype(BF16),
        "final_g": final_norm_g[None, :],
    }


def _window_row():
    return np.repeat(np.asarray(POOL_WINDOWS, np.float32), POOL_GW)[None, :]


def kernel(x_prompt, x_sample, state_gla, state_pool, meta_tokens, norm_g, w_in, w_alpha, b_alpha,
           gla_gain, w_a, pool_w, pool_scale, w_b, b_merge, w_out, final_norm_g):
    depth = w_in.shape[0]
    dec_b, dec_t, _ = x_sample.shape
    wrow = _window_row()
    cnt_meta = jnp.asarray(np.minimum(wrow, np.arange(1, N_META + 1, dtype=np.float32)[:, None]))
    inv_w = jnp.asarray(1.0 / wrow)

    h_meta = meta_tokens[None].astype(F32)
    h_p = x_prompt
    h_s = x_sample.reshape(dec_b * dec_t, D_MODEL)
    zero_s = jnp.zeros((1, HEADS, DKH, DVH), F32)
    zero_buf = jnp.zeros((1, POOL_PAD, POOL_W), F32)
    buf_s = jnp.pad(state_pool, ((0, 0), (0, 0), (POOL_PAD - POOL_BUF, 0), (0, 0)))

    sg_p, sp_p, sp_s = [], [], []
    sg_s = None
    for l in range(depth):
        last = l == depth - 1
        wl = _layer_weights(l, norm_g, w_in, w_alpha, b_alpha, gla_gain, w_a, pool_w, pool_scale,
                            w_b, b_merge, w_out, final_norm_g)
        h_meta, s_meta, buf_meta = _seq_layer(h_meta, zero_s, zero_buf, cnt_meta, wl,
                                              is_last=False, pow2_counts=False)
        h_p, s_p, b_p = _seq_layer(h_p, s_meta, buf_meta, inv_w, wl, is_last=last,
                                   pow2_counts=True)
        h_s, sg_s, b_s = _step_layer(h_s, state_gla, sg_s, buf_s[l], inv_w, wl, n_tok=dec_t,
                                     layer=l, is_last=last)
        sg_p.append(s_p)
        sp_p.append(b_p[:, POOL_PAD - POOL_BUF:, :])
        sp_s.append(b_s[:, POOL_PAD - POOL_BUF:, :])
    return (h_p, h_s.reshape(dec_b, dec_t, D_MODEL), jnp.stack(sg_p), jnp.stack(sp_p),
            sg_s, jnp.stack(sp_s))
```

```python
import functools

import numpy as np
import jax
import jax.numpy as jnp
from jax import lax
from jax.experimental import pallas as pl
from jax.experimental.pallas import tpu as pltpu

F32 = jnp.float32
BF16 = jnp.bfloat16

D_MODEL = 1024
N_META = 16
HEADS = 4
DKH = 128
DVH = 256
DK = HEADS * DKH
DV = HEADS * DVH
GATE_RANK = 16
GATE_TAU = 16.0
POOL_W = 512
POOL_WINDOWS = (2, 4, 8, 16)
POOL_GW = POOL_W // len(POOL_WINDOWS)
POOL_BUF = max(POOL_WINDOWS) - 1
POOL_PAD = POOL_BUF + 1
PAST_LEN = 16384
EPS = 1e-6
IN_SIZES = (DK, DK, DV, DV, POOL_W, POOL_W, GATE_RANK, 2 * D_MODEL)
IN_OFFS = tuple(int(o) for o in np.cumsum((0,) + IN_SIZES))

LANES = 128
SUBLANES = 8
VMEM_LIMIT_BYTES = 56 * 1024 * 1024

SEQ_BLOCK = 256
SEQ_CHUNK = 256
SAMPLE_BLOCK = 32
SAMPLE_SUB = 4


def _mm(a, b):
    return jnp.dot(a, b, preferred_element_type=F32)


def _mm_nt(a, b):
    return lax.dot_general(a, b, (((1,), (1,)), ((), ())), preferred_element_type=F32)


def _mm_tn(a, b):
    return lax.dot_general(a, b, (((0,), (0,)), ((), ())), preferred_element_type=F32)


def _sigmoid(x):
    return 1.0 / (1.0 + jnp.exp(-x))


def _silu(x):
    return x * _sigmoid(x)


def _log_sigmoid(x):
    return jnp.minimum(x, 0.0) - jnp.log1p(jnp.exp(-jnp.abs(x)))


def _rmsnorm(x, g):
    ms = jnp.mean(x * x, axis=-1, keepdims=True)
    return x * lax.rsqrt(ms + EPS) * g


def _row_to_col(row, eye):
    return jnp.sum(eye * row, axis=1, keepdims=True)


def _eye(n):
    r = lax.broadcasted_iota(jnp.int32, (n, n), 0)
    c = lax.broadcasted_iota(jnp.int32, (n, n), 1)
    return (r == c).astype(F32)


W_NAMES = ("norm_g", "w_q", "w_k", "w_v", "w_ga", "w_u", "w_gb", "w_al", "w_mg", "w_alpha",
           "b_alpha", "gla_gain", "w_a", "pool_w", "pool_scale", "w_b", "b_merge", "w_out",
           "final_g")


def _dense_in(x, w):
    xn = _rmsnorm(x, w["norm_g"][...]).astype(BF16)
    alow = _mm(xn, w["w_al"][...])
    q = _mm(xn, w["w_q"][...]) * (DKH ** -0.5)
    k = _mm(xn, w["w_k"][...])
    a = _mm(alow.astype(BF16), w["w_alpha"][...]) + w["b_alpha"][...]
    la = _log_sigmoid(a) * (1.0 / GATE_TAU)
    v = _mm(xn, w["w_v"][...])
    u = _mm(xn, w["w_u"][...])
    return xn, q, k, v, la, u


def _dense_out(x, xn, o, z, w, is_last, pre=None):
    if pre is None:
        pre = (_mm(xn, w["w_ga"][...]), _mm(xn, w["w_gb"][...]), _mm(xn, w["w_mg"][...]))
    ga, gb, mg = pre
    gain = w["gla_gain"][...]
    parts = []
    for h in range(HEADS):
        oh = o[:, h * DVH:(h + 1) * DVH]
        ms = jnp.mean(oh * oh, axis=-1, keepdims=True)
        parts.append(oh * lax.rsqrt(ms + EPS) * gain[:, h * DVH:(h + 1) * DVH])
    on = jnp.concatenate(parts, axis=-1) * _silu(ga)
    ya = _mm(on.astype(BF16), w["w_a"][...])

    yp = []
    for g in range(len(POOL_WINDOWS)):
        zg = z[:, g * POOL_GW:(g + 1) * POOL_GW].astype(BF16)
        yp.append(_mm(zg, w["pool_w"][g]))
    y_pool = jnp.concatenate(yp, axis=-1) * w["pool_scale"][...]
    yb = _mm((y_pool * _silu(gb)).astype(BF16), w["w_b"][...])

    gates = _sigmoid(mg + w["b_merge"][...])
    merged = gates[:, :D_MODEL] * ya + gates[:, D_MODEL:] * yb
    h_new = x + _mm(merged.astype(BF16), w["w_out"][...])
    y = _rmsnorm(h_new, w["final_g"][...]) if is_last else None
    return h_new, y


def _window_sums(e):
    p1 = e + pltpu.roll(e, 1, 0)
    t = p1[:, POOL_GW:]
    p2 = t + pltpu.roll(t, 2, 0)
    t = p2[:, POOL_GW:]
    p3 = t + pltpu.roll(t, 4, 0)
    t = p3[:, POOL_GW:]
    p4 = t + pltpu.roll(t, 8, 0)
    return jnp.concatenate([p1[:, :POOL_GW], p2[:, :POOL_GW], p3[:, :POOL_GW], p4], axis=-1)


def _split_hi_lo(x):
    hi = x.astype(BF16)
    lo = (x - hi.astype(F32)).astype(BF16)
    return hi, lo


def _seq_kernel(*refs, tb, chunk, is_last, pow2_counts):
    n_w = len(W_NAMES)
    h_ref, s0_ref, buf0_ref, cnt_ref = refs[:4]
    w = dict(zip(W_NAMES, refs[4:4 + n_w]))
    hy_ref, s_ref, bufn_ref, o_s, ext_s = refs[4 + n_w:]

    t = pl.program_id(1)

    @pl.when(t == 0)
    def _():
        s_ref[0] = s0_ref[0]
        ext_s[0:POOL_PAD, :] = buf0_ref[0]

    x = h_ref[0]
    xn, q, k, v, la, u = _dense_in(x, w)
    vb = v.astype(BF16)

    n_chunks = tb // chunk
    ri = lax.broadcasted_iota(jnp.int32, (tb, tb), 0)
    ci = lax.broadcasted_iota(jnp.int32, (tb, tb), 1)
    tril = ((ri >= ci) & ((ri // chunk) == (ci // chunk))).astype(BF16)
    causal = (lax.broadcasted_iota(jnp.int32, (chunk, chunk), 0)
              >= lax.broadcasted_iota(jnp.int32, (chunk, chunk), 1))
    eye = _eye(DKH)
    la_hi, la_lo = _split_hi_lo(la)
    b = _mm(tril, la_hi) + _mm(tril, la_lo)
    ga = _mm(xn, w["w_ga"][...])
    decs =[jnp.exp(b[(c + 1) * chunk - 1:(c + 1) * chunk, :]) for c in range(n_chunks)]
    to_end = jnp.concatenate(
        [b[(c + 1) * chunk - 1:(c + 1) * chunk, :] - b[c * chunk:(c + 1) * chunk, :]
         for c in range(n_chunks)], axis=0)
    qd = (q * jnp.exp(b)).astype(BF16)
    kinv = (k * jnp.exp(-b)).astype(BF16)
    kd = (k * jnp.exp(to_end)).astype(BF16)
    chunks = [(c, h) for c in range(n_chunks) for h in range(HEADS)]

    def sl(c, h):
        return (slice(c * chunk, (c + 1) * chunk), slice(h * DKH, (h + 1) * DKH),
                slice(h * DVH, (h + 1) * DVH))

    a_m = {}
    for c, h in chunks:
        rows, ks, _ = sl(c, h)
        a = _mm_nt(qd[rows, ks], kinv[rows, ks])
        a_m[c, h] = jnp.where(causal, a, 0.0).astype(BF16)
    gb = _mm(xn, w["w_gb"][...])
    o_in, d_s, dcol = {}, {}, {}
    for c, h in chunks:
        rows, ks, vs = sl(c, h)
        o_in[c, h] = _mm(a_m[c, h], vb[rows, vs])
        d_s[c, h] = _mm_tn(kd[rows, ks], vb[rows, vs])
        dcol[c, h] = _row_to_col(decs[c][:, ks], eye)
    mg = _mm(xn, w["w_mg"][...])

    s_cur = [s_ref[0, h] for h in range(HEADS)]
    for c in range(n_chunks):
        for h in range(HEADS):
            rows, ks, vs = sl(c, h)
            o_s[rows, vs] = o_in[c, h] + _mm(qd[rows, ks], s_cur[h].astype(BF16))
            s_cur[h] = dcol[c, h] * s_cur[h] + d_s[c, h]
    for h in range(HEADS):
        s_ref[0, h] = s_cur[h]

    ext_s[POOL_PAD:POOL_PAD + tb, :] = u
    sums = _window_sums(ext_s[...])[POOL_PAD:, :]
    if pow2_counts:
        z = sums * cnt_ref[...] - u
    else:
        z = sums / cnt_ref[...] - u
    bufn_ref[0] = ext_s[tb:tb + POOL_PAD, :]
    ext_s[0:POOL_PAD, :] = ext_s[tb:tb + POOL_PAD, :]

    h_new, y = _dense_out(x, xn, o_s[...], z, w, is_last, pre=(ga, gb, mg))
    hy_ref[0] = y if is_last else h_new


def _const_spec(shape):
    nd = len(shape)
    return pl.BlockSpec(shape, lambda *_: (0,) * nd, pipeline_mode=pl.Buffered(1))


def _seq_layer(h, s0, buf0, cnt, wl, *, is_last, pow2_counts):
    bn, t_len, _ = h.shape
    tb = min(SEQ_BLOCK, t_len)
    chunk = min(SEQ_CHUNK, tb)
    assert t_len % tb == 0 and tb % chunk == 0
    shared = s0.shape[0] == 1
    bsel = (lambda b, t: (0, 0, 0, 0)) if shared else (lambda b, t: (b, 0, 0, 0))
    bsel3 = (lambda b, t: (0, 0, 0)) if shared else (lambda b, t: (b, 0, 0))
    weights = [wl[n] for n in W_NAMES]
    in_specs = [
        pl.BlockSpec((1, tb, D_MODEL), lambda b, t: (b, t, 0)),
        pl.BlockSpec((1, HEADS, DKH, DVH), bsel),
        pl.BlockSpec((1, POOL_PAD, POOL_W), bsel3),
        _const_spec(cnt.shape),
    ] + [_const_spec(a.shape) for a in weights]
    out_shape = [jax.ShapeDtypeStruct(h.shape, F32),
                 jax.ShapeDtypeStruct((bn, HEADS, DKH, DVH), F32),
                 jax.ShapeDtypeStruct((bn, POOL_PAD, POOL_W), F32)]
    out_specs = [pl.BlockSpec((1, tb, D_MODEL), lambda b, t: (b, t, 0)),
                 pl.BlockSpec((1, HEADS, DKH, DVH), lambda b, t: (b, 0, 0, 0)),
                 pl.BlockSpec((1, POOL_PAD, POOL_W), lambda b, t: (b, 0, 0))]
    kern = functools.partial(_seq_kernel, tb=tb, chunk=chunk, is_last=is_last,
                             pow2_counts=pow2_counts)
    return pl.pallas_call(
        kern,
        grid=(bn, t_len // tb),
        in_specs=in_specs,
        out_specs=out_specs,
        out_shape=out_shape,
        scratch_shapes=[pltpu.VMEM((tb, DV), F32), pltpu.VMEM((tb + POOL_PAD, POOL_W), F32)],
        compiler_params=pltpu.CompilerParams(
            dimension_semantics=("arbitrary", "arbitrary"), vmem_limit_bytes=VMEM_LIMIT_BYTES),
        name="seq_layer",
    )(h, s0, buf0, cnt, *weights)


def _step_kernel(*refs, n_tok, sub, layer, is_last):
    n_w = len(W_NAMES)
    n_in = 5 if layer else 4
    h_ref, s0_ref, buf0_ref, cnt_ref = refs[:4]
    sprev_ref = refs[4] if layer else None
    w = dict(zip(W_NAMES, refs[n_in:n_in + n_w]))
    hy_ref, sall_ref, bufn_ref = refs[n_in + n_w:n_in + n_w + 3]
    xn_s, q_s, k_s, v_s, la_s, u_s, o_s, z_s = refs[n_in + n_w + 3:]

    j = pl.program_id(1)
    nj = pl.num_programs(1)
    pair = SUBLANES // n_tok
    assert pair == 2 and sub % pair == 0
    if layer:
        sall_ref[0:layer] = sprev_ref[...]
    s_ref = sall_ref.at[layer]

    @pl.when(j == 0)
    def _():
        xn, q, k, v, la, u = _dense_in(h_ref[...], w)
        xn_s[...] = xn
        q_s[...] = q
        k_s[...] = k
        v_s[...] = v
        la_s[...] = la
        u_s[...] = u

    row = lax.broadcasted_iota(jnp.int32, (SUBLANES, 1), 0)
    tok = row % n_tok
    first = row < n_tok
    ri = lax.broadcasted_iota(jnp.int32, (SUBLANES, SUBLANES), 0)
    ci = lax.broadcasted_iota(jnp.int32, (SUBLANES, SUBLANES), 1)
    amask = (ri >= ci) & ((ri // n_tok) == (ci // n_tok))
    eye = _eye(DKH)
    inv_cnt = cnt_ref[...]

    for p in range(sub // pair):
        r0 = pl.multiple_of((j * (sub // pair) + p) * SUBLANES, SUBLANES)
        rows = pl.ds(r0, SUBLANES)
        la = la_s[rows, :]
        b = la + jnp.where(tok >= 1, pltpu.roll(la, 1, 0), 0.0)
        b = b + jnp.where(tok >= 2, pltpu.roll(b, 2, 0), 0.0)
        b_end0 = b[n_tok - 1:n_tok, :]
        b_end1 = b[2 * n_tok - 1:2 * n_tok, :]
        b_end = jnp.where(first, b_end0, b_end1)
        q = q_s[rows, :]
        k = k_s[rows, :]
        v = v_s[rows, :]
        qd = q * jnp.exp(b)
        kinv = k * jnp.exp(-b)
        kd = k * jnp.exp(b_end - b)
        kd0 = jnp.where(first, kd, 0.0)
        kd1 = jnp.where(first, 0.0, kd)
        dec0 = jnp.exp(b_end0)
        dec1 = jnp.exp(b_end1)
        o_parts = []
        for h in range(HEADS):
            ks = slice(h * DKH, (h + 1) * DKH)
            vs = slice(h * DVH, (h + 1) * DVH)
            qh = qd[:, ks]
            vh = v[:, vs]
            a = jnp.where(amask, _mm_nt(qh, kinv[:, ks]), 0.0)
            s_a = s0_ref[pair * p, h]
            s_b = s0_ref[pair * p + 1, h]
            o = _mm(a, vh) + jnp.where(first, _mm(qh, s_a), _mm(qh, s_b))
            o_parts.append(o)
            s_ref[pair * p, h] = _row_to_col(dec0[:, ks], eye) * s_a + _mm_tn(kd0[:, ks], vh)
            s_ref[pair * p + 1, h] = _row_to_col(dec1[:, ks], eye) * s_b + _mm_tn(kd1[:, ks], vh)
        o_s[rows, :] = jnp.concatenate(o_parts, axis=-1)

        u = u_s[rows, :]
        e0 = jnp.concatenate([buf0_ref[pair * p], u], axis=0)
        e1 = jnp.concatenate([buf0_ref[pair * p + 1], pltpu.roll(u, n_tok, 0)], axis=0)
        n_ext = POOL_PAD + SUBLANES
        w0 = _window_sums(e0)[POOL_PAD:, :]
        w1 = _window_sums(e1)[POOL_PAD:, :]
        sums = jnp.where(first, w0, pltpu.roll(w1, n_tok, 0))
        z_s[rows, :] = sums * inv_cnt - u
        bufn_ref[pair * p] = pltpu.roll(e0, n_ext - n_tok, 0)[0:POOL_PAD, :]
        bufn_ref[pair * p + 1] = pltpu.roll(e1, n_ext - n_tok, 0)[0:POOL_PAD, :]

    @pl.when(j == nj - 1)
    def _():
        h_new, y = _dense_out(h_ref[...], xn_s[...], o_s[...], z_s[...], w, is_last)
        hy_ref[...] = y if is_last else h_new


def _step_layer(h, s_all, s_prev, buf0, cnt, wl, *, n_tok, layer, is_last):
    rows, _ = h.shape
    bn = rows // n_tok
    sub, blk = SAMPLE_SUB, SAMPLE_BLOCK
    assert bn % blk == 0 and blk % sub == 0
    n_inner = blk // sub
    brows = blk * n_tok
    weights = [wl[n] for n in W_NAMES]
    in_specs = [
        pl.BlockSpec((brows, D_MODEL), lambda i, j: (i, 0)),
        pl.BlockSpec((None, sub, HEADS, DKH, DVH), lambda i, j: (layer, i * n_inner + j, 0, 0, 0)),
        pl.BlockSpec((sub, POOL_PAD, POOL_W), lambda i, j: (i * n_inner + j, 0, 0)),
        _const_spec(cnt.shape),
    ]
    args = [h, s_all, buf0, cnt]
    if layer:
        in_specs.append(pl.BlockSpec((layer, sub, HEADS, DKH, DVH),
                                     lambda i, j: (0, i * n_inner + j, 0, 0, 0)))
        args.append(s_prev)
    in_specs += [_const_spec(a.shape) for a in weights]
    out_shape = [jax.ShapeDtypeStruct(h.shape, F32),
                 jax.ShapeDtypeStruct((layer + 1, bn, HEADS, DKH, DVH), F32),
                 jax.ShapeDtypeStruct((bn, POOL_PAD, POOL_W), F32)]
    out_specs = [pl.BlockSpec((brows, D_MODEL), lambda i, j: (i, 0)),
                 pl.BlockSpec((layer + 1, sub, HEADS, DKH, DVH),
                              lambda i, j: (0, i * n_inner + j, 0, 0, 0)),
                 pl.BlockSpec((sub, POOL_PAD, POOL_W), lambda i, j: (i * n_inner + j, 0, 0))]
    kern = functools.partial(_step_kernel, n_tok=n_tok, sub=sub, layer=layer, is_last=is_last)
    scratch = [pltpu.VMEM((brows, D_MODEL), BF16), pltpu.VMEM((brows, DK), F32),
               pltpu.VMEM((brows, DK), F32), pltpu.VMEM((brows, DV), F32),
               pltpu.VMEM((brows, DK), F32), pltpu.VMEM((brows, POOL_W), F32),
               pltpu.VMEM((brows, DV), F32), pltpu.VMEM((brows, POOL_W), F32)]
    return pl.pallas_call(
        kern,
        grid=(bn // blk, n_inner),
        in_specs=in_specs,
        out_specs=out_specs,
        out_shape=out_shape,
        scratch_shapes=scratch,
        compiler_params=pltpu.CompilerParams(
            dimension_semantics=("arbitrary", "arbitrary"), vmem_limit_bytes=VMEM_LIMIT_BYTES),
        name="step_layer",
    )(*args, *weights)


def _layer_weights(l, norm_g, w_in, w_alpha, b_alpha, gla_gain, w_a, pool_w, pool_scale, w_b,
                   b_merge, w_out, final_norm_g):
    seg = [w_in[l, :, IN_OFFS[i]:IN_OFFS[i + 1]].astype(BF16) for i in range(len(IN_SIZES))]
    pad = LANES - GATE_RANK
    return {
        "norm_g": norm_g[l][None, :],
        "w_q": seg[0], "w_k": seg[1], "w_v": seg[2], "w_ga": seg[3], "w_u": seg[4],
        "w_gb": seg[5],
        "w_al": jnp.pad(seg[6], ((0, 0), (0, pad))),
        "w_mg": seg[7],
        "w_alpha": jnp.pad(w_alpha[l].astype(BF16), ((0, pad), (0, 0))),
        "b_alpha": b_alpha[l][None, :],
        "gla_gain": gla_gain[l][None, :],
        "w_a": w_a[l].astype(BF16),
        "pool_w": pool_w[l].astype(BF16),
        "pool_scale": pool_scale[l][None, :],
        "w_b": w_b[l].astype(BF16),
        "b_merge": b_merge[l][None, :],
        "w_out": w_out[l].astype(BF16),
        "final_g": final_norm_g[None, :],
    }


def _window_row():
    return np.repeat(np.asarray(POOL_WINDOWS, np.float32), POOL_GW)[None, :]


def kernel(x_prompt, x_sample, state_gla, state_pool, meta_tokens, norm_g, w_in, w_alpha, b_alpha,
           gla_gain, w_a, pool_w, pool_scale, w_b, b_merge, w_out, final_norm_g):
    depth = w_in.shape[0]
    dec_b, dec_t, _ = x_sample.shape
    wrow = _window_row()
    cnt_meta = jnp.asarray(np.minimum(wrow, np.arange(1, N_META + 1, dtype=np.float32)[:, None]))
    inv_w = jnp.asarray(1.0 / wrow)

    h_meta = meta_tokens[None].astype(F32)
    h_p = x_prompt
    h_s = x_sample.reshape(dec_b * dec_t, D_MODEL)
    zero_s = jnp.zeros((1, HEADS, DKH, DVH), F32)
    zero_buf = jnp.zeros((1, POOL_PAD, POOL_W), F32)
    buf_s = jnp.pad(state_pool, ((0, 0), (0, 0), (POOL_PAD - POOL_BUF, 0), (0, 0)))

    sg_p, sp_p, sp_s = [], [], []
    sg_s = None
    for l in range(depth):
        last = l == depth - 1
        wl = _layer_weights(l, norm_g, w_in, w_alpha, b_alpha, gla_gain, w_a, pool_w, pool_scale,
                            w_b, b_merge, w_out, final_norm_g)
        h_meta, s_meta, buf_meta = _seq_layer(h_meta, zero_s, zero_buf, cnt_meta, wl,
                                              is_last=False, pow2_counts=False)
        h_p, s_p, b_p = _seq_layer(h_p, s_meta, buf_meta, inv_w, wl, is_last=last,
                                   pow2_counts=True)
        h_s, sg_s, b_s = _step_layer(h_s, state_gla, sg_s, buf_s[l], inv_w, wl, n_tok=dec_t,
                                     layer=l, is_last=last)
        sg_p.append(s_p)
        sp_p.append(b_p[:, POOL_PAD - POOL_BUF:, :])
        sp_s.append(b_s[:, POOL_PAD - POOL_BUF:, :])
    return (h_p, h_s.reshape(dec_b, dec_t, D_MODEL), jnp.stack(sg_p), jnp.stack(sp_p),
            sg_s, jnp.stack(sp_s))
```

```python
import functools

import numpy as np
import jax
import jax.numpy as jnp
from jax import lax
from jax.experimental import pallas as pl
from jax.experimental.pallas import tpu as pltpu

F32 = jnp.float32
BF16 = jnp.bfloat16

D_MODEL = 1024
N_META = 16
HEADS = 4
DKH = 128
DVH = 256
DK = HEADS * DKH
DV = HEADS * DVH
GATE_RANK = 16
GATE_TAU = 16.0
POOL_W = 512
POOL_WINDOWS = (2, 4, 8, 16)
POOL_GW = POOL_W // len(POOL_WINDOWS)
POOL_BUF = max(POOL_WINDOWS) - 1
POOL_PAD = POOL_BUF + 1
PAST_LEN = 16384
EPS = 1e-6
IN_SIZES = (DK, DK, DV, DV, POOL_W, POOL_W, GATE_RANK, 2 * D_MODEL)
IN_OFFS = tuple(int(o) for o in np.cumsum((0,) + IN_SIZES))

LANES = 128
SUBLANES = 8
VMEM_LIMIT_BYTES = 56 * 1024 * 1024

SEQ_BLOCK = 512
SEQ_CHUNK = 256
SAMPLE_BLOCK = 32
SAMPLE_SUB = 4


def _mm(a, b):
    return jnp.dot(a, b, preferred_element_type=F32)


def _mm_nt(a, b):
    return lax.dot_general(a, b, (((1,), (1,)), ((), ())), preferred_element_type=F32)


def _mm_tn(a, b):
    return lax.dot_general(a, b, (((0,), (0,)), ((), ())), preferred_element_type=F32)


def _sigmoid(x):
    return 1.0 / (1.0 + jnp.exp(-x))


def _silu(x):
    return x * _sigmoid(x)


def _log_sigmoid(x):
    return jnp.minimum(x, 0.0) - jnp.log1p(jnp.exp(-jnp.abs(x)))


def _rmsnorm(x, g):
    ms = jnp.mean(x * x, axis=-1, keepdims=True)
    return x * lax.rsqrt(ms + EPS) * g


def _row_to_col(row, eye):
    return jnp.sum(eye * row, axis=1, keepdims=True)


def _eye(n):
    r = lax.broadcasted_iota(jnp.int32, (n, n), 0)
    c = lax.broadcasted_iota(jnp.int32, (n, n), 1)
    return (r == c).astype(F32)


W_NAMES = ("norm_g", "w_q", "w_k", "w_v", "w_ga", "w_u", "w_gb", "w_al", "w_mg", "w_alpha",
           "b_alpha", "gla_gain", "w_a", "pool_w", "pool_scale", "w_b", "b_merge", "w_out",
           "final_g")


def _dense_in(x, w):
    xn = _rmsnorm(x, w["norm_g"][...]).astype(BF16)
    alow = _mm(xn, w["w_al"][...])
    q = _mm(xn, w["w_q"][...]) * (DKH ** -0.5)
    k = _mm(xn, w["w_k"][...])
    a = _mm(alow.astype(BF16), w["w_alpha"][...]) + w["b_alpha"][...]
    la = _log_sigmoid(a) * (1.0 / GATE_TAU)
    v = _mm(xn, w["w_v"][...])
    u = _mm(xn, w["w_u"][...])
    return xn, q, k, v, la, u


def _dense_out(x, xn, o, z, w, is_last, pre=None):
    if pre is None:
        pre = (_mm(xn, w["w_ga"][...]), _mm(xn, w["w_gb"][...]), _mm(xn, w["w_mg"][...]))
    ga, gb, mg = pre
    gain = w["gla_gain"][...]
    parts = []
    for h in range(HEADS):
        oh = o[:, h * DVH:(h + 1) * DVH]
        ms = jnp.mean(oh * oh, axis=-1, keepdims=True)
        parts.append(oh * lax.rsqrt(ms + EPS) * gain[:, h * DVH:(h + 1) * DVH])
    on = jnp.concatenate(parts, axis=-1) * _silu(ga)
    ya = _mm(on.astype(BF16), w["w_a"][...])

    yp = []
    for g in range(len(POOL_WINDOWS)):
        zg = z[:, g * POOL_GW:(g + 1) * POOL_GW].astype(BF16)
        yp.append(_mm(zg, w["pool_w"][g]))
    y_pool = jnp.concatenate(yp, axis=-1) * w["pool_scale"][...]
    yb = _mm((y_pool * _silu(gb)).astype(BF16), w["w_b"][...])

    gates = _sigmoid(mg + w["b_merge"][...])
    merged = gates[:, :D_MODEL] * ya + gates[:, D_MODEL:] * yb
    h_new = x + _mm(merged.astype(BF16), w["w_out"][...])
    y = _rmsnorm(h_new, w["final_g"][...]) if is_last else None
    return h_new, y


def _window_sums(e):
    p1 = e + pltpu.roll(e, 1, 0)
    t = p1[:, POOL_GW:]
    p2 = t + pltpu.roll(t, 2, 0)
    t = p2[:, POOL_GW:]
    p3 = t + pltpu.roll(t, 4, 0)
    t = p3[:, POOL_GW:]
    p4 = t + pltpu.roll(t, 8, 0)
    return jnp.concatenate([p1[:, :POOL_GW], p2[:, :POOL_GW], p3[:, :POOL_GW], p4], axis=-1)


def _split_hi_lo(x):
    hi = x.astype(BF16)
    lo = (x - hi.astype(F32)).astype(BF16)
    return hi, lo


def _seq_kernel(*refs, tb, chunk, is_last, pow2_counts):
    n_w = len(W_NAMES)
    h_ref, s0_ref, buf0_ref, cnt_ref = refs[:4]
    w = dict(zip(W_NAMES, refs[4:4 + n_w]))
    hy_ref, s_ref, bufn_ref, o_s, ext_s = refs[4 + n_w:]

    t = pl.program_id(1)

    @pl.when(t == 0)
    def _():
        s_ref[0] = s0_ref[0]
        ext_s[0:POOL_PAD, :] = buf0_ref[0]

    x = h_ref[0]
    xn, q, k, v, la, u = _dense_in(x, w)
    vb = v.astype(BF16)

    n_chunks = tb // chunk
    causal = (lax.broadcasted_iota(jnp.int32, (chunk, chunk), 0)
              >= lax.broadcasted_iota(jnp.int32, (chunk, chunk), 1))
    tril = causal.astype(BF16)
    eye = _eye(DKH)
    la_hi, la_lo = _split_hi_lo(la)
    b = jnp.concatenate(
        [_mm(tril, la_hi[c * chunk:(c + 1) * chunk]) + _mm(tril, la_lo[c * chunk:(c + 1) * chunk])
         for c in range(n_chunks)], axis=0)
    ga = _mm(xn, w["w_ga"][...])
    decs =[jnp.exp(b[(c + 1) * chunk - 1:(c + 1) * chunk, :]) for c in range(n_chunks)]
    to_end = jnp.concatenate(
        [b[(c + 1) * chunk - 1:(c + 1) * chunk, :] - b[c * chunk:(c + 1) * chunk, :]
         for c in range(n_chunks)], axis=0)
    qd = (q * jnp.exp(b)).astype(BF16)
    kinv = (k * jnp.exp(-b)).astype(BF16)
    kd = (k * jnp.exp(to_end)).astype(BF16)
    chunks = [(c, h) for c in range(n_chunks) for h in range(HEADS)]

    def sl(c, h):
        return (slice(c * chunk, (c + 1) * chunk), slice(h * DKH, (h + 1) * DKH),
                slice(h * DVH, (h + 1) * DVH))

    a_m = {}
    for c, h in chunks:
        rows, ks, _ = sl(c, h)
        a = _mm_nt(qd[rows, ks], kinv[rows, ks])
        a_m[c, h] = jnp.where(causal, a, 0.0).astype(BF16)
    gb = _mm(xn, w["w_gb"][...])
    o_in, d_s, dcol = {}, {}, {}
    for c, h in chunks:
        rows, ks, vs = sl(c, h)
        o_in[c, h] = _mm(a_m[c, h], vb[rows, vs])
        d_s[c, h] = _mm_tn(kd[rows, ks], vb[rows, vs])
        dcol[c, h] = _row_to_col(decs[c][:, ks], eye)
    mg = _mm(xn, w["w_mg"][...])

    s_cur = [s_ref[0, h] for h in range(HEADS)]
    for c in range(n_chunks):
        for h in range(HEADS):
            rows, ks, vs = sl(c, h)
            o_s[rows, vs] = o_in[c, h] + _mm(qd[rows, ks], s_cur[h].astype(BF16))
            s_cur[h] = dcol[c, h] * s_cur[h] + d_s[c, h]
    for h in range(HEADS):
        s_ref[0, h] = s_cur[h]

    ext_s[POOL_PAD:POOL_PAD + tb, :] = u
    sums = _window_sums(ext_s[...])[POOL_PAD:, :]
    if pow2_counts:
        z = sums * cnt_ref[...] - u
    else:
        z = sums / cnt_ref[...] - u
    bufn_ref[0] = ext_s[tb:tb + POOL_PAD, :]
    ext_s[0:POOL_PAD, :] = ext_s[tb:tb + POOL_PAD, :]

    h_new, y = _dense_out(x, xn, o_s[...], z, w, is_last, pre=(ga, gb, mg))
    hy_ref[0] = y if is_last else h_new


def _const_spec(shape):
    nd = len(shape)
    return pl.BlockSpec(shape, lambda *_: (0,) * nd, pipeline_mode=pl.Buffered(1))


def _seq_layer(h, s0, buf0, cnt, wl, *, is_last, pow2_counts):
    bn, t_len, _ = h.shape
    tb = min(SEQ_BLOCK, t_len)
    chunk = min(SEQ_CHUNK, tb)
    assert t_len % tb == 0 and tb % chunk == 0
    shared = s0.shape[0] == 1
    bsel = (lambda b, t: (0, 0, 0, 0)) if shared else (lambda b, t: (b, 0, 0, 0))
    bsel3 = (lambda b, t: (0, 0, 0)) if shared else (lambda b, t: (b, 0, 0))
    weights = [wl[n] for n in W_NAMES]
    in_specs = [
        pl.BlockSpec((1, tb, D_MODEL), lambda b, t: (b, t, 0)),
        pl.BlockSpec((1, HEADS, DKH, DVH), bsel),
        pl.BlockSpec((1, POOL_PAD, POOL_W), bsel3),
        _const_spec(cnt.shape),
    ] + [_const_spec(a.shape) for a in weights]
    out_shape = [jax.ShapeDtypeStruct(h.shape, F32),
                 jax.ShapeDtypeStruct((bn, HEADS, DKH, DVH), F32),
                 jax.ShapeDtypeStruct((bn, POOL_PAD, POOL_W), F32)]
    out_specs = [pl.BlockSpec((1, tb, D_MODEL), lambda b, t: (b, t, 0)),
                 pl.BlockSpec((1, HEADS, DKH, DVH), lambda b, t: (b, 0, 0, 0)),
                 pl.BlockSpec((1, POOL_PAD, POOL_W), lambda b, t: (b, 0, 0))]
    kern = functools.partial(_seq_kernel, tb=tb, chunk=chunk, is_last=is_last,
                             pow2_counts=pow2_counts)
    return pl.pallas_call(
        kern,
        grid=(bn, t_len // tb),
        in_specs=in_specs,
        out_specs=out_specs,
        out_shape=out_shape,
        scratch_shapes=[pltpu.VMEM((tb, DV), F32), pltpu.VMEM((tb + POOL_PAD, POOL_W), F32)],
        compiler_params=pltpu.CompilerParams(
            dimension_semantics=("arbitrary", "arbitrary"), vmem_limit_bytes=VMEM_LIMIT_BYTES),
        name="seq_layer",
    )(h, s0, buf0, cnt, *weights)


def _step_kernel(*refs, n_tok, sub, layer, is_last):
    n_w = len(W_NAMES)
    n_in = 5 if layer else 4
    h_ref, s0_ref, buf0_ref, cnt_ref = refs[:4]
    sprev_ref = refs[4] if layer else None
    w = dict(zip(W_NAMES, refs[n_in:n_in + n_w]))
    hy_ref, sall_ref, bufn_ref = refs[n_in + n_w:n_in + n_w + 3]
    xn_s, q_s, k_s, v_s, la_s, u_s, o_s, z_s = refs[n_in + n_w + 3:]

    j = pl.program_id(1)
    nj = pl.num_programs(1)
    pair = SUBLANES // n_tok
    assert pair == 2 and sub % pair == 0
    if layer:
        sall_ref[0:layer] = sprev_ref[...]
    s_ref = sall_ref.at[layer]

    @pl.when(j == 0)
    def _():
        xn, q, k, v, la, u = _dense_in(h_ref[...], w)
        xn_s[...] = xn
        q_s[...] = q
        k_s[...] = k
        v_s[...] = v
        la_s[...] = la
        u_s[...] = u

    nr = sub * n_tok
    rows = pl.ds(pl.multiple_of(j * nr, nr), nr)
    row = lax.broadcasted_iota(jnp.int32, (nr, 1), 0)
    tok = row % n_tok
    seq = row // n_tok
    ri = lax.broadcasted_iota(jnp.int32, (nr, nr), 0)
    ci = lax.broadcasted_iota(jnp.int32, (nr, nr), 1)
    amask = (ri >= ci) & ((ri // n_tok) == (ci // n_tok))
    eye = _eye(DKH)
    b = la_s[rows, :]
    step = 1
    while step < n_tok:
        b = b + jnp.where(tok >= step, pltpu.roll(b, step, 0), 0.0)
        step *= 2
    b_end = b
    for d in range(1, n_tok):
        b_end = jnp.where(tok == n_tok - 1 - d, pltpu.roll(b, nr - d, 0), b_end)
    k = k_s[rows, :]
    qd = (q_s[rows, :] * jnp.exp(b)).astype(BF16)
    kinv = (k * jnp.exp(-b)).astype(BF16)
    kd = k * jnp.exp(b_end - b)
    dec = jnp.exp(b_end)
    vb = v_s[rows, :].astype(BF16)
    o_parts = []
    for h in range(HEADS):
        ks = slice(h * DKH, (h + 1) * DKH)
        vs = slice(h * DVH, (h + 1) * DVH)
        a = jnp.where(amask, _mm_nt(qd[:, ks], kinv[:, ks]), 0.0).astype(BF16)
        o_parts.append(_mm(a, vb[:, vs]))
    for h in range(HEADS):
        ks = slice(h * DKH, (h + 1) * DKH)
        vs = slice(h * DVH, (h + 1) * DVH)
        o = o_parts[h]
        for s in range(sub):
            mine = seq == s
            s_old = s0_ref[s, h]
            o = o + jnp.where(mine, _mm(qd[:, ks], s_old.astype(BF16)), 0.0)
            kd_s = jnp.where(mine, kd[:, ks], 0.0).astype(BF16)
            dcol = _row_to_col(dec[s * n_tok:s * n_tok + 1, ks], eye)
            s_ref[s, h] = dcol * s_old + _mm_tn(kd_s, vb[:, vs])
        o_parts[h] = o
    o_s[rows, :] = jnp.concatenate(o_parts, axis=-1)

    first = lax.broadcasted_iota(jnp.int32, (SUBLANES, 1), 0) < n_tok
    inv_cnt = cnt_ref[...]
    for p in range(sub // pair):
        r0 = pl.multiple_of(j * nr + p * SUBLANES, SUBLANES)
        rows = pl.ds(r0, SUBLANES)
        u = u_s[rows, :]
        e0 = jnp.concatenate([buf0_ref[pair * p], u], axis=0)
        e1 = jnp.concatenate([buf0_ref[pair * p + 1], pltpu.roll(u, n_tok, 0)], axis=0)
        n_ext = POOL_PAD + SUBLANES
        w0 = _window_sums(e0)[POOL_PAD:, :]
        w1 = _window_sums(e1)[POOL_PAD:, :]
        sums = jnp.where(first, w0, pltpu.roll(w1, n_tok, 0))
        z_s[rows, :] = sums * inv_cnt - u
        bufn_ref[pair * p] = pltpu.roll(e0, n_ext - n_tok, 0)[0:POOL_PAD, :]
        bufn_ref[pair * p + 1] = pltpu.roll(e1, n_ext - n_tok, 0)[0:POOL_PAD, :]

    @pl.when(j == nj - 1)
    def _():
        h_new, y = _dense_out(h_ref[...], xn_s[...], o_s[...], z_s[...], w, is_last)
        hy_ref[...] = y if is_last else h_new


def _step_layer(h, s_all, s_prev, buf0, cnt, wl, *, n_tok, layer, is_last):
    rows, _ = h.shape
    bn = rows // n_tok
    sub, blk = SAMPLE_SUB, SAMPLE_BLOCK
    assert bn % blk == 0 and blk % sub == 0
    n_inner = blk // sub
    brows = blk * n_tok
    weights = [wl[n] for n in W_NAMES]
    in_specs = [
        pl.BlockSpec((brows, D_MODEL), lambda i, j: (i, 0)),
        pl.BlockSpec((None, sub, HEADS, DKH, DVH), lambda i, j: (layer, i * n_inner + j, 0, 0, 0)),
        pl.BlockSpec((sub, POOL_PAD, POOL_W), lambda i, j: (i * n_inner + j, 0, 0)),
        _const_spec(cnt.shape),
    ]
    args = [h, s_all, buf0, cnt]
    if layer:
        in_specs.append(pl.BlockSpec((layer, sub, HEADS, DKH, DVH),
                                     lambda i, j: (0, i * n_inner + j, 0, 0, 0)))
        args.append(s_prev)
    in_specs += [_const_spec(a.shape) for a in weights]
    out_shape = [jax.ShapeDtypeStruct(h.shape, F32),
                 jax.ShapeDtypeStruct((layer + 1, bn, HEADS, DKH, DVH), F32),
                 jax.ShapeDtypeStruct((bn, POOL_PAD, POOL_W), F32)]
    out_specs = [pl.BlockSpec((brows, D_MODEL), lambda i, j: (i, 0)),
                 pl.BlockSpec((layer + 1, sub, HEADS, DKH, DVH),
                              lambda i, j: (0, i * n_inner + j, 0, 0, 0)),
                 pl.BlockSpec((sub, POOL_PAD, POOL_W), lambda i, j: (i * n_inner + j, 0, 0))]
    kern = functools.partial(_step_kernel, n_tok=n_tok, sub=sub, layer=layer, is_last=is_last)
    scratch = [pltpu.VMEM((brows, D_MODEL), BF16), pltpu.VMEM((brows, DK), F32),
               pltpu.VMEM((brows, DK), F32), pltpu.VMEM((brows, DV), F32),
               pltpu.VMEM((brows, DK), F32), pltpu.VMEM((brows, POOL_W), F32),
               pltpu.VMEM((brows, DV), F32), pltpu.VMEM((brows, POOL_W), F32)]
    return pl.pallas_call(
        kern,
        grid=(bn // blk, n_inner),
        in_specs=in_specs,
        out_specs=out_specs,
        out_shape=out_shape,
        scratch_shapes=scratch,
        compiler_params=pltpu.CompilerParams(
            dimension_semantics=("arbitrary", "arbitrary"), vmem_limit_bytes=VMEM_LIMIT_BYTES),
        name="step_layer",
    )(*args, *weights)


def _layer_weights(l, norm_g, w_in, w_alpha, b_alpha, gla_gain, w_a, pool_w, pool_scale, w_b,
                   b_merge, w_out, final_norm_g):
    seg = [w_in[l, :, IN_OFFS[i]:IN_OFFS[i + 1]].astype(BF16) for i in range(len(IN_SIZES))]
    pad = LANES - GATE_RANK
    return {
        "norm_g": norm_g[l][None, :],
        "w_q": seg[0], "w_k": seg[1], "w_v": seg[2], "w_ga": seg[3], "w_u": seg[4],
        "w_gb": seg[5],
        "w_al": jnp.pad(seg[6], ((0, 0), (0, pad))),
        "w_mg": seg[7],
        "w_alpha": jnp.pad(w_alpha[l].astype(BF16), ((0, pad), (0, 0))),
        "b_alpha": b_alpha[l][None, :],
        "gla_gain": gla_gain[l][None, :],
        "w_a": w_a[l].astype(BF16),
        "pool_w": pool_w[l].astype(BF16),
        "pool_scale": pool_scale[l][None, :],
        "w_b": w_b[l].astype(BF16),
        "b_merge": b_merge[l][None, :],
        "w_out": w_out[l].astype(BF16),
        "final_g": final_norm_g[None, :],
    }


def _window_row():
    return np.repeat(np.asarray(POOL_WINDOWS, np.float32), POOL_GW)[None, :]


def kernel(x_prompt, x_sample, state_gla, state_pool, meta_tokens, norm_g, w_in, w_alpha, b_alpha,
           gla_gain, w_a, pool_w, pool_scale, w_b, b_merge, w_out, final_norm_g):
    depth = w_in.shape[0]
    dec_b, dec_t, _ = x_sample.shape
    wrow = _window_row()
    cnt_meta = jnp.asarray(np.minimum(wrow, np.arange(1, N_META + 1, dtype=np.float32)[:, None]))
    inv_w = jnp.asarray(1.0 / wrow)

    h_meta = meta_tokens[None].astype(F32)
    h_p = x_prompt
    h_s = x_sample.reshape(dec_b * dec_t, D_MODEL)
    zero_s = jnp.zeros((1, HEADS, DKH, DVH), F32)
    zero_buf = jnp.zeros((1, POOL_PAD, POOL_W), F32)
    buf_s = jnp.pad(state_pool, ((0, 0), (0, 0), (POOL_PAD - POOL_BUF, 0), (0, 0)))

    sg_p, sp_p, sp_s = [], [], []
    sg_s = None
    for l in range(depth):
        last = l == depth - 1
        wl = _layer_weights(l, norm_g, w_in, w_alpha, b_alpha, gla_gain, w_a, pool_w, pool_scale,
                            w_b, b_merge, w_out, final_norm_g)
        h_meta, s_meta, buf_meta = _seq_layer(h_meta, zero_s, zero_buf, cnt_meta, wl,
                                              is_last=False, pow2_counts=False)
        h_p, s_p, b_p = _seq_layer(h_p, s_meta, buf_meta, inv_w, wl, is_last=last,
                                   pow2_counts=True)
        h_s, sg_s, b_s = _step_layer(h_s, state_gla, sg_s, buf_s[l], inv_w, wl, n_tok=dec_t,
                                     layer=l, is_last=last)
        sg_p.append(s_p)
        sp_p.append(b_p[:, POOL_PAD - POOL_BUF:, :])
        sp_s.append(b_s[:, POOL_PAD - POOL_BUF:, :])
    return (h_p, h_s.reshape(dec_b, dec_t, D_MODEL), jnp.stack(sg_p), jnp.stack(sp_p),
            sg_s, jnp.stack(sp_s))
```

```python
import functools

import numpy as np
import jax
import jax.numpy as jnp
from jax import lax
from jax.experimental import pallas as pl
from jax.experimental.pallas import tpu as pltpu

F32 = jnp.float32
BF16 = jnp.bfloat16

D_MODEL = 1024
N_META = 16
HEADS = 4
DKH = 128
DVH = 256
DK = HEADS * DKH
DV = HEADS * DVH
GATE_RANK = 16
GATE_TAU = 16.0
POOL_W = 512
POOL_WINDOWS = (2, 4, 8, 16)
POOL_GW = POOL_W // len(POOL_WINDOWS)
POOL_BUF = max(POOL_WINDOWS) - 1
POOL_PAD = POOL_BUF + 1
PAST_LEN = 16384
EPS = 1e-6
IN_SIZES = (DK, DK, DV, DV, POOL_W, POOL_W, GATE_RANK, 2 * D_MODEL)
IN_OFFS = tuple(int(o) for o in np.cumsum((0,) + IN_SIZES))

LANES = 128
SUBLANES = 8
VMEM_LIMIT_BYTES = 56 * 1024 * 1024

MAX_FACTORED_LOG_DECAY = 60.0

SEQ_BLOCK = 512
SEQ_CHUNK = 256
SAMPLE_BLOCK = 32
SAMPLE_SUB = 4


def _mm(a, b):
    return jnp.dot(a, b, preferred_element_type=F32)


def _mm_nt(a, b):
    return lax.dot_general(a, b, (((1,), (1,)), ((), ())), preferred_element_type=F32)


def _mm_tn(a, b):
    return lax.dot_general(a, b, (((0,), (0,)), ((), ())), preferred_element_type=F32)


def _sigmoid(x):
    return 1.0 / (1.0 + jnp.exp(-x))


def _silu(x):
    return x * _sigmoid(x)


def _log_sigmoid(x):
    return jnp.minimum(x, 0.0) - jnp.log1p(jnp.exp(-jnp.abs(x)))


def _rmsnorm(x, g):
    ms = jnp.mean(x * x, axis=-1, keepdims=True)
    return x * lax.rsqrt(ms + EPS) * g


def _row_to_col(row, eye):
    return jnp.sum(eye * row, axis=1, keepdims=True)


def _eye(n):
    r = lax.broadcasted_iota(jnp.int32, (n, n), 0)
    c = lax.broadcasted_iota(jnp.int32, (n, n), 1)
    return (r == c).astype(F32)


W_NAMES = ("norm_g", "w_q", "w_k", "w_v", "w_ga", "w_u", "w_gb", "w_al", "w_mg", "w_alpha",
           "b_alpha", "gla_gain", "w_a", "pool_w", "pool_scale", "w_b", "b_merge", "w_out",
           "final_g")


def _gate_in(x, w):
    xn = _rmsnorm(x, w["norm_g"][...]).astype(BF16)
    alow = _mm(xn, w["w_al"][...])
    q = _mm(xn, w["w_q"][...]) * (DKH ** -0.5)
    k = _mm(xn, w["w_k"][...])
    a = _mm(alow.astype(BF16), w["w_alpha"][...]) + w["b_alpha"][...]
    v = _mm(xn, w["w_v"][...])
    u = _mm(xn, w["w_u"][...])
    return xn, q, k, v, u, a


def _log_gate(a):
    return _log_sigmoid(a) * (1.0 / GATE_TAU)


def _dense_out(x, xn, o, z, w, is_last, pre=None):
    if pre is None:
        pre = (_mm(xn, w["w_ga"][...]), _mm(xn, w["w_gb"][...]), _mm(xn, w["w_mg"][...]))
    ga, gb, mg = pre
    gain = w["gla_gain"][...]
    parts = []
    for h in range(HEADS):
        oh = o[:, h * DVH:(h + 1) * DVH]
        ms = jnp.mean(oh * oh, axis=-1, keepdims=True)
        parts.append(oh * lax.rsqrt(ms + EPS) * gain[:, h * DVH:(h + 1) * DVH])
    on = jnp.concatenate(parts, axis=-1) * _silu(ga)
    ya = _mm(on.astype(BF16), w["w_a"][...])

    yp = []
    for g in range(len(POOL_WINDOWS)):
        zg = z[:, g * POOL_GW:(g + 1) * POOL_GW].astype(BF16)
        yp.append(_mm(zg, w["pool_w"][g]))
    y_pool = jnp.concatenate(yp, axis=-1) * w["pool_scale"][...]
    yb = _mm((y_pool * _silu(gb)).astype(BF16), w["w_b"][...])

    gates = _sigmoid(mg + w["b_merge"][...])
    merged = gates[:, :D_MODEL] * ya + gates[:, D_MODEL:] * yb
    h_new = x + _mm(merged.astype(BF16), w["w_out"][...])
    y = _rmsnorm(h_new, w["final_g"][...]) if is_last else None
    return h_new, y


def _window_sums(e):
    p1 = e + pltpu.roll(e, 1, 0)
    t = p1[:, POOL_GW:]
    p2 = t + pltpu.roll(t, 2, 0)
    t = p2[:, POOL_GW:]
    p3 = t + pltpu.roll(t, 4, 0)
    t = p3[:, POOL_GW:]
    p4 = t + pltpu.roll(t, 8, 0)
    return jnp.concatenate([p1[:, :POOL_GW], p2[:, :POOL_GW], p3[:, :POOL_GW], p4], axis=-1)


def _split_hi_lo(x):
    hi = x.astype(BF16)
    lo = (x - hi.astype(F32)).astype(BF16)
    return hi, lo


def _seq_kernel(*refs, tb, chunk, is_last, pow2_counts):
    n_w = len(W_NAMES)
    h_ref, s0_ref, buf0_ref, cnt_ref = refs[:4]
    w = dict(zip(W_NAMES, refs[4:4 + n_w]))
    hy_ref, s_ref, bufn_ref, o_s, ext_s, a_s, b_s, k_s = refs[4 + n_w:]

    t = pl.program_id(1)

    @pl.when(t == 0)
    def _():
        s_ref[0] = s0_ref[0]
        ext_s[0:POOL_PAD, :] = buf0_ref[0]

    x = h_ref[0]
    xn, q, k, v, u, a_gate = _gate_in(x, w)
    n_chunks = tb // chunk
    chunks = [(c, h) for c in range(n_chunks) for h in range(HEADS)]

    def sl(c, h):
        return (slice(c * chunk, (c + 1) * chunk), slice(h * DKH, (h + 1) * DKH),
                slice(h * DVH, (h + 1) * DVH))

    neg = jnp.minimum(a_gate, 0.0)
    steepest = jnp.sum(neg[0:chunk], axis=0, keepdims=True)
    for c in range(1, n_chunks):
        steepest = jnp.minimum(
            steepest, jnp.sum(neg[c * chunk:(c + 1) * chunk], axis=0, keepdims=True))
    decay_bound = (jnp.min(steepest) - chunk * np.log(2.0)) * (1.0 / GATE_TAU)
    factored_ok = decay_bound >= -MAX_FACTORED_LOG_DECAY
    la = _log_gate(a_gate)

    def finish(factored):
        vb = v.astype(BF16)
        causal = (lax.broadcasted_iota(jnp.int32, (chunk, chunk), 0)
                  >= lax.broadcasted_iota(jnp.int32, (chunk, chunk), 1))
        tril = causal.astype(BF16)
        eye = _eye(DKH)
        la_hi, la_lo = _split_hi_lo(la)
        b = jnp.concatenate(
            [_mm(tril, la_hi[c * chunk:(c + 1) * chunk])
             + _mm(tril, la_lo[c * chunk:(c + 1) * chunk]) for c in range(n_chunks)], axis=0)
        ga = _mm(xn, w["w_ga"][...])
        b_ends = [b[(c + 1) * chunk - 1:(c + 1) * chunk, :] for c in range(n_chunks)]
        to_end = jnp.concatenate(
            [b_ends[c] - b[c * chunk:(c + 1) * chunk, :] for c in range(n_chunks)], axis=0)
        qd = (q * jnp.exp(b)).astype(BF16)
        kd = (k * jnp.exp(to_end)).astype(BF16)

        a_m = {}
        if factored:
            kinv = (k * jnp.exp(-b)).astype(BF16)
            for c, h in chunks:
                rows, ks, _ = sl(c, h)
                a = _mm_nt(qd[rows, ks], kinv[rows, ks])
                a_m[c, h] = jnp.where(causal, a, 0.0).astype(BF16)
        else:
            b_s[...] = b
            k_s[...] = k
            col_id = lax.broadcasted_iota(jnp.int32, (chunk, chunk), 1)
            for c, h in chunks:
                rows, ks, _ = sl(c, h)
                qh = q[rows, ks]
                bh = b[rows, ks]

                def columns(g, acc, c=c, ks=ks, qh=qh, bh=bh):
                    r0 = pl.multiple_of(c * chunk + g * SUBLANES, SUBLANES)
                    b_tile = b_s[pl.ds(r0, SUBLANES), ks]
                    k_tile = k_s[pl.ds(r0, SUBLANES), ks]
                    for r in range(SUBLANES):
                        decay = jnp.exp(jnp.minimum(bh - b_tile[r:r + 1, :], 0.0))
                        col = jnp.sum(qh * k_tile[r:r + 1, :] * decay, axis=1, keepdims=True)
                        acc = jnp.where(col_id == g * SUBLANES + r, col, acc)
                    return acc

                a = lax.fori_loop(0, chunk // SUBLANES, columns, jnp.zeros((chunk, chunk), F32))
                a_s[c, h] = jnp.where(causal, a, 0.0).astype(BF16)
                a_m[c, h] = a_s[c, h]
        gb = _mm(xn, w["w_gb"][...])

        o_in, d_s, dcol = {}, {}, {}
        for c, h in chunks:
            rows, ks, vs = sl(c, h)
            o_in[c, h] = _mm(a_m[c, h], vb[rows, vs])
            d_s[c, h] = _mm_tn(kd[rows, ks], vb[rows, vs])
            dcol[c, h] = _row_to_col(jnp.exp(b_ends[c])[:, ks], eye)
        mg = _mm(xn, w["w_mg"][...])

        s_cur = [s_ref[0, h] for h in range(HEADS)]
        for c, h in chunks:
            rows, ks, vs = sl(c, h)
            o_s[rows, vs] = o_in[c, h] + _mm(qd[rows, ks], s_cur[h].astype(BF16))
            s_cur[h] = dcol[c, h] * s_cur[h] + d_s[c, h]
        for h in range(HEADS):
            s_ref[0, h] = s_cur[h]

        ext_s[POOL_PAD:POOL_PAD + tb, :] = u
        sums = _window_sums(ext_s[...])[POOL_PAD:, :]
        if pow2_counts:
            z = sums * cnt_ref[...] - u
        else:
            z = sums / cnt_ref[...] - u
        bufn_ref[0] = ext_s[tb:tb + POOL_PAD, :]
        ext_s[0:POOL_PAD, :] = ext_s[tb:tb + POOL_PAD, :]

        h_new, y = _dense_out(x, xn, o_s[...], z, w, is_last, pre=(ga, gb, mg))
        hy_ref[0] = y if is_last else h_new

    pl.when(factored_ok)(functools.partial(finish, True))
    pl.when(jnp.logical_not(factored_ok))(functools.partial(finish, False))


def _const_spec(shape):
    nd = len(shape)
    return pl.BlockSpec(shape, lambda *_: (0,) * nd, pipeline_mode=pl.Buffered(1))


def _seq_layer(h, s0, buf0, cnt, wl, *, is_last, pow2_counts):
    bn, t_len, _ = h.shape
    tb = min(SEQ_BLOCK, t_len)
    chunk = min(SEQ_CHUNK, tb)
    assert t_len % tb == 0 and tb % chunk == 0
    shared = s0.shape[0] == 1
    bsel = (lambda b, t: (0, 0, 0, 0)) if shared else (lambda b, t: (b, 0, 0, 0))
    bsel3 = (lambda b, t: (0, 0, 0)) if shared else (lambda b, t: (b, 0, 0))
    weights = [wl[n] for n in W_NAMES]
    in_specs = [
        pl.BlockSpec((1, tb, D_MODEL), lambda b, t: (b, t, 0)),
        pl.BlockSpec((1, HEADS, DKH, DVH), bsel),
        pl.BlockSpec((1, POOL_PAD, POOL_W), bsel3),
        _const_spec(cnt.shape),
    ] + [_const_spec(a.shape) for a in weights]
    out_shape = [jax.ShapeDtypeStruct(h.shape, F32),
                 jax.ShapeDtypeStruct((bn, HEADS, DKH, DVH), F32),
                 jax.ShapeDtypeStruct((bn, POOL_PAD, POOL_W), F32)]
    out_specs = [pl.BlockSpec((1, tb, D_MODEL), lambda b, t: (b, t, 0)),
                 pl.BlockSpec((1, HEADS, DKH, DVH), lambda b, t: (b, 0, 0, 0)),
                 pl.BlockSpec((1, POOL_PAD, POOL_W), lambda b, t: (b, 0, 0))]
    kern = functools.partial(_seq_kernel, tb=tb, chunk=chunk, is_last=is_last,
                             pow2_counts=pow2_counts)
    return pl.pallas_call(
        kern,
        grid=(bn, t_len // tb),
        in_specs=in_specs,
        out_specs=out_specs,
        out_shape=out_shape,
        scratch_shapes=[pltpu.VMEM((tb, DV), F32), pltpu.VMEM((tb + POOL_PAD, POOL_W), F32),
                        pltpu.VMEM((tb // chunk, HEADS, chunk, chunk), BF16),
                        pltpu.VMEM((tb, DK), F32), pltpu.VMEM((tb, DK), F32)],
        compiler_params=pltpu.CompilerParams(
            dimension_semantics=("arbitrary", "arbitrary"), vmem_limit_bytes=VMEM_LIMIT_BYTES),
        name="seq_layer",
    )(h, s0, buf0, cnt, *weights)


def _step_kernel(*refs, n_tok, sub, layer, is_last):
    n_w = len(W_NAMES)
    n_in = 5 if layer else 4
    h_ref, s0_ref, buf0_ref, cnt_ref = refs[:4]
    sprev_ref = refs[4] if layer else None
    w = dict(zip(W_NAMES, refs[n_in:n_in + n_w]))
    hy_ref, sall_ref, bufn_ref = refs[n_in + n_w:n_in + n_w + 3]
    xn_s, q_s, k_s, v_s, la_s, u_s, o_s, z_s = refs[n_in + n_w + 3:]

    j = pl.program_id(1)
    nj = pl.num_programs(1)
    pair = SUBLANES // n_tok
    assert pair == 2 and sub % pair == 0
    if layer:
        sall_ref[0:layer] = sprev_ref[...]
    s_ref = sall_ref.at[layer]

    @pl.when(j == 0)
    def _():
        xn, q, k, v, u, a_gate = _gate_in(h_ref[...], w)
        la = _log_gate(a_gate)
        xn_s[...] = xn
        q_s[...] = q
        k_s[...] = k
        v_s[...] = v
        la_s[...] = la
        u_s[...] = u

    nr = sub * n_tok
    rows = pl.ds(pl.multiple_of(j * nr, nr), nr)
    row = lax.broadcasted_iota(jnp.int32, (nr, 1), 0)
    tok = row % n_tok
    seq = row // n_tok
    eye = _eye(DKH)
    b = la_s[rows, :]
    step = 1
    while step < n_tok:
        b = b + jnp.where(tok >= step, pltpu.roll(b, step, 0), 0.0)
        step *= 2
    b_end = b
    for d in range(1, n_tok):
        b_end = jnp.where(tok == n_tok - 1 - d, pltpu.roll(b, nr - d, 0), b_end)
    q = q_s[rows, :]
    k = k_s[rows, :]
    v = v_s[rows, :]
    qd = (q * jnp.exp(b)).astype(BF16)
    kd = k * jnp.exp(b_end - b)
    dec = jnp.exp(b_end)
    vb = v.astype(BF16)
    o_parts = []
    for h in range(HEADS):
        ks = slice(h * DKH, (h + 1) * DKH)
        vs = slice(h * DVH, (h + 1) * DVH)
        o = jnp.sum(q[:, ks] * k[:, ks], axis=1, keepdims=True) * v[:, vs]
        for d in range(1, n_tok):
            kj = pltpu.roll(k[:, ks], d, 0)
            bj = pltpu.roll(b[:, ks], d, 0)
            score = jnp.sum(q[:, ks] * kj * jnp.exp(jnp.minimum(b[:, ks] - bj, 0.0)), axis=1,
                            keepdims=True)
            o = o + jnp.where(tok >= d, score, 0.0) * pltpu.roll(v[:, vs], d, 0)
        o_parts.append(o)
    for h in range(HEADS):
        ks = slice(h * DKH, (h + 1) * DKH)
        vs = slice(h * DVH, (h + 1) * DVH)
        o = o_parts[h]
        for s in range(sub):
            mine = seq == s
            s_old = s0_ref[s, h]
            o = o + jnp.where(mine, _mm(qd[:, ks], s_old.astype(BF16)), 0.0)
            kd_s = jnp.where(mine, kd[:, ks], 0.0).astype(BF16)
            dcol = _row_to_col(dec[s * n_tok:s * n_tok + 1, ks], eye)
            s_ref[s, h] = dcol * s_old + _mm_tn(kd_s, vb[:, vs])
        o_parts[h] = o
    o_s[rows, :] = jnp.concatenate(o_parts, axis=-1)

    first = lax.broadcasted_iota(jnp.int32, (SUBLANES, 1), 0) < n_tok
    inv_cnt = cnt_ref[...]
    for p in range(sub // pair):
        r0 = pl.multiple_of(j * nr + p * SUBLANES, SUBLANES)
        rows = pl.ds(r0, SUBLANES)
        u = u_s[rows, :]
        e0 = jnp.concatenate([buf0_ref[pair * p], u], axis=0)
        e1 = jnp.concatenate([buf0_ref[pair * p + 1], pltpu.roll(u, n_tok, 0)], axis=0)
        n_ext = POOL_PAD + SUBLANES
        w0 = _window_sums(e0)[POOL_PAD:, :]
        w1 = _window_sums(e1)[POOL_PAD:, :]
        sums = jnp.where(first, w0, pltpu.roll(w1, n_tok, 0))
        z_s[rows, :] = sums * inv_cnt - u
        bufn_ref[pair * p] = pltpu.roll(e0, n_ext - n_tok, 0)[0:POOL_PAD, :]
        bufn_ref[pair * p + 1] = pltpu.roll(e1, n_ext - n_tok, 0)[0:POOL_PAD, :]

    @pl.when(j == nj - 1)
    def _():
        h_new, y = _dense_out(h_ref[...], xn_s[...], o_s[...], z_s[...], w, is_last)
        hy_ref[...] = y if is_last else h_new


def _step_layer(h, s_all, s_prev, buf0, cnt, wl, *, n_tok, layer, is_last):
    rows, _ = h.shape
    bn = rows // n_tok
    sub, blk = SAMPLE_SUB, SAMPLE_BLOCK
    assert bn % blk == 0 and blk % sub == 0
    n_inner = blk // sub
    brows = blk * n_tok
    weights = [wl[n] for n in W_NAMES]
    in_specs = [
        pl.BlockSpec((brows, D_MODEL), lambda i, j: (i, 0)),
        pl.BlockSpec((None, sub, HEADS, DKH, DVH), lambda i, j: (layer, i * n_inner + j, 0, 0, 0)),
        pl.BlockSpec((sub, POOL_PAD, POOL_W), lambda i, j: (i * n_inner + j, 0, 0)),
        _const_spec(cnt.shape),
    ]
    args = [h, s_all, buf0, cnt]
    if layer:
        in_specs.append(pl.BlockSpec((layer, sub, HEADS, DKH, DVH),
                                     lambda i, j: (0, i * n_inner + j, 0, 0, 0)))
        args.append(s_prev)
    in_specs += [_const_spec(a.shape) for a in weights]
    out_shape = [jax.ShapeDtypeStruct(h.shape, F32),
                 jax.ShapeDtypeStruct((layer + 1, bn, HEADS, DKH, DVH), F32),
                 jax.ShapeDtypeStruct((bn, POOL_PAD, POOL_W), F32)]
    out_specs = [pl.BlockSpec((brows, D_MODEL), lambda i, j: (i, 0)),
                 pl.BlockSpec((layer + 1, sub, HEADS, DKH, DVH),
                              lambda i, j: (0, i * n_inner + j, 0, 0, 0)),
                 pl.BlockSpec((sub, POOL_PAD, POOL_W), lambda i, j: (i * n_inner + j, 0, 0))]
    kern = functools.partial(_step_kernel, n_tok=n_tok, sub=sub, layer=layer, is_last=is_last)
    scratch = [pltpu.VMEM((brows, D_MODEL), BF16), pltpu.VMEM((brows, DK), F32),
               pltpu.VMEM((brows, DK), F32), pltpu.VMEM((brows, DV), F32),
               pltpu.VMEM((brows, DK), F32), pltpu.VMEM((brows, POOL_W), F32),
               pltpu.VMEM((brows, DV), F32), pltpu.VMEM((brows, POOL_W), F32)]
    return pl.pallas_call(
        kern,
        grid=(bn // blk, n_inner),
        in_specs=in_specs,
        out_specs=out_specs,
        out_shape=out_shape,
        scratch_shapes=scratch,
        compiler_params=pltpu.CompilerParams(
            dimension_semantics=("arbitrary", "arbitrary"), vmem_limit_bytes=VMEM_LIMIT_BYTES),
        name="step_layer",
    )(*args, *weights)


def _layer_weights(l, norm_g, w_in, w_alpha, b_alpha, gla_gain, w_a, pool_w, pool_scale, w_b,
                   b_merge, w_out, final_norm_g):
    seg = [w_in[l, :, IN_OFFS[i]:IN_OFFS[i + 1]].astype(BF16) for i in range(len(IN_SIZES))]
    pad = LANES - GATE_RANK
    return {
        "norm_g": norm_g[l][None, :],
        "w_q": seg[0], "w_k": seg[1], "w_v": seg[2], "w_ga": seg[3], "w_u": seg[4],
        "w_gb": seg[5],
        "w_al": jnp.pad(seg[6], ((0, 0), (0, pad))),
        "w_mg": seg[7],
        "w_alpha": jnp.pad(w_alpha[l].astype(BF16), ((0, pad), (0, 0))),
        "b_alpha": b_alpha[l][None, :],
        "gla_gain": gla_gain[l][None, :],
        "w_a": w_a[l].astype(BF16),
        "pool_w": pool_w[l].astype(BF16),
        "pool_scale": pool_scale[l][None, :],
        "w_b": w_b[l].astype(BF16),
        "b_merge": b_merge[l][None, :],
        "w_out": w_out[l].astype(BF16),
        "final_g": final_norm_g[None, :],
    }


def _window_row():
    return np.repeat(np.asarray(POOL_WINDOWS, np.float32), POOL_GW)[None, :]


def kernel(x_prompt, x_sample, state_gla, state_pool, meta_tokens, norm_g, w_in, w_alpha, b_alpha,
           gla_gain, w_a, pool_w, pool_scale, w_b, b_merge, w_out, final_norm_g):
    depth = w_in.shape[0]
    dec_b, dec_t, _ = x_sample.shape
    wrow = _window_row()
    cnt_meta = jnp.asarray(np.minimum(wrow, np.arange(1, N_META + 1, dtype=np.float32)[:, None]))
    inv_w = jnp.asarray(1.0 / wrow)

    h_meta = meta_tokens[None].astype(F32)
    h_p = x_prompt
    h_s = x_sample.reshape(dec_b * dec_t, D_MODEL)
    zero_s = jnp.zeros((1, HEADS, DKH, DVH), F32)
    zero_buf = jnp.zeros((1, POOL_PAD, POOL_W), F32)
    buf_s = jnp.pad(state_pool, ((0, 0), (0, 0), (POOL_PAD - POOL_BUF, 0), (0, 0)))

    sg_p, sp_p, sp_s = [], [], []
    sg_s = None
    for l in range(depth):
        last = l == depth - 1
        wl = _layer_weights(l, norm_g, w_in, w_alpha, b_alpha, gla_gain, w_a, pool_w, pool_scale,
                            w_b, b_merge, w_out, final_norm_g)
        h_meta, s_meta, buf_meta = _seq_layer(h_meta, zero_s, zero_buf, cnt_meta, wl,
                                              is_last=False, pow2_counts=False)
        h_p, s_p, b_p = _seq_layer(h_p, s_meta, buf_meta, inv_w, wl, is_last=last,
                                   pow2_counts=True)
        h_s, sg_s, b_s = _step_layer(h_s, state_gla, sg_s, buf_s[l], inv_w, wl, n_tok=dec_t,
                                     layer=l, is_last=last)
        sg_p.append(s_p)
        sp_p.append(b_p[:, POOL_PAD - POOL_BUF:, :])
        sp_s.append(b_s[:, POOL_PAD - POOL_BUF:, :])
    return (h_p, h_s.reshape(dec_b, dec_t, D_MODEL), jnp.stack(sg_p), jnp.stack(sp_p),
            sg_s, jnp.stack(sp_s))
```

```python
import functools

import numpy as np
import jax
import jax.numpy as jnp
from jax import lax
from jax.experimental import pallas as pl
from jax.experimental.pallas import tpu as pltpu

F32 = jnp.float32
BF16 = jnp.bfloat16

D_MODEL = 1024
N_META = 16
HEADS = 4
DKH = 128
DVH = 256
DK = HEADS * DKH
DV = HEADS * DVH
GATE_RANK = 16
GATE_TAU = 16.0
POOL_W = 512
POOL_WINDOWS = (2, 4, 8, 16)
POOL_GW = POOL_W // len(POOL_WINDOWS)
POOL_BUF = max(POOL_WINDOWS) - 1
POOL_PAD = POOL_BUF + 1
PAST_LEN = 16384
EPS = 1e-6
IN_SIZES = (DK, DK, DV, DV, POOL_W, POOL_W, GATE_RANK, 2 * D_MODEL)
IN_OFFS = tuple(int(o) for o in np.cumsum((0,) + IN_SIZES))

LANES = 128
SUBLANES = 8
VMEM_LIMIT_BYTES = 56 * 1024 * 1024

MAX_FACTORED_LOG_DECAY = 60.0

SEQ_BLOCK = 512
SEQ_CHUNK = 256
SAMPLE_BLOCK = 32
SAMPLE_SUB = 8
PREP_ROWS = 128


def _mm(a, b):
    return jnp.dot(a, b, preferred_element_type=F32)


def _mm_nt(a, b):
    return lax.dot_general(a, b, (((1,), (1,)), ((), ())), preferred_element_type=F32)


def _mm_tn(a, b):
    return lax.dot_general(a, b, (((0,), (0,)), ((), ())), preferred_element_type=F32)


def _sigmoid(x):
    return 1.0 / (1.0 + jnp.exp(-x))


def _silu(x):
    return x * _sigmoid(x)


def _log_sigmoid(x):
    return jnp.minimum(x, 0.0) - jnp.log1p(jnp.exp(-jnp.abs(x)))


def _rmsnorm(x, g):
    ms = jnp.mean(x * x, axis=-1, keepdims=True)
    return x * lax.rsqrt(ms + EPS) * g


def _row_to_col(row, eye):
    return jnp.sum(eye * row, axis=1, keepdims=True)


def _eye(n):
    r = lax.broadcasted_iota(jnp.int32, (n, n), 0)
    c = lax.broadcasted_iota(jnp.int32, (n, n), 1)
    return (r == c).astype(F32)


W_NAMES = ("norm_g", "w_main", "w_al", "w_mg", "w_alpha", "b_alpha", "gla_gain", "w_a", "pool_w",
           "pool_scale", "w_b", "b_merge", "w_out", "final_g")
MAIN_SEGS = ("w_q", "w_k", "w_v", "w_ga", "w_u", "w_gb")
MAIN_COLS = IN_OFFS[len(MAIN_SEGS)]


def _weight_views(refs):
    w = dict(zip(W_NAMES, refs))
    for i, name in enumerate(MAIN_SEGS):
        w[name] = w["w_main"].at[:, IN_OFFS[i]:IN_OFFS[i + 1]]
    return w


def _gate_in(x, w):
    xn = _rmsnorm(x, w["norm_g"][...]).astype(BF16)
    alow = _mm(xn, w["w_al"][...])
    q = _mm(xn, w["w_q"][...]) * (DKH ** -0.5)
    k = _mm(xn, w["w_k"][...])
    a = _mm(alow.astype(BF16), w["w_alpha"][...]) + w["b_alpha"][...]
    v = _mm(xn, w["w_v"][...])
    u = _mm(xn, w["w_u"][...])
    return xn, q, k, v, u, a


def _log_gate(a):
    return _log_sigmoid(a) * (1.0 / GATE_TAU)


def _dense_out(x, xn, o, z, w, is_last, pre=None):
    if pre is None:
        pre = (_mm(xn, w["w_ga"][...]), _mm(xn, w["w_gb"][...]), _mm(xn, w["w_mg"][...]))
    ga, gb, mg = pre
    gain = w["gla_gain"][...]
    parts = []
    for h in range(HEADS):
        oh = o[:, h * DVH:(h + 1) * DVH]
        ms = jnp.mean(oh * oh, axis=-1, keepdims=True)
        parts.append(oh * lax.rsqrt(ms + EPS) * gain[:, h * DVH:(h + 1) * DVH])
    on = jnp.concatenate(parts, axis=-1) * _silu(ga)
    ya = _mm(on.astype(BF16), w["w_a"][...])

    yp = []
    for g in range(len(POOL_WINDOWS)):
        zg = z[:, g * POOL_GW:(g + 1) * POOL_GW].astype(BF16)
        yp.append(_mm(zg, w["pool_w"][g]))
    y_pool = jnp.concatenate(yp, axis=-1) * w["pool_scale"][...]
    yb = _mm((y_pool * _silu(gb)).astype(BF16), w["w_b"][...])

    gates = _sigmoid(mg + w["b_merge"][...])
    merged = gates[:, :D_MODEL] * ya + gates[:, D_MODEL:] * yb
    h_new = x + _mm(merged.astype(BF16), w["w_out"][...])
    y = _rmsnorm(h_new, w["final_g"][...]) if is_last else None
    return h_new, y


def _window_sums(e):
    p1 = e + pltpu.roll(e, 1, 0)
    t = p1[:, POOL_GW:]
    p2 = t + pltpu.roll(t, 2, 0)
    t = p2[:, POOL_GW:]
    p3 = t + pltpu.roll(t, 4, 0)
    t = p3[:, POOL_GW:]
    p4 = t + pltpu.roll(t, 8, 0)
    return jnp.concatenate([p1[:, :POOL_GW], p2[:, :POOL_GW], p3[:, :POOL_GW], p4], axis=-1)


def _split_hi_lo(x):
    hi = x.astype(BF16)
    lo = (x - hi.astype(F32)).astype(BF16)
    return hi, lo


def _seq_kernel(*refs, tb, chunk, is_last, pow2_counts):
    n_w = len(W_NAMES)
    h_ref, s0_ref, buf0_ref, cnt_ref = refs[:4]
    w = _weight_views(refs[4:4 + n_w])
    hy_ref, s_ref, bufn_ref, o_s, ext_s, a_s, b_s, k_s = refs[4 + n_w:]

    t = pl.program_id(1)

    @pl.when(t == 0)
    def _():
        s_ref[0] = s0_ref[0]
        ext_s[0:POOL_PAD, :] = buf0_ref[0]

    x = h_ref[0]
    xn, q, k, v, u, a_gate = _gate_in(x, w)
    n_chunks = tb // chunk
    chunks = [(c, h) for c in range(n_chunks) for h in range(HEADS)]

    def sl(c, h):
        return (slice(c * chunk, (c + 1) * chunk), slice(h * DKH, (h + 1) * DKH),
                slice(h * DVH, (h + 1) * DVH))

    neg = jnp.minimum(a_gate, 0.0)
    steepest = jnp.sum(neg[0:chunk], axis=0, keepdims=True)
    for c in range(1, n_chunks):
        steepest = jnp.minimum(
            steepest, jnp.sum(neg[c * chunk:(c + 1) * chunk], axis=0, keepdims=True))
    decay_bound = (jnp.min(steepest) - chunk * np.log(2.0)) * (1.0 / GATE_TAU)
    factored_ok = decay_bound >= -MAX_FACTORED_LOG_DECAY
    la = _log_gate(a_gate)

    def finish(factored):
        vb = v.astype(BF16)
        causal = (lax.broadcasted_iota(jnp.int32, (chunk, chunk), 0)
                  >= lax.broadcasted_iota(jnp.int32, (chunk, chunk), 1))
        tril = causal.astype(BF16)
        eye = _eye(DKH)
        la_hi, la_lo = _split_hi_lo(la)
        b = jnp.concatenate(
            [_mm(tril, la_hi[c * chunk:(c + 1) * chunk])
             + _mm(tril, la_lo[c * chunk:(c + 1) * chunk]) for c in range(n_chunks)], axis=0)
        ga = _mm(xn, w["w_ga"][...])
        b_ends = [b[(c + 1) * chunk - 1:(c + 1) * chunk, :] for c in range(n_chunks)]
        to_end = jnp.concatenate(
            [b_ends[c] - b[c * chunk:(c + 1) * chunk, :] for c in range(n_chunks)], axis=0)
        qd = (q * jnp.exp(b)).astype(BF16)
        kd = (k * jnp.exp(to_end)).astype(BF16)

        a_m = {}
        if factored:
            kinv = (k * jnp.exp(-b)).astype(BF16)
            for c, h in chunks:
                rows, ks, _ = sl(c, h)
                a = _mm_nt(qd[rows, ks], kinv[rows, ks])
                a_m[c, h] = jnp.where(causal, a, 0.0).astype(BF16)
        else:
            b_s[...] = b
            k_s[...] = k
            col_id = lax.broadcasted_iota(jnp.int32, (chunk, chunk), 1)
            for c, h in chunks:
                rows, ks, _ = sl(c, h)
                qh = q[rows, ks]
                bh = b[rows, ks]

                def columns(g, acc, c=c, ks=ks, qh=qh, bh=bh):
                    r0 = pl.multiple_of(c * chunk + g * SUBLANES, SUBLANES)
                    b_tile = b_s[pl.ds(r0, SUBLANES), ks]
                    k_tile = k_s[pl.ds(r0, SUBLANES), ks]
                    for r in range(SUBLANES):
                        decay = jnp.exp(jnp.minimum(bh - b_tile[r:r + 1, :], 0.0))
                        col = jnp.sum(qh * k_tile[r:r + 1, :] * decay, axis=1, keepdims=True)
                        acc = jnp.where(col_id == g * SUBLANES + r, col, acc)
                    return acc

                a = lax.fori_loop(0, chunk // SUBLANES, columns, jnp.zeros((chunk, chunk), F32))
                a_s[c, h] = jnp.where(causal, a, 0.0).astype(BF16)
                a_m[c, h] = a_s[c, h]
        gb = _mm(xn, w["w_gb"][...])

        o_in, d_s, dcol = {}, {}, {}
        for c, h in chunks:
            rows, ks, vs = sl(c, h)
            o_in[c, h] = _mm(a_m[c, h], vb[rows, vs])
            d_s[c, h] = _mm_tn(kd[rows, ks], vb[rows, vs])
            dcol[c, h] = _row_to_col(jnp.exp(b_ends[c])[:, ks], eye)
        mg = _mm(xn, w["w_mg"][...])

        s_cur = [s_ref[0, h] for h in range(HEADS)]
        for c, h in chunks:
            rows, ks, vs = sl(c, h)
            o_s[rows, vs] = o_in[c, h] + _mm(qd[rows, ks], s_cur[h].astype(BF16))
            s_cur[h] = dcol[c, h] * s_cur[h] + d_s[c, h]
        for h in range(HEADS):
            s_ref[0, h] = s_cur[h]

        ext_s[POOL_PAD:POOL_PAD + tb, :] = u
        sums = _window_sums(ext_s[...])[POOL_PAD:, :]
        if pow2_counts:
            z = sums * cnt_ref[...] - u
        else:
            z = sums / cnt_ref[...] - u
        bufn_ref[0] = ext_s[tb:tb + POOL_PAD, :]
        ext_s[0:POOL_PAD, :] = ext_s[tb:tb + POOL_PAD, :]

        h_new, y = _dense_out(x, xn, o_s[...], z, w, is_last, pre=(ga, gb, mg))
        hy_ref[0] = y if is_last else h_new

    pl.when(factored_ok)(functools.partial(finish, True))
    pl.when(jnp.logical_not(factored_ok))(functools.partial(finish, False))


def _const_spec(shape):
    nd = len(shape)
    return pl.BlockSpec(shape, lambda *_: (0,) * nd, pipeline_mode=pl.Buffered(1))


def _weight_specs(ws, layer_of):
    specs = []
    for name in W_NAMES:
        shape = ws[name].shape
        if name == "final_g":
            specs.append(_const_spec(shape))
        else:
            zeros = (0,) * (len(shape) - 1)
            specs.append(pl.BlockSpec((None,) + shape[1:],
                                      lambda *g, zeros=zeros: (layer_of(*g),) + zeros,
                                      pipeline_mode=pl.Buffered(1)))
    return specs


def _seq_layer(h, s0, buf0, cnt, ws, layer, *, is_last, pow2_counts):
    bn, t_len, _ = h.shape
    tb = min(SEQ_BLOCK, t_len)
    chunk = min(SEQ_CHUNK, tb)
    assert t_len % tb == 0 and tb % chunk == 0
    shared = s0.shape[0] == 1
    bsel = (lambda b, t: (0, 0, 0, 0)) if shared else (lambda b, t: (b, 0, 0, 0))
    bsel3 = (lambda b, t: (0, 0, 0)) if shared else (lambda b, t: (b, 0, 0))
    weights = [ws[n] for n in W_NAMES]
    in_specs = [
        pl.BlockSpec((1, tb, D_MODEL), lambda b, t: (b, t, 0)),
        pl.BlockSpec((1, HEADS, DKH, DVH), bsel),
        pl.BlockSpec((1, POOL_PAD, POOL_W), bsel3),
        _const_spec(cnt.shape),
    ] + _weight_specs(ws, lambda b, t: layer)
    out_shape = [jax.ShapeDtypeStruct(h.shape, F32),
                 jax.ShapeDtypeStruct((bn, HEADS, DKH, DVH), F32),
                 jax.ShapeDtypeStruct((bn, POOL_PAD, POOL_W), F32)]
    out_specs = [pl.BlockSpec((1, tb, D_MODEL), lambda b, t: (b, t, 0)),
                 pl.BlockSpec((1, HEADS, DKH, DVH), lambda b, t: (b, 0, 0, 0)),
                 pl.BlockSpec((1, POOL_PAD, POOL_W), lambda b, t: (b, 0, 0))]
    kern = functools.partial(_seq_kernel, tb=tb, chunk=chunk, is_last=is_last,
                             pow2_counts=pow2_counts)
    return pl.pallas_call(
        kern,
        grid=(bn, t_len // tb),
        in_specs=in_specs,
        out_specs=out_specs,
        out_shape=out_shape,
        scratch_shapes=[pltpu.VMEM((tb, DV), F32), pltpu.VMEM((tb + POOL_PAD, POOL_W), F32),
                        pltpu.VMEM((tb // chunk, HEADS, chunk, chunk), BF16),
                        pltpu.VMEM((tb, DK), F32), pltpu.VMEM((tb, DK), F32)],
        compiler_params=pltpu.CompilerParams(
            dimension_semantics=("arbitrary", "arbitrary"), vmem_limit_bytes=VMEM_LIMIT_BYTES),
        name="seq_layer",
    )(h, s0, buf0, cnt, *weights)


def _step_kernel(*refs, n_tok, sub, depth):
    n_w = len(W_NAMES)
    x_ref, s0_ref, buf0_ref, cnt_ref = refs[:4]
    w = _weight_views(refs[4:4 + n_w])
    y_ref, s_ref, bufn_ref = refs[4 + n_w:4 + n_w + 3]
    h_s, xin_s, xn_s, q_s, k_s, v_s, la_s, u_s, o_s, z_s = refs[4 + n_w + 3:]

    layer = pl.program_id(0)
    blk = pl.program_id(1)
    j = pl.program_id(2)
    nj = pl.num_programs(2)
    brows = xin_s.shape[0]
    blk_rows = pl.ds(pl.multiple_of(blk * brows, brows), brows)
    pair = SUBLANES // n_tok
    assert pair == 2 and sub % pair == 0

    @pl.when(j == 0)
    def _():
        @pl.when(layer == 0)
        def _():
            xin_s[...] = x_ref[...]

        @pl.when(layer > 0)
        def _():
            xin_s[...] = h_s[blk_rows, :]

        xn, q, k, v, u, a_gate = _gate_in(xin_s[...], w)
        la = _log_gate(a_gate)
        xn_s[...] = xn
        q_s[...] = q
        k_s[...] = k
        v_s[...] = v
        la_s[...] = la
        u_s[...] = u

    nr = sub * n_tok
    rows = pl.ds(pl.multiple_of(j * nr, nr), nr)
    row = lax.broadcasted_iota(jnp.int32, (nr, 1), 0)
    tok = row % n_tok
    seq = row // n_tok
    eye = _eye(DKH)
    b = la_s[rows, :]
    step = 1
    while step < n_tok:
        b = b + jnp.where(tok >= step, pltpu.roll(b, step, 0), 0.0)
        step *= 2
    b_end = b
    for d in range(1, n_tok):
        b_end = jnp.where(tok == n_tok - 1 - d, pltpu.roll(b, nr - d, 0), b_end)
    q = q_s[rows, :]
    k = k_s[rows, :]
    v = v_s[rows, :]
    qd = (q * jnp.exp(b)).astype(BF16)
    kd = k * jnp.exp(b_end - b)
    dec = jnp.exp(b_end)
    vb = v.astype(BF16)
    o_parts = []
    for h in range(HEADS):
        ks = slice(h * DKH, (h + 1) * DKH)
        vs = slice(h * DVH, (h + 1) * DVH)
        o = jnp.sum(q[:, ks] * k[:, ks], axis=1, keepdims=True) * v[:, vs]
        for d in range(1, n_tok):
            kj = pltpu.roll(k[:, ks], d, 0)
            bj = pltpu.roll(b[:, ks], d, 0)
            score = jnp.sum(q[:, ks] * kj * jnp.exp(jnp.minimum(b[:, ks] - bj, 0.0)), axis=1,
                            keepdims=True)
            o = o + jnp.where(tok >= d, score, 0.0) * pltpu.roll(v[:, vs], d, 0)
        o_parts.append(o)
    for h in range(HEADS):
        ks = slice(h * DKH, (h + 1) * DKH)
        vs = slice(h * DVH, (h + 1) * DVH)
        o = o_parts[h]
        for s in range(sub):
            mine = seq == s
            s_old = s0_ref[s, h]
            o = o + jnp.where(mine, _mm(qd[:, ks], s_old.astype(BF16)), 0.0)
            kd_s = jnp.where(mine, kd[:, ks], 0.0).astype(BF16)
            dcol = _row_to_col(dec[s * n_tok:s * n_tok + 1, ks], eye)
            s_ref[s, h] = dcol * s_old + _mm_tn(kd_s, vb[:, vs])
        o_parts[h] = o
    o_s[rows, :] = jnp.concatenate(o_parts, axis=-1)

    first = lax.broadcasted_iota(jnp.int32, (SUBLANES, 1), 0) < n_tok
    inv_cnt = cnt_ref[...]
    for p in range(sub // pair):
        r0 = pl.multiple_of(j * nr + p * SUBLANES, SUBLANES)
        rows = pl.ds(r0, SUBLANES)
        u = u_s[rows, :]
        e0 = jnp.concatenate([buf0_ref[pair * p], u], axis=0)
        e1 = jnp.concatenate([buf0_ref[pair * p + 1], pltpu.roll(u, n_tok, 0)], axis=0)
        n_ext = POOL_PAD + SUBLANES
        w0 = _window_sums(e0)[POOL_PAD:, :]
        w1 = _window_sums(e1)[POOL_PAD:, :]
        sums = jnp.where(first, w0, pltpu.roll(w1, n_tok, 0))
        z_s[rows, :] = sums * inv_cnt - u
        bufn_ref[pair * p] = pltpu.roll(e0, n_ext - n_tok, 0)[0:POOL_PAD, :]
        bufn_ref[pair * p + 1] = pltpu.roll(e1, n_ext - n_tok, 0)[0:POOL_PAD, :]

    @pl.when(j == nj - 1)
    def _():
        h_new, y = _dense_out(xin_s[...], xn_s[...], o_s[...], z_s[...], w, True)

        @pl.when(layer < depth - 1)
        def _():
            h_s[blk_rows, :] = h_new

        @pl.when(layer == depth - 1)
        def _():
            y_ref[...] = y


def _step_layers(x, s_all, buf_all, cnt, ws, *, n_tok):
    rows, _ = x.shape
    depth, bn = s_all.shape[:2]
    sub, blk = SAMPLE_SUB, SAMPLE_BLOCK
    assert rows == bn * n_tok and bn % blk == 0 and blk % sub == 0
    n_blk, n_inner = bn // blk, blk // sub
    brows = blk * n_tok
    weights = [ws[n] for n in W_NAMES]

    def state_idx(l, i, j):
        return (l, i * n_inner + j, 0, 0, 0)

    def buf_idx(l, i, j):
        return (l, i * n_inner + j, 0, 0)

    in_specs = [
        pl.BlockSpec((brows, D_MODEL), lambda l, i, j: (i, 0)),
        pl.BlockSpec((None, sub, HEADS, DKH, DVH), state_idx),
        pl.BlockSpec((None, sub, POOL_PAD, POOL_W), buf_idx),
        _const_spec(cnt.shape),
    ] + _weight_specs(ws, lambda l, i, j: l)
    out_shape = [jax.ShapeDtypeStruct(x.shape, F32),
                 jax.ShapeDtypeStruct(s_all.shape, F32),
                 jax.ShapeDtypeStruct(buf_all.shape, F32)]
    out_specs = [pl.BlockSpec((brows, D_MODEL),
                              lambda l, i, j: (jnp.where(l == depth - 1, i, 0), 0)),
                 pl.BlockSpec((None, sub, HEADS, DKH, DVH), state_idx),
                 pl.BlockSpec((None, sub, POOL_PAD, POOL_W), buf_idx)]
    kern = functools.partial(_step_kernel, n_tok=n_tok, sub=sub, depth=depth)
    scratch = [pltpu.VMEM((rows, D_MODEL), F32), pltpu.VMEM((brows, D_MODEL), F32),
               pltpu.VMEM((brows, D_MODEL), BF16), pltpu.VMEM((brows, DK), F32),
               pltpu.VMEM((brows, DK), F32), pltpu.VMEM((brows, DV), F32),
               pltpu.VMEM((brows, DK), F32), pltpu.VMEM((brows, POOL_W), F32),
               pltpu.VMEM((brows, DV), F32), pltpu.VMEM((brows, POOL_W), F32)]
    return pl.pallas_call(
        kern,
        grid=(depth, n_blk, n_inner),
        in_specs=in_specs,
        out_specs=out_specs,
        out_shape=out_shape,
        scratch_shapes=scratch,
        compiler_params=pltpu.CompilerParams(
            dimension_semantics=("arbitrary", "arbitrary", "arbitrary"),
            vmem_limit_bytes=VMEM_LIMIT_BYTES),
        name="step_layers",
    )(x, s_all, buf_all, cnt, *weights)


def _prep_w_in_kernel(w_ref, main_ref, al_ref, mg_ref):
    main_ref[...] = w_ref[:, 0:MAIN_COLS].astype(BF16)
    tail = w_ref[:, MAIN_COLS:]
    lane = lax.broadcasted_iota(jnp.int32, (tail.shape[0], LANES), 1)
    al_ref[...] = jnp.where(lane < GATE_RANK, tail[:, 0:LANES], 0.0).astype(BF16)
    mg_ref[...] = tail[:, GATE_RANK:].astype(BF16)


def _prep_w_in(w_in):
    depth, rows, cols = w_in.shape
    mg_cols = cols - MAIN_COLS - GATE_RANK
    rb = PREP_ROWS
    assert rows % rb == 0
    return pl.pallas_call(
        _prep_w_in_kernel,
        grid=(depth, rows // rb),
        in_specs=[pl.BlockSpec((None, rb, cols), lambda l, i: (l, i, 0))],
        out_specs=[pl.BlockSpec((None, rb, MAIN_COLS), lambda l, i: (l, i, 0)),
                   pl.BlockSpec((None, rb, LANES), lambda l, i: (l, i, 0)),
                   pl.BlockSpec((None, rb, mg_cols), lambda l, i: (l, i, 0))],
        out_shape=[jax.ShapeDtypeStruct((depth, rows, MAIN_COLS), BF16),
                   jax.ShapeDtypeStruct((depth, rows, LANES), BF16),
                   jax.ShapeDtypeStruct((depth, rows, mg_cols), BF16)],
        name="prep_w_in",
    )(w_in)


def _stacked_weights(norm_g, w_in, w_alpha, b_alpha, gla_gain, w_a, pool_w, pool_scale, w_b,
                     b_merge, w_out, final_norm_g):
    w_main, w_al, w_mg = _prep_w_in(w_in)
    pad = LANES - GATE_RANK
    return {
        "norm_g": norm_g[:, None, :],
        "w_main": w_main, "w_al": w_al, "w_mg": w_mg,
        "w_alpha": jnp.pad(w_alpha.astype(BF16), ((0, 0), (0, pad), (0, 0))),
        "b_alpha": b_alpha[:, None, :],
        "gla_gain": gla_gain[:, None, :],
        "w_a": w_a.astype(BF16),
        "pool_w": pool_w.astype(BF16),
        "pool_scale": pool_scale[:, None, :],
        "w_b": w_b.astype(BF16),
        "b_merge": b_merge[:, None, :],
        "w_out": w_out.astype(BF16),
        "final_g": final_norm_g[None, :],
    }


def _window_row():
    return np.repeat(np.asarray(POOL_WINDOWS, np.float32), POOL_GW)[None, :]


def kernel(x_prompt, x_sample, state_gla, state_pool, meta_tokens, norm_g, w_in, w_alpha, b_alpha,
           gla_gain, w_a, pool_w, pool_scale, w_b, b_merge, w_out, final_norm_g):
    depth = w_in.shape[0]
    dec_b, dec_t, _ = x_sample.shape
    wrow = _window_row()
    cnt_meta = jnp.asarray(np.minimum(wrow, np.arange(1, N_META + 1, dtype=np.float32)[:, None]))
    inv_w = jnp.asarray(1.0 / wrow)

    ws = _stacked_weights(norm_g, w_in, w_alpha, b_alpha, gla_gain, w_a, pool_w, pool_scale, w_b,
                          b_merge, w_out, final_norm_g)
    h_meta = meta_tokens[None].astype(F32)
    h_p = x_prompt
    zero_s = jnp.zeros((1, HEADS, DKH, DVH), F32)
    zero_buf = jnp.zeros((1, POOL_PAD, POOL_W), F32)

    sg_p, sp_p = [], []
    for l in range(depth):
        last = l == depth - 1
        h_meta, s_meta, buf_meta = _seq_layer(h_meta, zero_s, zero_buf, cnt_meta, ws, l,
                                              is_last=False, pow2_counts=False)
        h_p, s_p, b_p = _seq_layer(h_p, s_meta, buf_meta, inv_w, ws, l, is_last=last,
                                   pow2_counts=True)
        sg_p.append(s_p)
        sp_p.append(b_p[:, POOL_PAD - POOL_BUF:, :])

    buf_s = jnp.pad(state_pool, ((0, 0), (0, 0), (POOL_PAD - POOL_BUF, 0), (0, 0)))
    y_s, sg_s, bufn_s = _step_layers(x_sample.reshape(dec_b * dec_t, D_MODEL), state_gla, buf_s,
                                     inv_w, ws, n_tok=dec_t)
    return (h_p, y_s.reshape(dec_b, dec_t, D_MODEL), jnp.stack(sg_p), jnp.stack(sp_p),
            sg_s, bufn_s[:, :, POOL_PAD - POOL_BUF:, :])
```

```python
import functools

import numpy as np
import jax
import jax.numpy as jnp
from jax import lax
from jax.experimental import pallas as pl
from jax.experimental.pallas import tpu as pltpu

F32 = jnp.float32
BF16 = jnp.bfloat16

D_MODEL = 1024
N_META = 16
HEADS = 4
DKH = 128
DVH = 256
DK = HEADS * DKH
DV = HEADS * DVH
GATE_RANK = 16
GATE_TAU = 16.0
POOL_W = 512
POOL_WINDOWS = (2, 4, 8, 16)
POOL_GW = POOL_W // len(POOL_WINDOWS)
POOL_BUF = max(POOL_WINDOWS) - 1
POOL_PAD = POOL_BUF + 1
PAST_LEN = 16384
EPS = 1e-6
IN_SIZES = (DK, DK, DV, DV, POOL_W, POOL_W, GATE_RANK, 2 * D_MODEL)
IN_OFFS = tuple(int(o) for o in np.cumsum((0,) + IN_SIZES))

LANES = 128
SUBLANES = 8
VMEM_LIMIT_BYTES = 56 * 1024 * 1024

MAX_FACTORED_LOG_DECAY = 60.0

SEQ_BLOCK = 512
SEQ_CHUNK = 256
SAMPLE_BLOCK = 32
SAMPLE_SUB = 8
PREP_COLS = 512


def _mm(a, b):
    return jnp.dot(a, b, preferred_element_type=F32)


def _mm_nt(a, b):
    return lax.dot_general(a, b, (((1,), (1,)), ((), ())), preferred_element_type=F32)


def _mm_tn(a, b):
    return lax.dot_general(a, b, (((0,), (0,)), ((), ())), preferred_element_type=F32)


def _sigmoid(x):
    return 1.0 / (1.0 + jnp.exp(-x))


def _silu(x):
    return x * _sigmoid(x)


def _log_sigmoid(x):
    return jnp.minimum(x, 0.0) - jnp.log1p(jnp.exp(-jnp.abs(x)))


def _rmsnorm(x, g):
    ms = jnp.mean(x * x, axis=-1, keepdims=True)
    return x * lax.rsqrt(ms + EPS) * g


def _row_to_col(row, eye):
    return jnp.sum(eye * row, axis=1, keepdims=True)


def _eye(n):
    r = lax.broadcasted_iota(jnp.int32, (n, n), 0)
    c = lax.broadcasted_iota(jnp.int32, (n, n), 1)
    return (r == c).astype(F32)


W_NAMES = ("norm_g", "w_main", "w_al", "w_mg", "w_alpha", "b_alpha", "gla_gain", "w_a", "pool_w",
           "pool_scale", "w_b", "b_merge", "w_out", "final_g")
MAIN_SEGS = ("w_q", "w_k", "w_v", "w_ga", "w_u", "w_gb")
MAIN_COLS = IN_OFFS[len(MAIN_SEGS)]


def _weight_views(refs):
    w = dict(zip(W_NAMES, refs))
    for i, name in enumerate(MAIN_SEGS):
        w[name] = w["w_main"].at[:, IN_OFFS[i]:IN_OFFS[i + 1]]
    return w


def _gate_in(x, w):
    xn = _rmsnorm(x, w["norm_g"][...]).astype(BF16)
    alow = _mm(xn, w["w_al"][...])
    q = _mm(xn, w["w_q"][...]) * (DKH ** -0.5)
    k = _mm(xn, w["w_k"][...])
    a = _mm(alow.astype(BF16), w["w_alpha"][...]) + w["b_alpha"][...]
    v = _mm(xn, w["w_v"][...])
    u = _mm(xn, w["w_u"][...])
    return xn, q, k, v, u, a


def _log_gate(a):
    return _log_sigmoid(a) * (1.0 / GATE_TAU)


def _dense_out(x, xn, o, z, w, is_last, pre=None):
    if pre is None:
        pre = (_mm(xn, w["w_ga"][...]), _mm(xn, w["w_gb"][...]), _mm(xn, w["w_mg"][...]))
    ga, gb, mg = pre
    gain = w["gla_gain"][...]
    parts = []
    for h in range(HEADS):
        oh = o[:, h * DVH:(h + 1) * DVH]
        ms = jnp.mean(oh * oh, axis=-1, keepdims=True)
        parts.append(oh * lax.rsqrt(ms + EPS) * gain[:, h * DVH:(h + 1) * DVH])
    on = jnp.concatenate(parts, axis=-1) * _silu(ga)
    ya = _mm(on.astype(BF16), w["w_a"][...])

    yp = []
    for g in range(len(POOL_WINDOWS)):
        zg = z[:, g * POOL_GW:(g + 1) * POOL_GW].astype(BF16)
        yp.append(_mm(zg, w["pool_w"][g]))
    y_pool = jnp.concatenate(yp, axis=-1) * w["pool_scale"][...]
    yb = _mm((y_pool * _silu(gb)).astype(BF16), w["w_b"][...])

    gates = _sigmoid(mg + w["b_merge"][...])
    merged = gates[:, :D_MODEL] * ya + gates[:, D_MODEL:] * yb
    h_new = x + _mm(merged.astype(BF16), w["w_out"][...])
    y = _rmsnorm(h_new, w["final_g"][...]) if is_last else None
    return h_new, y


def _window_sums(e):
    p1 = e + pltpu.roll(e, 1, 0)
    t = p1[:, POOL_GW:]
    p2 = t + pltpu.roll(t, 2, 0)
    t = p2[:, POOL_GW:]
    p3 = t + pltpu.roll(t, 4, 0)
    t = p3[:, POOL_GW:]
    p4 = t + pltpu.roll(t, 8, 0)
    return jnp.concatenate([p1[:, :POOL_GW], p2[:, :POOL_GW], p3[:, :POOL_GW], p4], axis=-1)


def _split_hi_lo(x):
    hi = x.astype(BF16)
    lo = (x - hi.astype(F32)).astype(BF16)
    return hi, lo


def _seq_kernel(*refs, tb, chunk, is_last, pow2_counts):
    n_w = len(W_NAMES)
    h_ref, s0_ref, buf0_ref, cnt_ref = refs[:4]
    w = _weight_views(refs[4:4 + n_w])
    hy_ref, s_ref, bufn_ref, o_s, ext_s, a_s, b_s, k_s = refs[4 + n_w:]

    t = pl.program_id(1)

    @pl.when(t == 0)
    def _():
        s_ref[0] = s0_ref[0]
        ext_s[0:POOL_PAD, :] = buf0_ref[0]

    x = h_ref[0]
    xn, q, k, v, u, a_gate = _gate_in(x, w)
    n_chunks = tb // chunk
    chunks = [(c, h) for c in range(n_chunks) for h in range(HEADS)]

    def sl(c, h):
        return (slice(c * chunk, (c + 1) * chunk), slice(h * DKH, (h + 1) * DKH),
                slice(h * DVH, (h + 1) * DVH))

    neg = jnp.minimum(a_gate, 0.0)
    steepest = jnp.sum(neg[0:chunk], axis=0, keepdims=True)
    for c in range(1, n_chunks):
        steepest = jnp.minimum(
            steepest, jnp.sum(neg[c * chunk:(c + 1) * chunk], axis=0, keepdims=True))
    decay_bound = (jnp.min(steepest) - chunk * np.log(2.0)) * (1.0 / GATE_TAU)
    factored_ok = decay_bound >= -MAX_FACTORED_LOG_DECAY
    la = _log_gate(a_gate)
    vb = v.astype(BF16)
    causal = (lax.broadcasted_iota(jnp.int32, (chunk, chunk), 0)
              >= lax.broadcasted_iota(jnp.int32, (chunk, chunk), 1))
    tril = causal.astype(BF16)
    eye = _eye(DKH)
    la_hi, la_lo = _split_hi_lo(la)
    b = jnp.concatenate(
        [_mm(tril, la_hi[c * chunk:(c + 1) * chunk])
         + _mm(tril, la_lo[c * chunk:(c + 1) * chunk]) for c in range(n_chunks)], axis=0)
    ga = _mm(xn, w["w_ga"][...])
    b_ends = [b[(c + 1) * chunk - 1:(c + 1) * chunk, :] for c in range(n_chunks)]
    to_end = jnp.concatenate(
        [b_ends[c] - b[c * chunk:(c + 1) * chunk, :] for c in range(n_chunks)], axis=0)
    qd = (q * jnp.exp(b)).astype(BF16)
    kd = (k * jnp.exp(to_end)).astype(BF16)

    d_s, dcol = {}, {}
    for c, h in chunks:
        rows, ks, vs = sl(c, h)
        d_s[c, h] = _mm_tn(kd[rows, ks], vb[rows, vs])
        dcol[c, h] = _row_to_col(jnp.exp(b_ends[c])[:, ks], eye)
    gb = _mm(xn, w["w_gb"][...])
    s_cur = [s_ref[0, h] for h in range(HEADS)]
    for c, h in chunks:
        rows, ks, vs = sl(c, h)
        o_s[rows, vs] = _mm(qd[rows, ks], s_cur[h].astype(BF16))
        s_cur[h] = dcol[c, h] * s_cur[h] + d_s[c, h]
    for h in range(HEADS):
        s_ref[0, h] = s_cur[h]

    @pl.when(factored_ok)
    def _():
        kinv = (k * jnp.exp(-b)).astype(BF16)
        for c, h in chunks:
            rows, ks, _ = sl(c, h)
            a = _mm_nt(qd[rows, ks], kinv[rows, ks])
            a_s[c, h] = jnp.where(causal, a, 0.0).astype(BF16)

    @pl.when(jnp.logical_not(factored_ok))
    def _():
        b_s[...] = b
        k_s[...] = k
        col_id = lax.broadcasted_iota(jnp.int32, (chunk, chunk), 1)
        for c, h in chunks:
            rows, ks, _ = sl(c, h)
            qh = q[rows, ks]
            bh = b[rows, ks]

            def columns(g, acc, c=c, ks=ks, qh=qh, bh=bh):
                r0 = pl.multiple_of(c * chunk + g * SUBLANES, SUBLANES)
                b_tile = b_s[pl.ds(r0, SUBLANES), ks]
                k_tile = k_s[pl.ds(r0, SUBLANES), ks]
                for r in range(SUBLANES):
                    decay = jnp.exp(jnp.minimum(bh - b_tile[r:r + 1, :], 0.0))
                    col = jnp.sum(qh * k_tile[r:r + 1, :] * decay, axis=1, keepdims=True)
                    acc = jnp.where(col_id == g * SUBLANES + r, col, acc)
                return acc

            a = lax.fori_loop(0, chunk // SUBLANES, columns, jnp.zeros((chunk, chunk), F32))
            a_s[c, h] = jnp.where(causal, a, 0.0).astype(BF16)

    mg = _mm(xn, w["w_mg"][...])
    for c, h in chunks:
        rows, ks, vs = sl(c, h)
        o_s[rows, vs] += _mm(a_s[c, h], vb[rows, vs])

    ext_s[POOL_PAD:POOL_PAD + tb, :] = u
    sums = _window_sums(ext_s[...])[POOL_PAD:, :]
    if pow2_counts:
        z = sums * cnt_ref[...] - u
    else:
        z = sums / cnt_ref[...] - u
    bufn_ref[0] = ext_s[tb:tb + POOL_PAD, :]
    ext_s[0:POOL_PAD, :] = ext_s[tb:tb + POOL_PAD, :]

    h_new, y = _dense_out(x, xn, o_s[...], z, w, is_last, pre=(ga, gb, mg))
    hy_ref[0] = y if is_last else h_new


def _const_spec(shape):
    nd = len(shape)
    return pl.BlockSpec(shape, lambda *_: (0,) * nd, pipeline_mode=pl.Buffered(1))


def _weight_specs(ws, layer_of):
    specs = []
    for name in W_NAMES:
        shape = ws[name].shape
        if name == "final_g":
            specs.append(_const_spec(shape))
        else:
            zeros = (0,) * (len(shape) - 1)
            specs.append(pl.BlockSpec((None,) + shape[1:],
                                      lambda *g, zeros=zeros: (layer_of(*g),) + zeros,
                                      pipeline_mode=pl.Buffered(1)))
    return specs


def _seq_layer(h, s0, buf0, cnt, ws, layer, *, is_last, pow2_counts):
    bn, t_len, _ = h.shape
    tb = min(SEQ_BLOCK, t_len)
    chunk = min(SEQ_CHUNK, tb)
    assert t_len % tb == 0 and tb % chunk == 0
    shared = s0.shape[0] == 1
    bsel = (lambda b, t: (0, 0, 0, 0)) if shared else (lambda b, t: (b, 0, 0, 0))
    bsel3 = (lambda b, t: (0, 0, 0)) if shared else (lambda b, t: (b, 0, 0))
    weights = [ws[n] for n in W_NAMES]
    in_specs = [
        pl.BlockSpec((1, tb, D_MODEL), lambda b, t: (b, t, 0)),
        pl.BlockSpec((1, HEADS, DKH, DVH), bsel),
        pl.BlockSpec((1, POOL_PAD, POOL_W), bsel3),
        _const_spec(cnt.shape),
    ] + _weight_specs(ws, lambda b, t: layer)
    out_shape = [jax.ShapeDtypeStruct(h.shape, F32),
                 jax.ShapeDtypeStruct((bn, HEADS, DKH, DVH), F32),
                 jax.ShapeDtypeStruct((bn, POOL_PAD, POOL_W), F32)]
    out_specs = [pl.BlockSpec((1, tb, D_MODEL), lambda b, t: (b, t, 0)),
                 pl.BlockSpec((1, HEADS, DKH, DVH), lambda b, t: (b, 0, 0, 0)),
                 pl.BlockSpec((1, POOL_PAD, POOL_W), lambda b, t: (b, 0, 0))]
    kern = functools.partial(_seq_kernel, tb=tb, chunk=chunk, is_last=is_last,
                             pow2_counts=pow2_counts)
    return pl.pallas_call(
        kern,
        grid=(bn, t_len // tb),
        in_specs=in_specs,
        out_specs=out_specs,
        out_shape=out_shape,
        scratch_shapes=[pltpu.VMEM((tb, DV), F32), pltpu.VMEM((tb + POOL_PAD, POOL_W), F32),
                        pltpu.VMEM((tb // chunk, HEADS, chunk, chunk), BF16),
                        pltpu.VMEM((tb, DK), F32), pltpu.VMEM((tb, DK), F32)],
        compiler_params=pltpu.CompilerParams(
            dimension_semantics=("arbitrary", "arbitrary"), vmem_limit_bytes=VMEM_LIMIT_BYTES),
        name="seq_layer",
    )(h, s0, buf0, cnt, *weights)


def _step_kernel(*refs, n_tok, sub, depth):
    n_w = len(W_NAMES)
    x_ref, s0_ref, buf0_ref, cnt_ref = refs[:4]
    w = _weight_views(refs[4:4 + n_w])
    y_ref, s_ref, bufn_ref = refs[4 + n_w:4 + n_w + 3]
    h_s, xin_s, xn_s, q_s, k_s, v_s, la_s, u_s, o_s, z_s = refs[4 + n_w + 3:]

    layer = pl.program_id(0)
    blk = pl.program_id(1)
    j = pl.program_id(2)
    nj = pl.num_programs(2)
    brows = xin_s.shape[0]
    blk_rows = pl.ds(pl.multiple_of(blk * brows, brows), brows)
    pair = SUBLANES // n_tok
    assert pair == 2 and sub % pair == 0

    @pl.when(j == 0)
    def _():
        @pl.when(layer == 0)
        def _():
            xin_s[...] = x_ref[...]

        @pl.when(layer > 0)
        def _():
            xin_s[...] = h_s[blk_rows, :]

        xn, q, k, v, u, a_gate = _gate_in(xin_s[...], w)
        la = _log_gate(a_gate)
        xn_s[...] = xn
        q_s[...] = q
        k_s[...] = k
        v_s[...] = v
        la_s[...] = la
        u_s[...] = u

    nr = sub * n_tok
    rows = pl.ds(pl.multiple_of(j * nr, nr), nr)
    row = lax.broadcasted_iota(jnp.int32, (nr, 1), 0)
    tok = row % n_tok
    seq = row // n_tok
    eye = _eye(DKH)
    b = la_s[rows, :]
    step = 1
    while step < n_tok:
        b = b + jnp.where(tok >= step, pltpu.roll(b, step, 0), 0.0)
        step *= 2
    b_end = b
    for d in range(1, n_tok):
        b_end = jnp.where(tok == n_tok - 1 - d, pltpu.roll(b, nr - d, 0), b_end)
    q = q_s[rows, :]
    k = k_s[rows, :]
    v = v_s[rows, :]
    qd = (q * jnp.exp(b)).astype(BF16)
    kd = k * jnp.exp(b_end - b)
    dec = jnp.exp(b_end)
    vb = v.astype(BF16)
    o_parts = []
    for h in range(HEADS):
        ks = slice(h * DKH, (h + 1) * DKH)
        vs = slice(h * DVH, (h + 1) * DVH)
        o = jnp.sum(q[:, ks] * k[:, ks], axis=1, keepdims=True) * v[:, vs]
        for d in range(1, n_tok):
            kj = pltpu.roll(k[:, ks], d, 0)
            bj = pltpu.roll(b[:, ks], d, 0)
            score = jnp.sum(q[:, ks] * kj * jnp.exp(jnp.minimum(b[:, ks] - bj, 0.0)), axis=1,
                            keepdims=True)
            o = o + jnp.where(tok >= d, score, 0.0) * pltpu.roll(v[:, vs], d, 0)
        o_parts.append(o)
    for h in range(HEADS):
        ks = slice(h * DKH, (h + 1) * DKH)
        vs = slice(h * DVH, (h + 1) * DVH)
        o = o_parts[h]
        for s in range(sub):
            mine = seq == s
            s_old = s0_ref[s, h]
            o = o + jnp.where(mine, _mm(qd[:, ks], s_old.astype(BF16)), 0.0)
            kd_s = jnp.where(mine, kd[:, ks], 0.0).astype(BF16)
            dcol = _row_to_col(dec[s * n_tok:s * n_tok + 1, ks], eye)
            s_ref[s, h] = dcol * s_old + _mm_tn(kd_s, vb[:, vs])
        o_parts[h] = o
    o_s[rows, :] = jnp.concatenate(o_parts, axis=-1)

    first = lax.broadcasted_iota(jnp.int32, (SUBLANES, 1), 0) < n_tok
    inv_cnt = cnt_ref[...]
    for p in range(sub // pair):
        r0 = pl.multiple_of(j * nr + p * SUBLANES, SUBLANES)
        rows = pl.ds(r0, SUBLANES)
        u = u_s[rows, :]
        e0 = jnp.concatenate([buf0_ref[pair * p], u], axis=0)
        e1 = jnp.concatenate([buf0_ref[pair * p + 1], pltpu.roll(u, n_tok, 0)], axis=0)
        n_ext = POOL_PAD + SUBLANES
        w0 = _window_sums(e0)[POOL_PAD:, :]
        w1 = _window_sums(e1)[POOL_PAD:, :]
        sums = jnp.where(first, w0, pltpu.roll(w1, n_tok, 0))
        z_s[rows, :] = sums * inv_cnt - u
        drop = n_tok + POOL_PAD - POOL_BUF
        bufn_ref[pair * p] = pltpu.roll(e0, n_ext - drop, 0)[0:POOL_BUF, :]
        bufn_ref[pair * p + 1] = pltpu.roll(e1, n_ext - drop, 0)[0:POOL_BUF, :]

    @pl.when(j == nj - 1)
    def _():
        h_new, y = _dense_out(xin_s[...], xn_s[...], o_s[...], z_s[...], w, True)

        @pl.when(layer < depth - 1)
        def _():
            h_s[blk_rows, :] = h_new

        @pl.when(layer == depth - 1)
        def _():
            y_ref[...] = y


def _step_layers(x, s_all, buf_all, cnt, ws, *, n_tok):
    rows, _ = x.shape
    depth, bn = s_all.shape[:2]
    sub, blk = SAMPLE_SUB, SAMPLE_BLOCK
    assert rows == bn * n_tok and bn % blk == 0 and blk % sub == 0
    n_blk, n_inner = bn // blk, blk // sub
    brows = blk * n_tok
    weights = [ws[n] for n in W_NAMES]

    def state_idx(l, i, j):
        return (l, i * n_inner + j, 0, 0, 0)

    def buf_idx(l, i, j):
        return (l, i * n_inner + j, 0, 0)

    in_specs = [
        pl.BlockSpec((brows, D_MODEL), lambda l, i, j: (i, 0)),
        pl.BlockSpec((None, sub, HEADS, DKH, DVH), state_idx),
        pl.BlockSpec((None, sub, POOL_PAD, POOL_W), buf_idx),
        _const_spec(cnt.shape),
    ] + _weight_specs(ws, lambda l, i, j: l)
    out_shape = [jax.ShapeDtypeStruct(x.shape, F32),
                 jax.ShapeDtypeStruct(s_all.shape, F32),
                 jax.ShapeDtypeStruct((depth, bn, POOL_BUF, POOL_W), F32)]
    out_specs = [pl.BlockSpec((brows, D_MODEL),
                              lambda l, i, j: (jnp.where(l == depth - 1, i, 0), 0)),
                 pl.BlockSpec((None, sub, HEADS, DKH, DVH), state_idx),
                 pl.BlockSpec((None, sub, POOL_BUF, POOL_W), buf_idx)]
    kern = functools.partial(_step_kernel, n_tok=n_tok, sub=sub, depth=depth)
    scratch = [pltpu.VMEM((rows, D_MODEL), F32), pltpu.VMEM((brows, D_MODEL), F32),
               pltpu.VMEM((brows, D_MODEL), BF16), pltpu.VMEM((brows, DK), F32),
               pltpu.VMEM((brows, DK), F32), pltpu.VMEM((brows, DV), F32),
               pltpu.VMEM((brows, DK), F32), pltpu.VMEM((brows, POOL_W), F32),
               pltpu.VMEM((brows, DV), F32), pltpu.VMEM((brows, POOL_W), F32)]
    return pl.pallas_call(
        kern,
        grid=(depth, n_blk, n_inner),
        in_specs=in_specs,
        out_specs=out_specs,
        out_shape=out_shape,
        scratch_shapes=scratch,
        compiler_params=pltpu.CompilerParams(
            dimension_semantics=("arbitrary", "arbitrary", "arbitrary"),
            vmem_limit_bytes=VMEM_LIMIT_BYTES),
        name="step_layers",
    )(x, s_all, buf_all, cnt, *weights)


def _prep_main_kernel(wt_ref, main_ref):
    main_ref[...] = wt_ref[...].T.astype(BF16)


def _prep_tail_kernel(wt_ref, al_ref, mg_ref):
    lane = lax.broadcasted_iota(jnp.int32, al_ref.shape, 1)
    al_ref[...] = jnp.where(lane < GATE_RANK, wt_ref[0:LANES, :].T, 0.0).astype(BF16)
    mg_ref[...] = wt_ref[GATE_RANK:, :].T.astype(BF16)


def _prep_w_in(w_in):
    depth, rows, cols = w_in.shape
    n_tail = cols - MAIN_COLS
    wt = jnp.swapaxes(w_in, 1, 2)
    cb = PREP_COLS
    assert MAIN_COLS % cb == 0
    w_main = pl.pallas_call(
        _prep_main_kernel,
        grid=(depth, MAIN_COLS // cb),
        in_specs=[pl.BlockSpec((None, cb, rows), lambda l, i: (l, i, 0))],
        out_specs=pl.BlockSpec((None, rows, cb), lambda l, i: (l, 0, i)),
        out_shape=jax.ShapeDtypeStruct((depth, rows, MAIN_COLS), BF16),
        name="prep_w_main",
    )(wt)
    w_al, w_mg = pl.pallas_call(
        _prep_tail_kernel,
        grid=(depth,),
        in_specs=[pl.BlockSpec((None, n_tail, rows), lambda l: (l, 0, 0))],
        out_specs=[pl.BlockSpec((None, rows, LANES), lambda l: (l, 0, 0)),
                   pl.BlockSpec((None, rows, n_tail - GATE_RANK), lambda l: (l, 0, 0))],
        out_shape=[jax.ShapeDtypeStruct((depth, rows, LANES), BF16),
                   jax.ShapeDtypeStruct((depth, rows, n_tail - GATE_RANK), BF16)],
        compiler_params=pltpu.CompilerParams(vmem_limit_bytes=VMEM_LIMIT_BYTES),
        name="prep_w_tail",
    )(wt[:, MAIN_COLS:, :])
    return w_main, w_al, w_mg


def _stacked_weights(norm_g, w_in, w_alpha, b_alpha, gla_gain, w_a, pool_w, pool_scale, w_b,
                     b_merge, w_out, final_norm_g):
    w_main, w_al, w_mg = _prep_w_in(w_in)
    pad = LANES - GATE_RANK
    return {
        "norm_g": norm_g[:, None, :],
        "w_main": w_main, "w_al": w_al, "w_mg": w_mg,
        "w_alpha": jnp.pad(w_alpha.astype(BF16), ((0, 0), (0, pad), (0, 0))),
        "b_alpha": b_alpha[:, None, :],
        "gla_gain": gla_gain[:, None, :],
        "w_a": w_a.astype(BF16),
        "pool_w": pool_w.astype(BF16),
        "pool_scale": pool_scale[:, None, :],
        "w_b": w_b.astype(BF16),
        "b_merge": b_merge[:, None, :],
        "w_out": w_out.astype(BF16),
        "final_g": final_norm_g[None, :],
    }


def _window_row():
    return np.repeat(np.asarray(POOL_WINDOWS, np.float32), POOL_GW)[None, :]


def kernel(x_prompt, x_sample, state_gla, state_pool, meta_tokens, norm_g, w_in, w_alpha, b_alpha,
           gla_gain, w_a, pool_w, pool_scale, w_b, b_merge, w_out, final_norm_g):
    depth = w_in.shape[0]
    dec_b, dec_t, _ = x_sample.shape
    wrow = _window_row()
    cnt_meta = jnp.asarray(np.minimum(wrow, np.arange(1, N_META + 1, dtype=np.float32)[:, None]))
    inv_w = jnp.asarray(1.0 / wrow)

    ws = _stacked_weights(norm_g, w_in, w_alpha, b_alpha, gla_gain, w_a, pool_w, pool_scale, w_b,
                          b_merge, w_out, final_norm_g)
    h_meta = meta_tokens[None].astype(F32)
    h_p = x_prompt
    zero_s = jnp.zeros((1, HEADS, DKH, DVH), F32)
    zero_buf = jnp.zeros((1, POOL_PAD, POOL_W), F32)

    sg_p, sp_p = [], []
    for l in range(depth):
        last = l == depth - 1
        h_meta, s_meta, buf_meta = _seq_layer(h_meta, zero_s, zero_buf, cnt_meta, ws, l,
                                              is_last=False, pow2_counts=False)
        h_p, s_p, b_p = _seq_layer(h_p, s_meta, buf_meta, inv_w, ws, l, is_last=last,
                                   pow2_counts=True)
        sg_p.append(s_p)
        sp_p.append(b_p[:, POOL_PAD - POOL_BUF:, :])

    buf_s = jnp.pad(state_pool, ((0, 0), (0, 0), (POOL_PAD - POOL_BUF, 0), (0, 0)))
    y_s, sg_s, bufn_s = _step_layers(x_sample.reshape(dec_b * dec_t, D_MODEL), state_gla, buf_s,
                                     inv_w, ws, n_tok=dec_t)
    return (h_p, y_s.reshape(dec_b, dec_t, D_MODEL), jnp.stack(sg_p), jnp.stack(sp_p),
            sg_s, bufn_s)
```

```python
import functools

import numpy as np
import jax
import jax.numpy as jnp
from jax import lax
from jax.experimental import pallas as pl
from jax.experimental.pallas import tpu as pltpu

F32 = jnp.float32
BF16 = jnp.bfloat16

D_MODEL = 1024
N_META = 16
HEADS = 4
DKH = 128
DVH = 256
DK = HEADS * DKH
DV = HEADS * DVH
GATE_RANK = 16
GATE_TAU = 16.0
POOL_W = 512
POOL_WINDOWS = (2, 4, 8, 16)
POOL_GW = POOL_W // len(POOL_WINDOWS)
POOL_BUF = max(POOL_WINDOWS) - 1
POOL_PAD = POOL_BUF + 1
PAST_LEN = 16384
EPS = 1e-6
IN_SIZES = (DK, DK, DV, DV, POOL_W, POOL_W, GATE_RANK, 2 * D_MODEL)
IN_OFFS = tuple(int(o) for o in np.cumsum((0,) + IN_SIZES))

LANES = 128
SUBLANES = 8
VMEM_LIMIT_BYTES = 56 * 1024 * 1024

MAX_FACTORED_LOG_DECAY = 60.0

SEQ_BLOCK = 512
SEQ_CHUNK = 256
SAMPLE_BLOCK = 32
SAMPLE_SUB = 8
PREP_COLS = 512


def _mm(a, b):
    return jnp.dot(a, b, preferred_element_type=F32)


def _mm_nt(a, b):
    return lax.dot_general(a, b, (((1,), (1,)), ((), ())), preferred_element_type=F32)


def _mm_tn(a, b):
    return lax.dot_general(a, b, (((0,), (0,)), ((), ())), preferred_element_type=F32)


def _sigmoid(x):
    return 1.0 / (1.0 + jnp.exp(-x))


def _silu(x):
    return x * _sigmoid(x)


def _log_sigmoid(x):
    return jnp.minimum(x, 0.0) - jnp.log1p(jnp.exp(-jnp.abs(x)))


def _rmsnorm(x, g):
    ms = jnp.mean(x * x, axis=-1, keepdims=True)
    return x * lax.rsqrt(ms + EPS) * g


def _row_to_col(row, eye):
    return jnp.sum(eye * row, axis=1, keepdims=True)


def _eye(n):
    r = lax.broadcasted_iota(jnp.int32, (n, n), 0)
    c = lax.broadcasted_iota(jnp.int32, (n, n), 1)
    return (r == c).astype(F32)


W_NAMES = ("norm_g", "w_main", "w_al", "w_mg", "w_alpha", "b_alpha", "gla_gain", "w_a", "pool_w",
           "pool_scale", "w_b", "b_merge", "w_out", "final_g")
MAIN_SEGS = ("w_q", "w_k", "w_v", "w_ga", "w_u", "w_gb")
MAIN_COLS = IN_OFFS[len(MAIN_SEGS)]


def _weight_views(refs):
    w = dict(zip(W_NAMES, refs))
    for i, name in enumerate(MAIN_SEGS):
        w[name] = w["w_main"].at[:, IN_OFFS[i]:IN_OFFS[i + 1]]
    return w


def _gate_in(x, w):
    xn = _rmsnorm(x, w["norm_g"][...]).astype(BF16)
    alow = _mm(xn, w["w_al"][...])
    q = _mm(xn, w["w_q"][...]) * (DKH ** -0.5)
    k = _mm(xn, w["w_k"][...])
    a = _mm(alow.astype(BF16), w["w_alpha"][...]) + w["b_alpha"][...]
    v = _mm(xn, w["w_v"][...])
    u = _mm(xn, w["w_u"][...])
    return xn, q, k, v, u, a


def _log_gate(a):
    return _log_sigmoid(a) * (1.0 / GATE_TAU)


def _dense_out(x, xn, o, z, w, is_last, pre=None):
    if pre is None:
        pre = (_mm(xn, w["w_ga"][...]), _mm(xn, w["w_gb"][...]), _mm(xn, w["w_mg"][...]))
    ga, gb, mg = pre
    gain = w["gla_gain"][...]
    parts = []
    for h in range(HEADS):
        oh = o[:, h * DVH:(h + 1) * DVH]
        ms = jnp.mean(oh * oh, axis=-1, keepdims=True)
        parts.append(oh * lax.rsqrt(ms + EPS) * gain[:, h * DVH:(h + 1) * DVH])
    on = jnp.concatenate(parts, axis=-1) * _silu(ga)
    ya = _mm(on.astype(BF16), w["w_a"][...])

    yp = []
    for g in range(len(POOL_WINDOWS)):
        zg = z[:, g * POOL_GW:(g + 1) * POOL_GW].astype(BF16)
        yp.append(_mm(zg, w["pool_w"][g]))
    y_pool = jnp.concatenate(yp, axis=-1) * w["pool_scale"][...]
    yb = _mm((y_pool * _silu(gb)).astype(BF16), w["w_b"][...])

    gates = _sigmoid(mg + w["b_merge"][...])
    merged = gates[:, :D_MODEL] * ya + gates[:, D_MODEL:] * yb
    h_new = x + _mm(merged.astype(BF16), w["w_out"][...])
    y = _rmsnorm(h_new, w["final_g"][...]) if is_last else None
    return h_new, y


def _window_sums(e):
    p1 = e + pltpu.roll(e, 1, 0)
    t = p1[:, POOL_GW:]
    p2 = t + pltpu.roll(t, 2, 0)
    t = p2[:, POOL_GW:]
    p3 = t + pltpu.roll(t, 4, 0)
    t = p3[:, POOL_GW:]
    p4 = t + pltpu.roll(t, 8, 0)
    return jnp.concatenate([p1[:, :POOL_GW], p2[:, :POOL_GW], p3[:, :POOL_GW], p4], axis=-1)


def _split_hi_lo(x):
    hi = x.astype(BF16)
    lo = (x - hi.astype(F32)).astype(BF16)
    return hi, lo


def _seq_kernel(*refs, tb, chunk, is_last, pow2_counts):
    n_w = len(W_NAMES)
    h_ref, s0_ref, buf0_ref, cnt_ref = refs[:4]
    w = _weight_views(refs[4:4 + n_w])
    hy_ref, s_ref, bufn_ref, o_s, ext_s, oc_s, b_s, k_s = refs[4 + n_w:]

    t = pl.program_id(1)

    @pl.when(t == 0)
    def _():
        s_ref[0] = s0_ref[0]
        ext_s[0:POOL_PAD, :] = buf0_ref[0]

    x = h_ref[0]
    xn, q, k, v, u, a_gate = _gate_in(x, w)
    n_chunks = tb // chunk
    chunks = [(c, h) for c in range(n_chunks) for h in range(HEADS)]

    def sl(c, h):
        return (slice(c * chunk, (c + 1) * chunk), slice(h * DKH, (h + 1) * DKH),
                slice(h * DVH, (h + 1) * DVH))

    neg = jnp.minimum(a_gate, 0.0)
    steepest = jnp.sum(neg[0:chunk], axis=0, keepdims=True)
    for c in range(1, n_chunks):
        steepest = jnp.minimum(
            steepest, jnp.sum(neg[c * chunk:(c + 1) * chunk], axis=0, keepdims=True))
    decay_bound = (jnp.min(steepest) - chunk * np.log(2.0)) * (1.0 / GATE_TAU)
    factored_ok = decay_bound >= -MAX_FACTORED_LOG_DECAY
    la = _log_gate(a_gate)
    vb = v.astype(BF16)
    causal = (lax.broadcasted_iota(jnp.int32, (chunk, chunk), 0)
              >= lax.broadcasted_iota(jnp.int32, (chunk, chunk), 1))
    tril = causal.astype(BF16)
    eye = _eye(DKH)
    la_hi, la_lo = _split_hi_lo(la)
    b = jnp.concatenate(
        [_mm(tril, la_hi[c * chunk:(c + 1) * chunk])
         + _mm(tril, la_lo[c * chunk:(c + 1) * chunk]) for c in range(n_chunks)], axis=0)
    ga = _mm(xn, w["w_ga"][...])
    b_ends = [b[(c + 1) * chunk - 1:(c + 1) * chunk, :] for c in range(n_chunks)]
    to_end = jnp.concatenate(
        [b_ends[c] - b[c * chunk:(c + 1) * chunk, :] for c in range(n_chunks)], axis=0)
    qd = (q * jnp.exp(b)).astype(BF16)
    kd = (k * jnp.exp(to_end)).astype(BF16)

    kinv = (k * jnp.exp(-b)).astype(BF16)
    a_m = {}
    for c, h in chunks:
        rows, ks, _ = sl(c, h)
        a = _mm_nt(qd[rows, ks], kinv[rows, ks])
        a_m[c, h] = jnp.where(causal, a, 0.0).astype(BF16)
    gb = _mm(xn, w["w_gb"][...])
    o_in, d_s, dcol = {}, {}, {}
    for c, h in chunks:
        rows, ks, vs = sl(c, h)
        o_in[c, h] = _mm(a_m[c, h], vb[rows, vs])
        d_s[c, h] = _mm_tn(kd[rows, ks], vb[rows, vs])
        dcol[c, h] = _row_to_col(jnp.exp(b_ends[c])[:, ks], eye)
    mg = _mm(xn, w["w_mg"][...])

    s_cur = [s_ref[0, h] for h in range(HEADS)]
    for c, h in chunks:
        rows, ks, vs = sl(c, h)
        o_state = _mm(qd[rows, ks], s_cur[h].astype(BF16))
        oc_s[rows, vs] = o_state
        o_s[rows, vs] = o_in[c, h] + o_state
        s_cur[h] = dcol[c, h] * s_cur[h] + d_s[c, h]
    for h in range(HEADS):
        s_ref[0, h] = s_cur[h]

    ext_s[POOL_PAD:POOL_PAD + tb, :] = u
    sums = _window_sums(ext_s[...])[POOL_PAD:, :]
    if pow2_counts:
        z = sums * cnt_ref[...] - u
    else:
        z = sums / cnt_ref[...] - u
    bufn_ref[0] = ext_s[tb:tb + POOL_PAD, :]
    ext_s[0:POOL_PAD, :] = ext_s[tb:tb + POOL_PAD, :]

    h_new, y = _dense_out(x, xn, o_s[...], z, w, is_last, pre=(ga, gb, mg))
    hy_ref[0] = y if is_last else h_new

    @pl.when(jnp.logical_not(factored_ok))
    def _():
        b_s[...] = b
        k_s[...] = k
        col_id = lax.broadcasted_iota(jnp.int32, (chunk, chunk), 1)
        for c, h in chunks:
            rows, ks, vs = sl(c, h)
            qh = q[rows, ks]
            bh = b[rows, ks]

            def columns(g, acc, c=c, ks=ks, qh=qh, bh=bh):
                r0 = pl.multiple_of(c * chunk + g * SUBLANES, SUBLANES)
                b_tile = b_s[pl.ds(r0, SUBLANES), ks]
                k_tile = k_s[pl.ds(r0, SUBLANES), ks]
                for r in range(SUBLANES):
                    decay = jnp.exp(jnp.minimum(bh - b_tile[r:r + 1, :], 0.0))
                    col = jnp.sum(qh * k_tile[r:r + 1, :] * decay, axis=1, keepdims=True)
                    acc = jnp.where(col_id == g * SUBLANES + r, col, acc)
                return acc

            a = lax.fori_loop(0, chunk // SUBLANES, columns, jnp.zeros((chunk, chunk), F32))
            a = jnp.where(causal, a, 0.0).astype(BF16)
            o_s[rows, vs] = oc_s[rows, vs] + _mm(a, vb[rows, vs])
        h_fix, y_fix = _dense_out(x, xn, o_s[...], z, w, is_last, pre=(ga, gb, mg))
        hy_ref[0] = y_fix if is_last else h_fix


def _const_spec(shape):
    nd = len(shape)
    return pl.BlockSpec(shape, lambda *_: (0,) * nd, pipeline_mode=pl.Buffered(1))


def _weight_specs(ws, layer_of):
    specs = []
    for name in W_NAMES:
        shape = ws[name].shape
        if name == "final_g":
            specs.append(_const_spec(shape))
        else:
            zeros = (0,) * (len(shape) - 1)
            specs.append(pl.BlockSpec((None,) + shape[1:],
                                      lambda *g, zeros=zeros: (layer_of(*g),) + zeros,
                                      pipeline_mode=pl.Buffered(1)))
    return specs


def _seq_layer(h, s0, buf0, cnt, ws, layer, *, is_last, pow2_counts):
    bn, t_len, _ = h.shape
    tb = min(SEQ_BLOCK, t_len)
    chunk = min(SEQ_CHUNK, tb)
    assert t_len % tb == 0 and tb % chunk == 0
    shared = s0.shape[0] == 1
    bsel = (lambda b, t: (0, 0, 0, 0)) if shared else (lambda b, t: (b, 0, 0, 0))
    bsel3 = (lambda b, t: (0, 0, 0)) if shared else (lambda b, t: (b, 0, 0))
    weights = [ws[n] for n in W_NAMES]
    in_specs = [
        pl.BlockSpec((1, tb, D_MODEL), lambda b, t: (b, t, 0)),
        pl.BlockSpec((1, HEADS, DKH, DVH), bsel),
        pl.BlockSpec((1, POOL_PAD, POOL_W), bsel3),
        _const_spec(cnt.shape),
    ] + _weight_specs(ws, lambda b, t: layer)
    out_shape = [jax.ShapeDtypeStruct(h.shape, F32),
                 jax.ShapeDtypeStruct((bn, HEADS, DKH, DVH), F32),
                 jax.ShapeDtypeStruct((bn, POOL_PAD, POOL_W), F32)]
    out_specs = [pl.BlockSpec((1, tb, D_MODEL), lambda b, t: (b, t, 0)),
                 pl.BlockSpec((1, HEADS, DKH, DVH), lambda b, t: (b, 0, 0, 0)),
                 pl.BlockSpec((1, POOL_PAD, POOL_W), lambda b, t: (b, 0, 0))]
    kern = functools.partial(_seq_kernel, tb=tb, chunk=chunk, is_last=is_last,
                             pow2_counts=pow2_counts)
    return pl.pallas_call(
        kern,
        grid=(bn, t_len // tb),
        in_specs=in_specs,
        out_specs=out_specs,
        out_shape=out_shape,
        scratch_shapes=[pltpu.VMEM((tb, DV), F32), pltpu.VMEM((tb + POOL_PAD, POOL_W), F32),
                        pltpu.VMEM((tb, DV), F32),
                        pltpu.VMEM((tb, DK), F32), pltpu.VMEM((tb, DK), F32)],
        compiler_params=pltpu.CompilerParams(
            dimension_semantics=("arbitrary", "arbitrary"), vmem_limit_bytes=VMEM_LIMIT_BYTES),
        name="seq_layer",
    )(h, s0, buf0, cnt, *weights)


def _step_kernel(*refs, n_tok, sub, depth):
    n_w = len(W_NAMES)
    x_ref, s0_ref, buf0_ref, cnt_ref = refs[:4]
    w = _weight_views(refs[4:4 + n_w])
    y_ref, s_ref, bufn_ref = refs[4 + n_w:4 + n_w + 3]
    h_s, xin_s, xn_s, q_s, k_s, v_s, la_s, u_s, o_s, z_s = refs[4 + n_w + 3:]

    layer = pl.program_id(0)
    blk = pl.program_id(1)
    j = pl.program_id(2)
    nj = pl.num_programs(2)
    brows = xin_s.shape[0]
    blk_rows = pl.ds(pl.multiple_of(blk * brows, brows), brows)
    pair = SUBLANES // n_tok
    assert pair == 2 and sub % pair == 0

    @pl.when(j == 0)
    def _():
        @pl.when(layer == 0)
        def _():
            xin_s[...] = x_ref[...]

        @pl.when(layer > 0)
        def _():
            xin_s[...] = h_s[blk_rows, :]

        xn, q, k, v, u, a_gate = _gate_in(xin_s[...], w)
        la = _log_gate(a_gate)
        xn_s[...] = xn
        q_s[...] = q
        k_s[...] = k
        v_s[...] = v
        la_s[...] = la
        u_s[...] = u

    nr = sub * n_tok
    rows = pl.ds(pl.multiple_of(j * nr, nr), nr)
    row = lax.broadcasted_iota(jnp.int32, (nr, 1), 0)
    tok = row % n_tok
    seq = row // n_tok
    eye = _eye(DKH)
    b = la_s[rows, :]
    step = 1
    while step < n_tok:
        b = b + jnp.where(tok >= step, pltpu.roll(b, step, 0), 0.0)
        step *= 2
    b_end = b
    for d in range(1, n_tok):
        b_end = jnp.where(tok == n_tok - 1 - d, pltpu.roll(b, nr - d, 0), b_end)
    q = q_s[rows, :]
    k = k_s[rows, :]
    v = v_s[rows, :]
    qd = (q * jnp.exp(b)).astype(BF16)
    kd = k * jnp.exp(b_end - b)
    dec = jnp.exp(b_end)
    vb = v.astype(BF16)
    o_parts = []
    for h in range(HEADS):
        ks = slice(h * DKH, (h + 1) * DKH)
        vs = slice(h * DVH, (h + 1) * DVH)
        o = jnp.sum(q[:, ks] * k[:, ks], axis=1, keepdims=True) * v[:, vs]
        for d in range(1, n_tok):
            kj = pltpu.roll(k[:, ks], d, 0)
            bj = pltpu.roll(b[:, ks], d, 0)
            score = jnp.sum(q[:, ks] * kj * jnp.exp(jnp.minimum(b[:, ks] - bj, 0.0)), axis=1,
                            keepdims=True)
            o = o + jnp.where(tok >= d, score, 0.0) * pltpu.roll(v[:, vs], d, 0)
        o_parts.append(o)
    for h in range(HEADS):
        ks = slice(h * DKH, (h + 1) * DKH)
        vs = slice(h * DVH, (h + 1) * DVH)
        o = o_parts[h]
        for s in range(sub):
            mine = seq == s
            s_old = s0_ref[s, h]
            o = o + jnp.where(mine, _mm(qd[:, ks], s_old.astype(BF16)), 0.0)
            kd_s = jnp.where(mine, kd[:, ks], 0.0).astype(BF16)
            dcol = _row_to_col(dec[s * n_tok:s * n_tok + 1, ks], eye)
            s_ref[s, h] = dcol * s_old + _mm_tn(kd_s, vb[:, vs])
        o_parts[h] = o
    o_s[rows, :] = jnp.concatenate(o_parts, axis=-1)

    first = lax.broadcasted_iota(jnp.int32, (SUBLANES, 1), 0) < n_tok
    inv_cnt = cnt_ref[...]
    for p in range(sub // pair):
        r0 = pl.multiple_of(j * nr + p * SUBLANES, SUBLANES)
        rows = pl.ds(r0, SUBLANES)
        u = u_s[rows, :]
        e0 = jnp.concatenate([buf0_ref[pair * p], u], axis=0)
        e1 = jnp.concatenate([buf0_ref[pair * p + 1], pltpu.roll(u, n_tok, 0)], axis=0)
        n_ext = POOL_PAD + SUBLANES
        w0 = _window_sums(e0)[POOL_PAD:, :]
        w1 = _window_sums(e1)[POOL_PAD:, :]
        sums = jnp.where(first, w0, pltpu.roll(w1, n_tok, 0))
        z_s[rows, :] = sums * inv_cnt - u
        drop = n_tok + POOL_PAD - POOL_BUF
        bufn_ref[pair * p] = pltpu.roll(e0, n_ext - drop, 0)[0:POOL_BUF, :]
        bufn_ref[pair * p + 1] = pltpu.roll(e1, n_ext - drop, 0)[0:POOL_BUF, :]

    @pl.when(j == nj - 1)
    def _():
        h_new, y = _dense_out(xin_s[...], xn_s[...], o_s[...], z_s[...], w, True)

        @pl.when(layer < depth - 1)
        def _():
            h_s[blk_rows, :] = h_new

        @pl.when(layer == depth - 1)
        def _():
            y_ref[...] = y


def _step_layers(x, s_all, buf_all, cnt, ws, *, n_tok):
    rows, _ = x.shape
    depth, bn = s_all.shape[:2]
    sub, blk = SAMPLE_SUB, SAMPLE_BLOCK
    assert rows == bn * n_tok and bn % blk == 0 and blk % sub == 0
    n_blk, n_inner = bn // blk, blk // sub
    brows = blk * n_tok
    weights = [ws[n] for n in W_NAMES]

    def state_idx(l, i, j):
        return (l, i * n_inner + j, 0, 0, 0)

    def buf_idx(l, i, j):
        return (l, i * n_inner + j, 0, 0)

    in_specs = [
        pl.BlockSpec((brows, D_MODEL), lambda l, i, j: (i, 0)),
        pl.BlockSpec((None, sub, HEADS, DKH, DVH), state_idx),
        pl.BlockSpec((None, sub, POOL_PAD, POOL_W), buf_idx),
        _const_spec(cnt.shape),
    ] + _weight_specs(ws, lambda l, i, j: l)
    out_shape = [jax.ShapeDtypeStruct(x.shape, F32),
                 jax.ShapeDtypeStruct(s_all.shape, F32),
                 jax.ShapeDtypeStruct((depth, bn, POOL_BUF, POOL_W), F32)]
    out_specs = [pl.BlockSpec((brows, D_MODEL),
                              lambda l, i, j: (jnp.where(l == depth - 1, i, 0), 0)),
                 pl.BlockSpec((None, sub, HEADS, DKH, DVH), state_idx),
                 pl.BlockSpec((None, sub, POOL_BUF, POOL_W), buf_idx)]
    kern = functools.partial(_step_kernel, n_tok=n_tok, sub=sub, depth=depth)
    scratch = [pltpu.VMEM((rows, D_MODEL), F32), pltpu.VMEM((brows, D_MODEL), F32),
               pltpu.VMEM((brows, D_MODEL), BF16), pltpu.VMEM((brows, DK), F32),
               pltpu.VMEM((brows, DK), F32), pltpu.VMEM((brows, DV), F32),
               pltpu.VMEM((brows, DK), F32), pltpu.VMEM((brows, POOL_W), F32),
               pltpu.VMEM((brows, DV), F32), pltpu.VMEM((brows, POOL_W), F32)]
    return pl.pallas_call(
        kern,
        grid=(depth, n_blk, n_inner),
        in_specs=in_specs,
        out_specs=out_specs,
        out_shape=out_shape,
        scratch_shapes=scratch,
        compiler_params=pltpu.CompilerParams(
            dimension_semantics=("arbitrary", "arbitrary", "arbitrary"),
            vmem_limit_bytes=VMEM_LIMIT_BYTES),
        name="step_layers",
    )(x, s_all, buf_all, cnt, *weights)


def _prep_main_kernel(wt_ref, main_ref):
    main_ref[...] = wt_ref[...].T.astype(BF16)


def _prep_tail_kernel(wt_ref, al_ref, mg_ref):
    lane = lax.broadcasted_iota(jnp.int32, al_ref.shape, 1)
    al_ref[...] = jnp.where(lane < GATE_RANK, wt_ref[0:LANES, :].T, 0.0).astype(BF16)
    mg_ref[...] = wt_ref[GATE_RANK:, :].T.astype(BF16)


def _prep_w_in(w_in):
    depth, rows, cols = w_in.shape
    n_tail = cols - MAIN_COLS
    wt = jnp.swapaxes(w_in, 1, 2)
    cb = PREP_COLS
    assert MAIN_COLS % cb == 0
    w_main = pl.pallas_call(
        _prep_main_kernel,
        grid=(depth, MAIN_COLS // cb),
        in_specs=[pl.BlockSpec((None, cb, rows), lambda l, i: (l, i, 0))],
        out_specs=pl.BlockSpec((None, rows, cb), lambda l, i: (l, 0, i)),
        out_shape=jax.ShapeDtypeStruct((depth, rows, MAIN_COLS), BF16),
        name="prep_w_main",
    )(wt)
    w_al, w_mg = pl.pallas_call(
        _prep_tail_kernel,
        grid=(depth,),
        in_specs=[pl.BlockSpec((None, n_tail, rows), lambda l: (l, 0, 0))],
        out_specs=[pl.BlockSpec((None, rows, LANES), lambda l: (l, 0, 0)),
                   pl.BlockSpec((None, rows, n_tail - GATE_RANK), lambda l: (l, 0, 0))],
        out_shape=[jax.ShapeDtypeStruct((depth, rows, LANES), BF16),
                   jax.ShapeDtypeStruct((depth, rows, n_tail - GATE_RANK), BF16)],
        compiler_params=pltpu.CompilerParams(vmem_limit_bytes=VMEM_LIMIT_BYTES),
        name="prep_w_tail",
    )(wt[:, MAIN_COLS:, :])
    return w_main, w_al, w_mg


def _stacked_weights(norm_g, w_in, w_alpha, b_alpha, gla_gain, w_a, pool_w, pool_scale, w_b,
                     b_merge, w_out, final_norm_g):
    w_main, w_al, w_mg = _prep_w_in(w_in)
    pad = LANES - GATE_RANK
    return {
        "norm_g": norm_g[:, None, :],
        "w_main": w_main, "w_al": w_al, "w_mg": w_mg,
        "w_alpha": jnp.pad(w_alpha.astype(BF16), ((0, 0), (0, pad), (0, 0))),
        "b_alpha": b_alpha[:, None, :],
        "gla_gain": gla_gain[:, None, :],
        "w_a": w_a.astype(BF16),
        "pool_w": pool_w.astype(BF16),
        "pool_scale": pool_scale[:, None, :],
        "w_b": w_b.astype(BF16),
        "b_merge": b_merge[:, None, :],
        "w_out": w_out.astype(BF16),
        "final_g": final_norm_g[None, :],
    }


def _window_row():
    return np.repeat(np.asarray(POOL_WINDOWS, np.float32), POOL_GW)[None, :]


def kernel(x_prompt, x_sample, state_gla, state_pool, meta_tokens, norm_g, w_in, w_alpha, b_alpha,
           gla_gain, w_a, pool_w, pool_scale, w_b, b_merge, w_out, final_norm_g):
    depth = w_in.shape[0]
    dec_b, dec_t, _ = x_sample.shape
    wrow = _window_row()
    cnt_meta = jnp.asarray(np.minimum(wrow, np.arange(1, N_META + 1, dtype=np.float32)[:, None]))
    inv_w = jnp.asarray(1.0 / wrow)

    ws = _stacked_weights(norm_g, w_in, w_alpha, b_alpha, gla_gain, w_a, pool_w, pool_scale, w_b,
                          b_merge, w_out, final_norm_g)
    h_meta = meta_tokens[None].astype(F32)
    h_p = x_prompt
    zero_s = jnp.zeros((1, HEADS, DKH, DVH), F32)
    zero_buf = jnp.zeros((1, POOL_PAD, POOL_W), F32)

    sg_p, sp_p = [], []
    for l in range(depth):
        last = l == depth - 1
        h_meta, s_meta, buf_meta = _seq_layer(h_meta, zero_s, zero_buf, cnt_meta, ws, l,
                                              is_last=False, pow2_counts=False)
        h_p, s_p, b_p = _seq_layer(h_p, s_meta, buf_meta, inv_w, ws, l, is_last=last,
                                   pow2_counts=True)
        sg_p.append(s_p)
        sp_p.append(b_p[:, POOL_PAD - POOL_BUF:, :])

    buf_s = jnp.pad(state_pool, ((0, 0), (0, 0), (POOL_PAD - POOL_BUF, 0), (0, 0)))
    y_s, sg_s, bufn_s = _step_layers(x_sample.reshape(dec_b * dec_t, D_MODEL), state_gla, buf_s,
                                     inv_w, ws, n_tok=dec_t)
    return (h_p, y_s.reshape(dec_b, dec_t, D_MODEL), jnp.stack(sg_p), jnp.stack(sp_p),
            sg_s, bufn_s)
```

```python
import functools

import numpy as np
import jax
import jax.numpy as jnp
from jax import lax
from jax.experimental import pallas as pl
from jax.experimental.pallas import tpu as pltpu

F32 = jnp.float32
BF16 = jnp.bfloat16

D_MODEL = 1024
N_META = 16
HEADS = 4
DKH = 128
DVH = 256
DK = HEADS * DKH
DV = HEADS * DVH
GATE_RANK = 16
GATE_TAU = 16.0
POOL_W = 512
POOL_WINDOWS = (2, 4, 8, 16)
POOL_GW = POOL_W // len(POOL_WINDOWS)
POOL_BUF = max(POOL_WINDOWS) - 1
POOL_PAD = POOL_BUF + 1
PAST_LEN = 16384
EPS = 1e-6
IN_SIZES = (DK, DK, DV, DV, POOL_W, POOL_W, GATE_RANK, 2 * D_MODEL)
IN_OFFS = tuple(int(o) for o in np.cumsum((0,) + IN_SIZES))

LANES = 128
SUBLANES = 8
VMEM_LIMIT_BYTES = 56 * 1024 * 1024

MAX_FACTORED_LOG_DECAY = 60.0

SEQ_BLOCK = 512
SEQ_CHUNK = 256
SAMPLE_BLOCK = 32
SAMPLE_SUB = 8
PREP_COLS = 512


def _mm(a, b):
    return jnp.dot(a, b, preferred_element_type=F32)


def _mm_nt(a, b):
    return lax.dot_general(a, b, (((1,), (1,)), ((), ())), preferred_element_type=F32)


def _mm_tn(a, b):
    return lax.dot_general(a, b, (((0,), (0,)), ((), ())), preferred_element_type=F32)


def _sigmoid(x):
    return 1.0 / (1.0 + jnp.exp(-x))


def _silu(x):
    return x * _sigmoid(x)


def _log_sigmoid(x):
    return jnp.minimum(x, 0.0) - jnp.log1p(jnp.exp(-jnp.abs(x)))


def _rmsnorm(x, g):
    ms = jnp.mean(x * x, axis=-1, keepdims=True)
    return x * lax.rsqrt(ms + EPS) * g


def _row_to_col(row, eye):
    return jnp.sum(eye * row, axis=1, keepdims=True)


def _eye(n):
    r = lax.broadcasted_iota(jnp.int32, (n, n), 0)
    c = lax.broadcasted_iota(jnp.int32, (n, n), 1)
    return (r == c).astype(F32)


W_NAMES = ("norm_g", "w_main", "w_al", "w_mg", "w_alpha", "b_alpha", "gla_gain", "w_a", "pool_w",
           "pool_scale", "w_b", "b_merge", "w_out", "final_g")
MAIN_SEGS = ("w_q", "w_k", "w_v", "w_ga", "w_u", "w_gb")
MAIN_COLS = IN_OFFS[len(MAIN_SEGS)]


def _weight_views(refs):
    w = dict(zip(W_NAMES, refs))
    for i, name in enumerate(MAIN_SEGS):
        w[name] = w["w_main"].at[:, IN_OFFS[i]:IN_OFFS[i + 1]]
    return w


def _gate_in(x, w):
    xn = _rmsnorm(x, w["norm_g"][...]).astype(BF16)
    alow = _mm(xn, w["w_al"][...])
    q = _mm(xn, w["w_q"][...]) * (DKH ** -0.5)
    k = _mm(xn, w["w_k"][...])
    a = _mm(alow.astype(BF16), w["w_alpha"][...]) + w["b_alpha"][...]
    v = _mm(xn, w["w_v"][...])
    u = _mm(xn, w["w_u"][...])
    return xn, q, k, v, u, a


def _log_gate(a):
    return _log_sigmoid(a) * (1.0 / GATE_TAU)


def _dense_out(x, xn, o, z, w, is_last, pre=None):
    if pre is None:
        pre = (_mm(xn, w["w_ga"][...]), _mm(xn, w["w_gb"][...]), _mm(xn, w["w_mg"][...]))
    ga, gb, mg = pre
    gain = w["gla_gain"][...]
    parts = []
    for h in range(HEADS):
        oh = o[:, h * DVH:(h + 1) * DVH]
        ms = jnp.mean(oh * oh, axis=-1, keepdims=True)
        parts.append(oh * lax.rsqrt(ms + EPS) * gain[:, h * DVH:(h + 1) * DVH])
    on = jnp.concatenate(parts, axis=-1) * _silu(ga)
    ya = _mm(on.astype(BF16), w["w_a"][...])

    yp = []
    for g in range(len(POOL_WINDOWS)):
        zg = z[:, g * POOL_GW:(g + 1) * POOL_GW].astype(BF16)
        yp.append(_mm(zg, w["pool_w"][g]))
    y_pool = jnp.concatenate(yp, axis=-1) * w["pool_scale"][...]
    yb = _mm((y_pool * _silu(gb)).astype(BF16), w["w_b"][...])

    gates = _sigmoid(mg + w["b_merge"][...])
    merged = gates[:, :D_MODEL] * ya + gates[:, D_MODEL:] * yb
    h_new = x + _mm(merged.astype(BF16), w["w_out"][...])
    y = _rmsnorm(h_new, w["final_g"][...]) if is_last else None
    return h_new, y


def _window_sums(e):
    p1 = e + pltpu.roll(e, 1, 0)
    t = p1[:, POOL_GW:]
    p2 = t + pltpu.roll(t, 2, 0)
    t = p2[:, POOL_GW:]
    p3 = t + pltpu.roll(t, 4, 0)
    t = p3[:, POOL_GW:]
    p4 = t + pltpu.roll(t, 8, 0)
    return jnp.concatenate([p1[:, :POOL_GW], p2[:, :POOL_GW], p3[:, :POOL_GW], p4], axis=-1)


def _split_hi_lo(x):
    hi = x.astype(BF16)
    lo = (x - hi.astype(F32)).astype(BF16)
    return hi, lo


def _cumsum_rows(x, n):
    pos = lax.broadcasted_iota(jnp.int32, (x.shape[0], 1), 0) % n
    step = 1
    while step < n:
        x = x + jnp.where(pos >= step, pltpu.roll(x, step, 0), 0.0)
        step *= 2
    return x


def _seq_kernel(*refs, tb, chunk, is_last, pow2_counts):
    n_w = len(W_NAMES)
    h_ref, s0_ref, buf0_ref, cnt_ref = refs[:4]
    w = _weight_views(refs[4:4 + n_w])
    hy_ref, s_ref, bufn_ref, o_s, ext_s, oc_s, b_s, k_s = refs[4 + n_w:]

    t = pl.program_id(1)

    @pl.when(t == 0)
    def _():
        s_ref[0] = s0_ref[0]
        ext_s[0:POOL_PAD, :] = buf0_ref[0]

    x = h_ref[0]
    xn, q, k, v, u, a_gate = _gate_in(x, w)
    n_chunks = tb // chunk
    chunks = [(c, h) for c in range(n_chunks) for h in range(HEADS)]

    def sl(c, h):
        return (slice(c * chunk, (c + 1) * chunk), slice(h * DKH, (h + 1) * DKH),
                slice(h * DVH, (h + 1) * DVH))

    neg = jnp.minimum(a_gate, 0.0)
    steepest = jnp.sum(neg[0:chunk], axis=0, keepdims=True)
    for c in range(1, n_chunks):
        steepest = jnp.minimum(
            steepest, jnp.sum(neg[c * chunk:(c + 1) * chunk], axis=0, keepdims=True))
    decay_bound = (jnp.min(steepest) - chunk * np.log(2.0)) * (1.0 / GATE_TAU)
    factored_ok = decay_bound >= -MAX_FACTORED_LOG_DECAY
    la = _log_gate(a_gate)
    vb = v.astype(BF16)
    causal = (lax.broadcasted_iota(jnp.int32, (chunk, chunk), 0)
              >= lax.broadcasted_iota(jnp.int32, (chunk, chunk), 1))
    tril = causal.astype(BF16)
    eye = _eye(DKH)
    la_hi, la_lo = _split_hi_lo(la)
    b = jnp.concatenate(
        [_mm(tril, la_hi[c * chunk:(c + 1) * chunk])
         + _mm(tril, la_lo[c * chunk:(c + 1) * chunk]) for c in range(n_chunks)], axis=0)
    ga = _mm(xn, w["w_ga"][...])
    b_ends = [b[(c + 1) * chunk - 1:(c + 1) * chunk, :] for c in range(n_chunks)]
    to_end = jnp.concatenate(
        [b_ends[c] - b[c * chunk:(c + 1) * chunk, :] for c in range(n_chunks)], axis=0)
    qd = (q * jnp.exp(b)).astype(BF16)
    kd = (k * jnp.exp(to_end)).astype(BF16)

    kinv = (k * jnp.exp(-b)).astype(BF16)
    a_m = {}
    for c, h in chunks:
        rows, ks, _ = sl(c, h)
        a = _mm_nt(qd[rows, ks], kinv[rows, ks])
        a_m[c, h] = jnp.where(causal, a, 0.0).astype(BF16)
    gb = _mm(xn, w["w_gb"][...])
    o_in, d_s, dcol = {}, {}, {}
    for c, h in chunks:
        rows, ks, vs = sl(c, h)
        o_in[c, h] = _mm(a_m[c, h], vb[rows, vs])
        d_s[c, h] = _mm_tn(kd[rows, ks], vb[rows, vs])
        dcol[c, h] = _row_to_col(jnp.exp(b_ends[c])[:, ks], eye)
    mg = _mm(xn, w["w_mg"][...])

    s_cur = [s_ref[0, h] for h in range(HEADS)]
    for c, h in chunks:
        rows, ks, vs = sl(c, h)
        o_state = _mm(qd[rows, ks], s_cur[h].astype(BF16))
        oc_s[rows, vs] = o_state
        o_s[rows, vs] = o_in[c, h] + o_state
        s_cur[h] = dcol[c, h] * s_cur[h] + d_s[c, h]
    for h in range(HEADS):
        s_ref[0, h] = s_cur[h]

    ext_s[POOL_PAD:POOL_PAD + tb, :] = u
    sums = _window_sums(ext_s[...])[POOL_PAD:, :]
    if pow2_counts:
        z = sums * cnt_ref[...] - u
    else:
        z = sums / cnt_ref[...] - u
    bufn_ref[0] = ext_s[tb:tb + POOL_PAD, :]
    ext_s[0:POOL_PAD, :] = ext_s[tb:tb + POOL_PAD, :]

    h_new, y = _dense_out(x, xn, o_s[...], z, w, is_last, pre=(ga, gb, mg))
    hy_ref[0] = y if is_last else h_new

    @pl.when(jnp.logical_not(factored_ok))
    def _():
        b_s[...] = b
        k_s[...] = k
        col_id = lax.broadcasted_iota(jnp.int32, (chunk, chunk), 1)
        for c, h in chunks:
            rows, ks, vs = sl(c, h)
            qh = q[rows, ks]
            bh = b[rows, ks]

            def columns(g, acc, c=c, ks=ks, qh=qh, bh=bh):
                r0 = pl.multiple_of(c * chunk + g * SUBLANES, SUBLANES)
                b_tile = b_s[pl.ds(r0, SUBLANES), ks]
                k_tile = k_s[pl.ds(r0, SUBLANES), ks]
                for r in range(SUBLANES):
                    decay = jnp.exp(jnp.minimum(bh - b_tile[r:r + 1, :], 0.0))
                    col = jnp.sum(qh * k_tile[r:r + 1, :] * decay, axis=1, keepdims=True)
                    acc = jnp.where(col_id == g * SUBLANES + r, col, acc)
                return acc

            a = lax.fori_loop(0, chunk // SUBLANES, columns, jnp.zeros((chunk, chunk), F32))
            a = jnp.where(causal, a, 0.0).astype(BF16)
            o_s[rows, vs] = oc_s[rows, vs] + _mm(a, vb[rows, vs])
        h_fix, y_fix = _dense_out(x, xn, o_s[...], z, w, is_last, pre=(ga, gb, mg))
        hy_ref[0] = y_fix if is_last else h_fix


def _const_spec(shape):
    nd = len(shape)
    return pl.BlockSpec(shape, lambda *_: (0,) * nd, pipeline_mode=pl.Buffered(1))


def _weight_specs(ws, layer_of):
    specs = []
    for name in W_NAMES:
        shape = ws[name].shape
        if name == "final_g":
            specs.append(_const_spec(shape))
        else:
            zeros = (0,) * (len(shape) - 1)
            specs.append(pl.BlockSpec((None,) + shape[1:],
                                      lambda *g, zeros=zeros: (layer_of(*g),) + zeros,
                                      pipeline_mode=pl.Buffered(1)))
    return specs


def _seq_layer(h, s0, buf0, cnt, ws, layer, *, is_last, pow2_counts):
    bn, t_len, _ = h.shape
    tb = min(SEQ_BLOCK, t_len)
    chunk = min(SEQ_CHUNK, tb)
    assert t_len % tb == 0 and tb % chunk == 0
    shared = s0.shape[0] == 1
    bsel = (lambda b, t: (0, 0, 0, 0)) if shared else (lambda b, t: (b, 0, 0, 0))
    bsel3 = (lambda b, t: (0, 0, 0)) if shared else (lambda b, t: (b, 0, 0))
    weights = [ws[n] for n in W_NAMES]
    in_specs = [
        pl.BlockSpec((1, tb, D_MODEL), lambda b, t: (b, t, 0)),
        pl.BlockSpec((1, HEADS, DKH, DVH), bsel),
        pl.BlockSpec((1, POOL_PAD, POOL_W), bsel3),
        _const_spec(cnt.shape),
    ] + _weight_specs(ws, lambda b, t: layer)
    out_shape = [jax.ShapeDtypeStruct(h.shape, F32),
                 jax.ShapeDtypeStruct((bn, HEADS, DKH, DVH), F32),
                 jax.ShapeDtypeStruct((bn, POOL_PAD, POOL_W), F32)]
    out_specs = [pl.BlockSpec((1, tb, D_MODEL), lambda b, t: (b, t, 0)),
                 pl.BlockSpec((1, HEADS, DKH, DVH), lambda b, t: (b, 0, 0, 0)),
                 pl.BlockSpec((1, POOL_PAD, POOL_W), lambda b, t: (b, 0, 0))]
    kern = functools.partial(_seq_kernel, tb=tb, chunk=chunk, is_last=is_last,
                             pow2_counts=pow2_counts)
    return pl.pallas_call(
        kern,
        grid=(bn, t_len // tb),
        in_specs=in_specs,
        out_specs=out_specs,
        out_shape=out_shape,
        scratch_shapes=[pltpu.VMEM((tb, DV), F32), pltpu.VMEM((tb + POOL_PAD, POOL_W), F32),
                        pltpu.VMEM((tb, DV), F32),
                        pltpu.VMEM((tb, DK), F32), pltpu.VMEM((tb, DK), F32)],
        compiler_params=pltpu.CompilerParams(
            dimension_semantics=("arbitrary", "arbitrary"), vmem_limit_bytes=VMEM_LIMIT_BYTES),
        name="seq_layer",
    )(h, s0, buf0, cnt, *weights)


def _step_kernel(*refs, n_tok, sub, depth):
    n_w = len(W_NAMES)
    x_ref, s0_ref, buf0_ref, cnt_ref = refs[:4]
    w = _weight_views(refs[4:4 + n_w])
    y_ref, s_ref, bufn_ref = refs[4 + n_w:4 + n_w + 3]
    h_s, xin_s, xn_s, q_s, k_s, v_s, la_s, u_s, o_s, z_s = refs[4 + n_w + 3:]

    layer = pl.program_id(0)
    blk = pl.program_id(1)
    j = pl.program_id(2)
    nj = pl.num_programs(2)
    brows = xin_s.shape[0]
    blk_rows = pl.ds(pl.multiple_of(blk * brows, brows), brows)
    pair = SUBLANES // n_tok
    assert pair == 2 and sub % pair == 0

    @pl.when(j == 0)
    def _():
        @pl.when(layer == 0)
        def _():
            xin_s[...] = x_ref[...]

        @pl.when(layer > 0)
        def _():
            xin_s[...] = h_s[blk_rows, :]

        xn, q, k, v, u, a_gate = _gate_in(xin_s[...], w)
        la = _log_gate(a_gate)
        xn_s[...] = xn
        q_s[...] = q
        k_s[...] = k
        v_s[...] = v
        la_s[...] = la
        u_s[...] = u

    nr = sub * n_tok
    rows = pl.ds(pl.multiple_of(j * nr, nr), nr)
    row = lax.broadcasted_iota(jnp.int32, (nr, 1), 0)
    tok = row % n_tok
    seq = row // n_tok
    eye = _eye(DKH)
    b = _cumsum_rows(la_s[rows, :], n_tok)
    b_end = b
    for d in range(1, n_tok):
        b_end = jnp.where(tok == n_tok - 1 - d, pltpu.roll(b, nr - d, 0), b_end)
    q = q_s[rows, :]
    k = k_s[rows, :]
    v = v_s[rows, :]
    qd = (q * jnp.exp(b)).astype(BF16)
    kd = k * jnp.exp(b_end - b)
    dec = jnp.exp(b_end)
    vb = v.astype(BF16)
    o_parts = []
    for h in range(HEADS):
        ks = slice(h * DKH, (h + 1) * DKH)
        vs = slice(h * DVH, (h + 1) * DVH)
        o = jnp.sum(q[:, ks] * k[:, ks], axis=1, keepdims=True) * v[:, vs]
        for d in range(1, n_tok):
            kj = pltpu.roll(k[:, ks], d, 0)
            bj = pltpu.roll(b[:, ks], d, 0)
            score = jnp.sum(q[:, ks] * kj * jnp.exp(jnp.minimum(b[:, ks] - bj, 0.0)), axis=1,
                            keepdims=True)
            o = o + jnp.where(tok >= d, score, 0.0) * pltpu.roll(v[:, vs], d, 0)
        o_parts.append(o)
    for h in range(HEADS):
        ks = slice(h * DKH, (h + 1) * DKH)
        vs = slice(h * DVH, (h + 1) * DVH)
        o = o_parts[h]
        for s in range(sub):
            mine = seq == s
            s_old = s0_ref[s, h]
            o = o + jnp.where(mine, _mm(qd[:, ks], s_old.astype(BF16)), 0.0)
            kd_s = jnp.where(mine, kd[:, ks], 0.0).astype(BF16)
            dcol = _row_to_col(dec[s * n_tok:s * n_tok + 1, ks], eye)
            s_ref[s, h] = dcol * s_old + _mm_tn(kd_s, vb[:, vs])
        o_parts[h] = o
    o_s[rows, :] = jnp.concatenate(o_parts, axis=-1)

    first = lax.broadcasted_iota(jnp.int32, (SUBLANES, 1), 0) < n_tok
    inv_cnt = cnt_ref[...]
    for p in range(sub // pair):
        r0 = pl.multiple_of(j * nr + p * SUBLANES, SUBLANES)
        rows = pl.ds(r0, SUBLANES)
        u = u_s[rows, :]
        e0 = jnp.concatenate([buf0_ref[pair * p], u], axis=0)
        e1 = jnp.concatenate([buf0_ref[pair * p + 1], pltpu.roll(u, n_tok, 0)], axis=0)
        n_ext = POOL_PAD + SUBLANES
        w0 = _window_sums(e0)[POOL_PAD:, :]
        w1 = _window_sums(e1)[POOL_PAD:, :]
        sums = jnp.where(first, w0, pltpu.roll(w1, n_tok, 0))
        z_s[rows, :] = sums * inv_cnt - u
        drop = n_tok + POOL_PAD - POOL_BUF
        bufn_ref[pair * p] = pltpu.roll(e0, n_ext - drop, 0)[0:POOL_BUF, :]
        bufn_ref[pair * p + 1] = pltpu.roll(e1, n_ext - drop, 0)[0:POOL_BUF, :]

    @pl.when(j == nj - 1)
    def _():
        h_new, y = _dense_out(xin_s[...], xn_s[...], o_s[...], z_s[...], w, True)

        @pl.when(layer < depth - 1)
        def _():
            h_s[blk_rows, :] = h_new

        @pl.when(layer == depth - 1)
        def _():
            y_ref[...] = y


def _step_layers(x, s_all, buf_all, cnt, ws, *, n_tok):
    rows, _ = x.shape
    depth, bn = s_all.shape[:2]
    sub, blk = SAMPLE_SUB, SAMPLE_BLOCK
    assert rows == bn * n_tok and bn % blk == 0 and blk % sub == 0
    n_blk, n_inner = bn // blk, blk // sub
    brows = blk * n_tok
    weights = [ws[n] for n in W_NAMES]

    def state_idx(l, i, j):
        return (l, i * n_inner + j, 0, 0, 0)

    def buf_idx(l, i, j):
        return (l, i * n_inner + j, 0, 0)

    in_specs = [
        pl.BlockSpec((brows, D_MODEL), lambda l, i, j: (i, 0)),
        pl.BlockSpec((None, sub, HEADS, DKH, DVH), state_idx),
        pl.BlockSpec((None, sub, POOL_PAD, POOL_W), buf_idx),
        _const_spec(cnt.shape),
    ] + _weight_specs(ws, lambda l, i, j: l)
    out_shape = [jax.ShapeDtypeStruct(x.shape, F32),
                 jax.ShapeDtypeStruct(s_all.shape, F32),
                 jax.ShapeDtypeStruct((depth, bn, POOL_BUF, POOL_W), F32)]
    out_specs = [pl.BlockSpec((brows, D_MODEL),
                              lambda l, i, j: (jnp.where(l == depth - 1, i, 0), 0)),
                 pl.BlockSpec((None, sub, HEADS, DKH, DVH), state_idx),
                 pl.BlockSpec((None, sub, POOL_BUF, POOL_W), buf_idx)]
    kern = functools.partial(_step_kernel, n_tok=n_tok, sub=sub, depth=depth)
    scratch = [pltpu.VMEM((rows, D_MODEL), F32), pltpu.VMEM((brows, D_MODEL), F32),
               pltpu.VMEM((brows, D_MODEL), BF16), pltpu.VMEM((brows, DK), F32),
               pltpu.VMEM((brows, DK), F32), pltpu.VMEM((brows, DV), F32),
               pltpu.VMEM((brows, DK), F32), pltpu.VMEM((brows, POOL_W), F32),
               pltpu.VMEM((brows, DV), F32), pltpu.VMEM((brows, POOL_W), F32)]
    return pl.pallas_call(
        kern,
        grid=(depth, n_blk, n_inner),
        in_specs=in_specs,
        out_specs=out_specs,
        out_shape=out_shape,
        scratch_shapes=scratch,
        compiler_params=pltpu.CompilerParams(
            dimension_semantics=("arbitrary", "arbitrary", "arbitrary"),
            vmem_limit_bytes=VMEM_LIMIT_BYTES),
        name="step_layers",
    )(x, s_all, buf_all, cnt, *weights)


def _prep_main_kernel(wt_ref, main_ref):
    main_ref[...] = wt_ref[...].T.astype(BF16)


def _prep_tail_kernel(wt_ref, al_ref, mg_ref):
    lane = lax.broadcasted_iota(jnp.int32, al_ref.shape, 1)
    al_ref[...] = jnp.where(lane < GATE_RANK, wt_ref[0, 0:LANES, :].T, 0.0).astype(BF16)
    mg_ref[...] = wt_ref[0, GATE_RANK:, :].T.astype(BF16)


def _prep_w_in(w_in):
    depth, rows, cols = w_in.shape
    n_tail = cols - MAIN_COLS
    wt = jnp.swapaxes(w_in, 1, 2)
    cb = PREP_COLS
    assert MAIN_COLS % cb == 0
    w_main = pl.pallas_call(
        _prep_main_kernel,
        grid=(depth, MAIN_COLS // cb),
        in_specs=[pl.BlockSpec((None, cb, rows), lambda l, i: (l, i, 0))],
        out_specs=pl.BlockSpec((None, rows, cb), lambda l, i: (l, 0, i)),
        out_shape=jax.ShapeDtypeStruct((depth, rows, MAIN_COLS), BF16),
        name="prep_w_main",
    )(wt)
    w_al, w_mg = pl.pallas_call(
        _prep_tail_kernel,
        grid=(depth,),
        in_specs=[pl.BlockSpec((pl.Element(1), pl.Element(n_tail), pl.Element(rows)),
                               lambda l: (l, MAIN_COLS, 0))],
        out_specs=[pl.BlockSpec((None, rows, LANES), lambda l: (l, 0, 0)),
                   pl.BlockSpec((None, rows, n_tail - GATE_RANK), lambda l: (l, 0, 0))],
        out_shape=[jax.ShapeDtypeStruct((depth, rows, LANES), BF16),
                   jax.ShapeDtypeStruct((depth, rows, n_tail - GATE_RANK), BF16)],
        compiler_params=pltpu.CompilerParams(vmem_limit_bytes=VMEM_LIMIT_BYTES),
        name="prep_w_tail",
    )(wt)
    return w_main, w_al, w_mg


def _stacked_weights(norm_g, w_in, w_alpha, b_alpha, gla_gain, w_a, pool_w, pool_scale, w_b,
                     b_merge, w_out, final_norm_g):
    w_main, w_al, w_mg = _prep_w_in(w_in)
    pad = LANES - GATE_RANK
    return {
        "norm_g": norm_g[:, None, :],
        "w_main": w_main, "w_al": w_al, "w_mg": w_mg,
        "w_alpha": jnp.pad(w_alpha.astype(BF16), ((0, 0), (0, pad), (0, 0))),
        "b_alpha": b_alpha[:, None, :],
        "gla_gain": gla_gain[:, None, :],
        "w_a": w_a.astype(BF16),
        "pool_w": pool_w.astype(BF16),
        "pool_scale": pool_scale[:, None, :],
        "w_b": w_b.astype(BF16),
        "b_merge": b_merge[:, None, :],
        "w_out": w_out.astype(BF16),
        "final_g": final_norm_g[None, :],
    }


def _window_row():
    return np.repeat(np.asarray(POOL_WINDOWS, np.float32), POOL_GW)[None, :]


def kernel(x_prompt, x_sample, state_gla, state_pool, meta_tokens, norm_g, w_in, w_alpha, b_alpha,
           gla_gain, w_a, pool_w, pool_scale, w_b, b_merge, w_out, final_norm_g):
    depth = w_in.shape[0]
    dec_b, dec_t, _ = x_sample.shape
    wrow = _window_row()
    cnt_meta = jnp.asarray(np.minimum(wrow, np.arange(1, N_META + 1, dtype=np.float32)[:, None]))
    inv_w = jnp.asarray(1.0 / wrow)

    ws = _stacked_weights(norm_g, w_in, w_alpha, b_alpha, gla_gain, w_a, pool_w, pool_scale, w_b,
                          b_merge, w_out, final_norm_g)
    h_meta = meta_tokens[None].astype(F32)
    h_p = x_prompt
    zero_s = jnp.zeros((1, HEADS, DKH, DVH), F32)
    zero_buf = jnp.zeros((1, POOL_PAD, POOL_W), F32)

    sg_p, sp_p = [], []
    for l in range(depth):
        last = l == depth - 1
        h_meta, s_meta, buf_meta = _seq_layer(h_meta, zero_s, zero_buf, cnt_meta, ws, l,
                                              is_last=False, pow2_counts=False)
        h_p, s_p, b_p = _seq_layer(h_p, s_meta, buf_meta, inv_w, ws, l, is_last=last,
                                   pow2_counts=True)
        sg_p.append(s_p)
        sp_p.append(b_p[:, POOL_PAD - POOL_BUF:, :])

    buf_s = jnp.pad(state_pool, ((0, 0), (0, 0), (POOL_PAD - POOL_BUF, 0), (0, 0)))
    y_s, sg_s, bufn_s = _step_layers(x_sample.reshape(dec_b * dec_t, D_MODEL), state_gla, buf_s,
                                     inv_w, ws, n_tok=dec_t)
    return (h_p, y_s.reshape(dec_b, dec_t, D_MODEL), jnp.stack(sg_p), jnp.stack(sp_p),
            sg_s, bufn_s)
```

```python
import functools

import numpy as np
import jax
import jax.numpy as jnp
from jax import lax
from jax.experimental import pallas as pl
from jax.experimental.pallas import tpu as pltpu

F32 = jnp.float32
BF16 = jnp.bfloat16

D_MODEL = 1024
N_META = 16
HEADS = 4
DKH = 128
DVH = 256
DK = HEADS * DKH
DV = HEADS * DVH
GATE_RANK = 16
GATE_TAU = 16.0
POOL_W = 512
POOL_WINDOWS = (2, 4, 8, 16)
POOL_GW = POOL_W // len(POOL_WINDOWS)
POOL_BUF = max(POOL_WINDOWS) - 1
POOL_PAD = POOL_BUF + 1
EPS = 1e-6
IN_SIZES = (DK, DK, DV, DV, POOL_W, POOL_W, GATE_RANK, 2 * D_MODEL)
IN_OFFS = tuple(int(o) for o in np.cumsum((0,) + IN_SIZES))

LANES = 128
SUBLANES = 8
VMEM_LIMIT_BYTES = 56 * 1024 * 1024

MAX_FACTORED_LOG_DECAY = 60.0

SEQ_BLOCK = 512
SEQ_CHUNK = 256
SAMPLE_BLOCK = 32
SAMPLE_SUB = 8
PREP_COLS = 512


def _mm(a, b):
    return jnp.dot(a, b, preferred_element_type=F32)


def _mm_nt(a, b):
    return lax.dot_general(a, b, (((1,), (1,)), ((), ())), preferred_element_type=F32)


def _mm_tn(a, b):
    return lax.dot_general(a, b, (((0,), (0,)), ((), ())), preferred_element_type=F32)


def _sigmoid(x):
    return 1.0 / (1.0 + jnp.exp(-x))


def _silu(x):
    return x * _sigmoid(x)


def _log_sigmoid(x):
    return jnp.minimum(x, 0.0) - jnp.log1p(jnp.exp(-jnp.abs(x)))


def _rmsnorm(x, g):
    ms = jnp.mean(x * x, axis=-1, keepdims=True)
    return x * lax.rsqrt(ms + EPS) * g


def _row_to_col(row, eye):
    return jnp.sum(eye * row, axis=1, keepdims=True)


def _eye(n):
    r = lax.broadcasted_iota(jnp.int32, (n, n), 0)
    c = lax.broadcasted_iota(jnp.int32, (n, n), 1)
    return (r == c).astype(F32)


W_NAMES = ("norm_g", "w_main", "w_al", "w_mg", "w_alpha", "b_alpha", "gla_gain", "w_a", "pool_w",
           "pool_scale", "w_b", "b_merge", "w_out", "final_g")
MAIN_SEGS = ("w_q", "w_k", "w_v", "w_ga", "w_u", "w_gb")
MAIN_COLS = IN_OFFS[len(MAIN_SEGS)]


def _weight_views(refs):
    w = dict(zip(W_NAMES, refs))
    for i, name in enumerate(MAIN_SEGS):
        w[name] = w["w_main"].at[:, IN_OFFS[i]:IN_OFFS[i + 1]]
    return w


def _gate_in(x, w):
    xn = _rmsnorm(x, w["norm_g"][...]).astype(BF16)
    alow = _mm(xn, w["w_al"][...])
    q = _mm(xn, w["w_q"][...]) * (DKH ** -0.5)
    k = _mm(xn, w["w_k"][...])
    a = _mm(alow.astype(BF16), w["w_alpha"][...]) + w["b_alpha"][...]
    v = _mm(xn, w["w_v"][...])
    u = _mm(xn, w["w_u"][...])
    return xn, q, k, v, u, a


def _log_gate(a):
    return _log_sigmoid(a) * (1.0 / GATE_TAU)


def _dense_out(x, xn, o, z, w, is_last, pre=None):
    if pre is None:
        pre = (_mm(xn, w["w_ga"][...]), _mm(xn, w["w_gb"][...]), _mm(xn, w["w_mg"][...]))
    ga, gb, mg = pre
    gain = w["gla_gain"][...]
    parts = []
    for h in range(HEADS):
        oh = o[:, h * DVH:(h + 1) * DVH]
        ms = jnp.mean(oh * oh, axis=-1, keepdims=True)
        parts.append(oh * lax.rsqrt(ms + EPS) * gain[:, h * DVH:(h + 1) * DVH])
    on = jnp.concatenate(parts, axis=-1) * _silu(ga)
    ya = _mm(on.astype(BF16), w["w_a"][...])

    yp = []
    for g in range(len(POOL_WINDOWS)):
        zg = z[:, g * POOL_GW:(g + 1) * POOL_GW].astype(BF16)
        yp.append(_mm(zg, w["pool_w"][g]))
    y_pool = jnp.concatenate(yp, axis=-1) * w["pool_scale"][...]
    yb = _mm((y_pool * _silu(gb)).astype(BF16), w["w_b"][...])

    gates = _sigmoid(mg + w["b_merge"][...])
    merged = gates[:, :D_MODEL] * ya + gates[:, D_MODEL:] * yb
    h_new = x + _mm(merged.astype(BF16), w["w_out"][...])
    y = _rmsnorm(h_new, w["final_g"][...]) if is_last else None
    return h_new, y


def _window_sums(e):
    p1 = e + pltpu.roll(e, 1, 0)
    t = p1[:, POOL_GW:]
    p2 = t + pltpu.roll(t, 2, 0)
    t = p2[:, POOL_GW:]
    p3 = t + pltpu.roll(t, 4, 0)
    t = p3[:, POOL_GW:]
    p4 = t + pltpu.roll(t, 8, 0)
    return jnp.concatenate([p1[:, :POOL_GW], p2[:, :POOL_GW], p3[:, :POOL_GW], p4], axis=-1)


def _split_hi_lo(x):
    hi = x.astype(BF16)
    lo = (x - hi.astype(F32)).astype(BF16)
    return hi, lo


def _cumsum_rows(x, n):
    pos = lax.broadcasted_iota(jnp.int32, (x.shape[0], 1), 0) % n
    step = 1
    while step < n:
        x = x + jnp.where(pos >= step, pltpu.roll(x, step, 0), 0.0)
        step *= 2
    return x


def _seq_kernel(*refs, tb, chunk, is_last):
    n_w = len(W_NAMES)
    h_ref, s0_ref, buf0_ref, cnt_ref = refs[:4]
    w = _weight_views(refs[4:4 + n_w])
    hy_ref, s_ref, bufn_ref, o_s, ext_s, oc_s, b_s, k_s = refs[4 + n_w:]

    t = pl.program_id(1)

    @pl.when(t == 0)
    def _():
        s_ref[0] = s0_ref[0]
        ext_s[0:POOL_PAD, :] = buf0_ref[0]

    x = h_ref[0]
    xn, q, k, v, u, a_gate = _gate_in(x, w)
    n_chunks = tb // chunk
    chunks = [(c, h) for c in range(n_chunks) for h in range(HEADS)]

    def sl(c, h):
        return (slice(c * chunk, (c + 1) * chunk), slice(h * DKH, (h + 1) * DKH),
                slice(h * DVH, (h + 1) * DVH))

    neg = jnp.minimum(a_gate, 0.0)
    steepest = jnp.sum(neg[0:chunk], axis=0, keepdims=True)
    for c in range(1, n_chunks):
        steepest = jnp.minimum(
            steepest, jnp.sum(neg[c * chunk:(c + 1) * chunk], axis=0, keepdims=True))
    decay_bound = (jnp.min(steepest) - chunk * np.log(2.0)) * (1.0 / GATE_TAU)
    factored_ok = decay_bound >= -MAX_FACTORED_LOG_DECAY
    la = _log_gate(a_gate)
    vb = v.astype(BF16)
    causal = (lax.broadcasted_iota(jnp.int32, (chunk, chunk), 0)
              >= lax.broadcasted_iota(jnp.int32, (chunk, chunk), 1))
    tril = causal.astype(BF16)
    eye = _eye(DKH)
    la_hi, la_lo = _split_hi_lo(la)
    b = jnp.concatenate(
        [_mm(tril, la_hi[c * chunk:(c + 1) * chunk])
         + _mm(tril, la_lo[c * chunk:(c + 1) * chunk]) for c in range(n_chunks)], axis=0)
    ga = _mm(xn, w["w_ga"][...])
    b_ends = [b[(c + 1) * chunk - 1:(c + 1) * chunk, :] for c in range(n_chunks)]
    to_end = jnp.concatenate(
        [b_ends[c] - b[c * chunk:(c + 1) * chunk, :] for c in range(n_chunks)], axis=0)
    qd = (q * jnp.exp(b)).astype(BF16)
    kd = (k * jnp.exp(to_end)).astype(BF16)

    kinv = (k * jnp.exp(-b)).astype(BF16)
    a_m = {}
    for c, h in chunks:
        rows, ks, _ = sl(c, h)
        a = _mm_nt(qd[rows, ks], kinv[rows, ks])
        a_m[c, h] = jnp.where(causal, a, 0.0).astype(BF16)
    gb = _mm(xn, w["w_gb"][...])
    o_in, d_s, dcol = {}, {}, {}
    for c, h in chunks:
        rows, ks, vs = sl(c, h)
        o_in[c, h] = _mm(a_m[c, h], vb[rows, vs])
        d_s[c, h] = _mm_tn(kd[rows, ks], vb[rows, vs])
        dcol[c, h] = _row_to_col(jnp.exp(b_ends[c])[:, ks], eye)
    mg = _mm(xn, w["w_mg"][...])

    s_cur = [s_ref[0, h] for h in range(HEADS)]
    for c, h in chunks:
        rows, ks, vs = sl(c, h)
        o_state = _mm(qd[rows, ks], s_cur[h].astype(BF16))
        oc_s[rows, vs] = o_state
        o_s[rows, vs] = o_in[c, h] + o_state
        s_cur[h] = dcol[c, h] * s_cur[h] + d_s[c, h]
    for h in range(HEADS):
        s_ref[0, h] = s_cur[h]

    ext_s[POOL_PAD:POOL_PAD + tb, :] = u
    z = _window_sums(ext_s[...])[POOL_PAD:, :] * cnt_ref[...] - u
    bufn_ref[0] = ext_s[tb:tb + POOL_PAD, :]
    ext_s[0:POOL_PAD, :] = ext_s[tb:tb + POOL_PAD, :]

    h_new, y = _dense_out(x, xn, o_s[...], z, w, is_last, pre=(ga, gb, mg))
    hy_ref[0] = y if is_last else h_new

    @pl.when(jnp.logical_not(factored_ok))
    def _():
        b_s[...] = b
        k_s[...] = k
        col_id = lax.broadcasted_iota(jnp.int32, (chunk, chunk), 1)
        for c, h in chunks:
            rows, ks, vs = sl(c, h)
            qh = q[rows, ks]
            bh = b[rows, ks]

            def columns(g, acc, c=c, ks=ks, qh=qh, bh=bh):
                r0 = pl.multiple_of(c * chunk + g * SUBLANES, SUBLANES)
                b_tile = b_s[pl.ds(r0, SUBLANES), ks]
                k_tile = k_s[pl.ds(r0, SUBLANES), ks]
                for r in range(SUBLANES):
                    decay = jnp.exp(jnp.minimum(bh - b_tile[r:r + 1, :], 0.0))
                    col = jnp.sum(qh * k_tile[r:r + 1, :] * decay, axis=1, keepdims=True)
                    acc = jnp.where(col_id == g * SUBLANES + r, col, acc)
                return acc

            a = lax.fori_loop(0, chunk // SUBLANES, columns, jnp.zeros((chunk, chunk), F32))
            a = jnp.where(causal, a, 0.0).astype(BF16)
            o_s[rows, vs] = oc_s[rows, vs] + _mm(a, vb[rows, vs])
        h_fix, y_fix = _dense_out(x, xn, o_s[...], z, w, is_last, pre=(ga, gb, mg))
        hy_ref[0] = y_fix if is_last else h_fix


def _const_spec(shape):
    nd = len(shape)
    return pl.BlockSpec(shape, lambda *_: (0,) * nd, pipeline_mode=pl.Buffered(1))


def _weight_specs(ws, layer_of):
    specs = []
    for name in W_NAMES:
        shape = ws[name].shape
        if name == "final_g":
            specs.append(_const_spec(shape))
        else:
            zeros = (0,) * (len(shape) - 1)
            specs.append(pl.BlockSpec((None,) + shape[1:],
                                      lambda *g, zeros=zeros: (layer_of(*g),) + zeros,
                                      pipeline_mode=pl.Buffered(1)))
    return specs


def _seq_layer(h, s_meta, buf_meta, inv_cnt, ws, layer, *, is_last):
    bn, t_len, _ = h.shape
    tb = min(SEQ_BLOCK, t_len)
    chunk = min(SEQ_CHUNK, tb)
    assert t_len % tb == 0 and tb % chunk == 0 and buf_meta.shape[1] == POOL_PAD
    weights = [ws[n] for n in W_NAMES]
    in_specs = [
        pl.BlockSpec((1, tb, D_MODEL), lambda b, t: (b, t, 0)),
        pl.BlockSpec((1, HEADS, DKH, DVH), lambda b, t: (layer, 0, 0, 0)),
        pl.BlockSpec((1, POOL_PAD, POOL_W), lambda b, t: (layer, 0, 0)),
        _const_spec(inv_cnt.shape),
    ] + _weight_specs(ws, lambda b, t: layer)
    out_shape = [jax.ShapeDtypeStruct(h.shape, F32),
                 jax.ShapeDtypeStruct((bn, HEADS, DKH, DVH), F32),
                 jax.ShapeDtypeStruct((bn, POOL_PAD, POOL_W), F32)]
    out_specs = [pl.BlockSpec((1, tb, D_MODEL), lambda b, t: (b, t, 0)),
                 pl.BlockSpec((1, HEADS, DKH, DVH), lambda b, t: (b, 0, 0, 0)),
                 pl.BlockSpec((1, POOL_PAD, POOL_W), lambda b, t: (b, 0, 0))]
    kern = functools.partial(_seq_kernel, tb=tb, chunk=chunk, is_last=is_last)
    return pl.pallas_call(
        kern,
        grid=(bn, t_len // tb),
        in_specs=in_specs,
        out_specs=out_specs,
        out_shape=out_shape,
        scratch_shapes=[pltpu.VMEM((tb, DV), F32), pltpu.VMEM((tb + POOL_PAD, POOL_W), F32),
                        pltpu.VMEM((tb, DV), F32),
                        pltpu.VMEM((tb, DK), F32), pltpu.VMEM((tb, DK), F32)],
        compiler_params=pltpu.CompilerParams(
            dimension_semantics=("arbitrary", "arbitrary"), vmem_limit_bytes=VMEM_LIMIT_BYTES),
        name="seq_layer",
    )(h, s_meta, buf_meta, inv_cnt, *weights)


def _intra_by_distance(q, k, v, b, tok, n_tok):
    o_parts = []
    for h in range(HEADS):
        ks = slice(h * DKH, (h + 1) * DKH)
        vs = slice(h * DVH, (h + 1) * DVH)
        o = jnp.sum(q[:, ks] * k[:, ks], axis=1, keepdims=True) * v[:, vs]
        for d in range(1, n_tok):
            kj = pltpu.roll(k[:, ks], d, 0)
            bj = pltpu.roll(b[:, ks], d, 0)
            score = jnp.sum(q[:, ks] * kj * jnp.exp(jnp.minimum(b[:, ks] - bj, 0.0)), axis=1,
                            keepdims=True)
            o = o + jnp.where(tok >= d, score, 0.0) * pltpu.roll(v[:, vs], d, 0)
        o_parts.append(o)
    return o_parts


def _meta_layer(xm, cnt, w):
    n = xm.shape[0]
    xn, q, k, v, u, a_gate = _gate_in(xm, w)
    b = _cumsum_rows(_log_gate(a_gate), n)
    tok = lax.broadcasted_iota(jnp.int32, (n, 1), 0)
    o = jnp.concatenate(_intra_by_distance(q, k, v, b, tok, n), axis=-1)
    kd = (k * jnp.exp(b[n - 1:n, :] - b)).astype(BF16)
    vb = v.astype(BF16)
    state = [_mm_tn(kd[:, h * DKH:(h + 1) * DKH], vb[:, h * DVH:(h + 1) * DVH])
             for h in range(HEADS)]
    ext = jnp.concatenate([jnp.zeros((POOL_PAD, POOL_W), F32), u], axis=0)
    z = _window_sums(ext)[POOL_PAD:, :] / cnt - u
    h_new, _ = _dense_out(xm, xn, o, z, w, False)
    return h_new, state, u
def _step_kernel(*refs, n_tok, sub, depth):
    n_w = len(W_NAMES)
    n_in = 6
    x_ref, s0_ref, buf0_ref, cnt_ref, meta_ref, cntm_ref = refs[:n_in]
    w = _weight_views(refs[n_in:n_in + n_w])
    y_ref, s_ref, bufn_ref, smeta_ref, bufmeta_ref = refs[n_in + n_w:n_in + n_w + 5]
    h_s, hm_s, xin_s, xn_s, q_s, k_s, v_s, la_s, u_s, o_s, z_s = refs[n_in + n_w + 5:]

    layer = pl.program_id(0)
    blk = pl.program_id(1)
    j = pl.program_id(2)
    nj = pl.num_programs(2)
    brows = xin_s.shape[0]
    blk_rows = pl.ds(pl.multiple_of(blk * brows, brows), brows)
    pair = SUBLANES // n_tok
    assert pair == 2 and sub % pair == 0

    @pl.when((blk == 0) & (j == 0))
    def _():
        @pl.when(layer == 0)
        def _():
            hm_s[...] = meta_ref[...]

        hm_new, state, u_meta = _meta_layer(hm_s[...], cntm_ref[...], w)
        hm_s[...] = hm_new
        for h in range(HEADS):
            smeta_ref[h] = state[h]
        bufmeta_ref[...] = u_meta

    @pl.when(j == 0)
    def _():
        @pl.when(layer == 0)
        def _():
            xin_s[...] = x_ref[...]

        @pl.when(layer > 0)
        def _():
            xin_s[...] = h_s[blk_rows, :]

        xn, q, k, v, u, a_gate = _gate_in(xin_s[...], w)
        la = _log_gate(a_gate)
        xn_s[...] = xn
        q_s[...] = q
        k_s[...] = k
        v_s[...] = v
        la_s[...] = la
        u_s[...] = u

    nr = sub * n_tok
    rows = pl.ds(pl.multiple_of(j * nr, nr), nr)
    row = lax.broadcasted_iota(jnp.int32, (nr, 1), 0)
    tok = row % n_tok
    seq = row // n_tok
    eye = _eye(DKH)
    b = _cumsum_rows(la_s[rows, :], n_tok)
    b_end = b
    for d in range(1, n_tok):
        b_end = jnp.where(tok == n_tok - 1 - d, pltpu.roll(b, nr - d, 0), b_end)
    q = q_s[rows, :]
    k = k_s[rows, :]
    v = v_s[rows, :]
    qd = (q * jnp.exp(b)).astype(BF16)
    kd = k * jnp.exp(b_end - b)
    dec = jnp.exp(b_end)
    vb = v.astype(BF16)
    o_parts = _intra_by_distance(q, k, v, b, tok, n_tok)
    for h in range(HEADS):
        ks = slice(h * DKH, (h + 1) * DKH)
        vs = slice(h * DVH, (h + 1) * DVH)
        o = o_parts[h]
        for s in range(sub):
            mine = seq == s
            s_old = s0_ref[s, h]
            o = o + jnp.where(mine, _mm(qd[:, ks], s_old.astype(BF16)), 0.0)
            kd_s = jnp.where(mine, kd[:, ks], 0.0).astype(BF16)
            dcol = _row_to_col(dec[s * n_tok:s * n_tok + 1, ks], eye)
            s_ref[s, h] = dcol * s_old + _mm_tn(kd_s, vb[:, vs])
        o_parts[h] = o
    o_s[rows, :] = jnp.concatenate(o_parts, axis=-1)

    first = lax.broadcasted_iota(jnp.int32, (SUBLANES, 1), 0) < n_tok
    inv_cnt = cnt_ref[...]
    for p in range(sub // pair):
        r0 = pl.multiple_of(j * nr + p * SUBLANES, SUBLANES)
        rows = pl.ds(r0, SUBLANES)
        u = u_s[rows, :]
        e0 = jnp.concatenate([buf0_ref[pair * p], u], axis=0)
        e1 = jnp.concatenate([buf0_ref[pair * p + 1], pltpu.roll(u, n_tok, 0)], axis=0)
        n_ext = POOL_PAD + SUBLANES
        w0 = _window_sums(e0)[POOL_PAD:, :]
        w1 = _window_sums(e1)[POOL_PAD:, :]
        sums = jnp.where(first, w0, pltpu.roll(w1, n_tok, 0))
        z_s[rows, :] = sums * inv_cnt - u
        drop = n_tok + POOL_PAD - POOL_BUF
        bufn_ref[pair * p] = pltpu.roll(e0, n_ext - drop, 0)[0:POOL_BUF, :]
        bufn_ref[pair * p + 1] = pltpu.roll(e1, n_ext - drop, 0)[0:POOL_BUF, :]

    @pl.when(j == nj - 1)
    def _():
        h_new, y = _dense_out(xin_s[...], xn_s[...], o_s[...], z_s[...], w, True)

        @pl.when(layer < depth - 1)
        def _():
            h_s[blk_rows, :] = h_new

        @pl.when(layer == depth - 1)
        def _():
            y_ref[...] = y


def _step_layers(x, s_all, buf_all, cnt, meta, cnt_meta, ws, *, n_tok):
    rows, _ = x.shape
    depth, bn = s_all.shape[:2]
    sub, blk = SAMPLE_SUB, SAMPLE_BLOCK
    assert rows == bn * n_tok and bn % blk == 0 and blk % sub == 0
    n_blk, n_inner = bn // blk, blk // sub
    brows = blk * n_tok
    weights = [ws[n] for n in W_NAMES]

    def state_idx(l, i, j):
        return (l, i * n_inner + j, 0, 0, 0)

    def buf_idx(l, i, j):
        return (l, i * n_inner + j, 0, 0)

    in_specs = [
        pl.BlockSpec((brows, D_MODEL), lambda l, i, j: (i, 0)),
        pl.BlockSpec((None, sub, HEADS, DKH, DVH), state_idx),
        pl.BlockSpec((None, sub, POOL_PAD, POOL_W), buf_idx),
        _const_spec(cnt.shape),
        _const_spec(meta.shape),
        _const_spec(cnt_meta.shape),
    ] + _weight_specs(ws, lambda l, i, j: l)
    n_meta = meta.shape[0]
    out_shape = [jax.ShapeDtypeStruct(x.shape, F32),
                 jax.ShapeDtypeStruct(s_all.shape, F32),
                 jax.ShapeDtypeStruct((depth, bn, POOL_BUF, POOL_W), F32),
                 jax.ShapeDtypeStruct((depth, HEADS, DKH, DVH), F32),
                 jax.ShapeDtypeStruct((depth, n_meta, POOL_W), F32)]
    out_specs = [pl.BlockSpec((brows, D_MODEL),
                              lambda l, i, j: (jnp.where(l == depth - 1, i, 0), 0)),
                 pl.BlockSpec((None, sub, HEADS, DKH, DVH), state_idx),
                 pl.BlockSpec((None, sub, POOL_BUF, POOL_W), buf_idx),
                 pl.BlockSpec((None, HEADS, DKH, DVH), lambda l, i, j: (l, 0, 0, 0)),
                 pl.BlockSpec((None, n_meta, POOL_W), lambda l, i, j: (l, 0, 0))]
    kern = functools.partial(_step_kernel, n_tok=n_tok, sub=sub, depth=depth)
    scratch = [pltpu.VMEM((rows, D_MODEL), F32), pltpu.VMEM((n_meta, D_MODEL), F32),
               pltpu.VMEM((brows, D_MODEL), F32),
               pltpu.VMEM((brows, D_MODEL), BF16), pltpu.VMEM((brows, DK), F32),
               pltpu.VMEM((brows, DK), F32), pltpu.VMEM((brows, DV), F32),
               pltpu.VMEM((brows, DK), F32), pltpu.VMEM((brows, POOL_W), F32),
               pltpu.VMEM((brows, DV), F32), pltpu.VMEM((brows, POOL_W), F32)]
    return pl.pallas_call(
        kern,
        grid=(depth, n_blk, n_inner),
        in_specs=in_specs,
        out_specs=out_specs,
        out_shape=out_shape,
        scratch_shapes=scratch,
        compiler_params=pltpu.CompilerParams(
            dimension_semantics=("arbitrary", "arbitrary", "arbitrary"),
            vmem_limit_bytes=VMEM_LIMIT_BYTES),
        name="step_layers",
    )(x, s_all, buf_all, cnt, meta, cnt_meta, *weights)


def _prep_main_kernel(wt_ref, main_ref):
    main_ref[...] = wt_ref[...].T.astype(BF16)


def _prep_tail_kernel(wt_ref, al_ref, mg_ref):
    lane = lax.broadcasted_iota(jnp.int32, al_ref.shape, 1)
    al_ref[...] = jnp.where(lane < GATE_RANK, wt_ref[0, 0:LANES, :].T, 0.0).astype(BF16)
    mg_ref[...] = wt_ref[0, GATE_RANK:, :].T.astype(BF16)


def _prep_w_in(w_in):
    depth, rows, cols = w_in.shape
    n_tail = cols - MAIN_COLS
    wt = jnp.swapaxes(w_in, 1, 2)
    cb = PREP_COLS
    assert MAIN_COLS % cb == 0
    w_main = pl.pallas_call(
        _prep_main_kernel,
        grid=(depth, MAIN_COLS // cb),
        in_specs=[pl.BlockSpec((None, cb, rows), lambda l, i: (l, i, 0))],
        out_specs=pl.BlockSpec((None, rows, cb), lambda l, i: (l, 0, i)),
        out_shape=jax.ShapeDtypeStruct((depth, rows, MAIN_COLS), BF16),
        name="prep_w_main",
    )(wt)
    w_al, w_mg = pl.pallas_call(
        _prep_tail_kernel,
        grid=(depth,),
        in_specs=[pl.BlockSpec((pl.Element(1), pl.Element(n_tail), pl.Element(rows)),
                               lambda l: (l, MAIN_COLS, 0))],
        out_specs=[pl.BlockSpec((None, rows, LANES), lambda l: (l, 0, 0)),
                   pl.BlockSpec((None, rows, n_tail - GATE_RANK), lambda l: (l, 0, 0))],
        out_shape=[jax.ShapeDtypeStruct((depth, rows, LANES), BF16),
                   jax.ShapeDtypeStruct((depth, rows, n_tail - GATE_RANK), BF16)],
        compiler_params=pltpu.CompilerParams(vmem_limit_bytes=VMEM_LIMIT_BYTES),
        name="prep_w_tail",
    )(wt)
    return w_main, w_al, w_mg


def _stacked_weights(norm_g, w_in, w_alpha, b_alpha, gla_gain, w_a, pool_w, pool_scale, w_b,
                     b_merge, w_out, final_norm_g):
    w_main, w_al, w_mg = _prep_w_in(w_in)
    pad = LANES - GATE_RANK
    return {
        "norm_g": norm_g[:, None, :],
        "w_main": w_main, "w_al": w_al, "w_mg": w_mg,
        "w_alpha": jnp.pad(w_alpha.astype(BF16), ((0, 0), (0, pad), (0, 0))),
        "b_alpha": b_alpha[:, None, :],
        "gla_gain": gla_gain[:, None, :],
        "w_a": w_a.astype(BF16),
        "pool_w": pool_w.astype(BF16),
        "pool_scale": pool_scale[:, None, :],
        "w_b": w_b.astype(BF16),
        "b_merge": b_merge[:, None, :],
        "w_out": w_out.astype(BF16),
        "final_g": final_norm_g[None, :],
    }


def _window_row():
    return np.repeat(np.asarray(POOL_WINDOWS, np.float32), POOL_GW)[None, :]


def kernel(x_prompt, x_sample, state_gla, state_pool, meta_tokens, norm_g, w_in, w_alpha, b_alpha,
           gla_gain, w_a, pool_w, pool_scale, w_b, b_merge, w_out, final_norm_g):
    depth = w_in.shape[0]
    dec_b, dec_t, _ = x_sample.shape
    wrow = _window_row()
    cnt_meta = jnp.asarray(np.minimum(wrow, np.arange(1, N_META + 1, dtype=np.float32)[:, None]))
    inv_w = jnp.asarray(1.0 / wrow)

    ws = _stacked_weights(norm_g, w_in, w_alpha, b_alpha, gla_gain, w_a, pool_w, pool_scale, w_b,
                          b_merge, w_out, final_norm_g)
    buf_s = jnp.pad(state_pool, ((0, 0), (0, 0), (POOL_PAD - POOL_BUF, 0), (0, 0)))
    y_s, sg_s, bufn_s, s_meta, buf_meta = _step_layers(
        x_sample.reshape(dec_b * dec_t, D_MODEL), state_gla, buf_s, inv_w,
        meta_tokens.astype(F32), cnt_meta, ws, n_tok=dec_t)

    h_p = x_prompt
    sg_p, sp_p = [], []
    for l in range(depth):
        h_p, s_p, b_p = _seq_layer(h_p, s_meta, buf_meta, inv_w, ws, l, is_last=l == depth - 1)
        sg_p.append(s_p)
        sp_p.append(b_p[:, POOL_PAD - POOL_BUF:, :])
    return (h_p, y_s.reshape(dec_b, dec_t, D_MODEL), jnp.stack(sg_p), jnp.stack(sp_p),
            sg_s, bufn_s)
```

```python
import functools

import numpy as np
import jax
import jax.numpy as jnp
from jax import lax
from jax.experimental import pallas as pl
from jax.experimental.pallas import tpu as pltpu

F32 = jnp.float32
BF16 = jnp.bfloat16

D_MODEL = 1024
N_META = 16
HEADS = 4
DKH = 128
DVH = 256
DK = HEADS * DKH
DV = HEADS * DVH
GATE_RANK = 16
GATE_TAU = 16.0
POOL_W = 512
POOL_WINDOWS = (2, 4, 8, 16)
POOL_GW = POOL_W // len(POOL_WINDOWS)
POOL_BUF = max(POOL_WINDOWS) - 1
POOL_PAD = POOL_BUF + 1
EPS = 1e-6
IN_SIZES = (DK, DK, DV, DV, POOL_W, POOL_W, GATE_RANK, 2 * D_MODEL)
IN_OFFS = tuple(int(o) for o in np.cumsum((0,) + IN_SIZES))

LANES = 128
SUBLANES = 8
VMEM_LIMIT_BYTES = 56 * 1024 * 1024

MAX_FACTORED_LOG_DECAY = 60.0

SEQ_BLOCK = 512
SEQ_CHUNK = 256
SAMPLE_BLOCK = 32
SAMPLE_SUB = 8
PREP_COLS = 512


def _mm(a, b):
    return jnp.dot(a, b, preferred_element_type=F32)


def _mm_nt(a, b):
    return lax.dot_general(a, b, (((1,), (1,)), ((), ())), preferred_element_type=F32)


def _mm_tn(a, b):
    return lax.dot_general(a, b, (((0,), (0,)), ((), ())), preferred_element_type=F32)


def _sigmoid(x):
    return 1.0 / (1.0 + jnp.exp(-x))


def _silu(x):
    return x * _sigmoid(x)


def _log_sigmoid(x):
    return jnp.minimum(x, 0.0) - jnp.log1p(jnp.exp(-jnp.abs(x)))


def _rmsnorm(x, g):
    ms = jnp.mean(x * x, axis=-1, keepdims=True)
    return x * lax.rsqrt(ms + EPS) * g


def _row_to_col(row, eye):
    return jnp.sum(eye * row, axis=1, keepdims=True)


def _eye(n):
    r = lax.broadcasted_iota(jnp.int32, (n, n), 0)
    c = lax.broadcasted_iota(jnp.int32, (n, n), 1)
    return (r == c).astype(F32)


W_NAMES = ("norm_g", "w_main", "w_al", "w_mg", "w_alpha", "b_alpha", "gla_gain", "w_a", "pool_w",
           "pool_scale", "w_b", "b_merge", "w_out", "final_g")
MAIN_SEGS = ("w_q", "w_k", "w_v", "w_ga", "w_u", "w_gb")
MAIN_COLS = IN_OFFS[len(MAIN_SEGS)]


def _weight_views(refs):
    w = dict(zip(W_NAMES, refs))
    for i, name in enumerate(MAIN_SEGS):
        w[name] = w["w_main"].at[:, IN_OFFS[i]:IN_OFFS[i + 1]]
    return w


def _gate_in(x, w):
    xn = _rmsnorm(x, w["norm_g"][...]).astype(BF16)
    alow = _mm(xn, w["w_al"][...])
    q = _mm(xn, w["w_q"][...]) * (DKH ** -0.5)
    k = _mm(xn, w["w_k"][...])
    a = _mm(alow.astype(BF16), w["w_alpha"][...]) + w["b_alpha"][...]
    v = _mm(xn, w["w_v"][...])
    u = _mm(xn, w["w_u"][...])
    return xn, q, k, v, u, a


def _log_gate(a):
    return _log_sigmoid(a) * (1.0 / GATE_TAU)


def _dense_out(x, xn, o, z, w, is_last, pre=None):
    if pre is None:
        pre = (_mm(xn, w["w_ga"][...]), _mm(xn, w["w_gb"][...]), _mm(xn, w["w_mg"][...]))
    ga, gb, mg = pre
    gain = w["gla_gain"][...]
    parts = []
    for h in range(HEADS):
        oh = o[:, h * DVH:(h + 1) * DVH]
        ms = jnp.mean(oh * oh, axis=-1, keepdims=True)
        parts.append(oh * lax.rsqrt(ms + EPS) * gain[:, h * DVH:(h + 1) * DVH])
    on = jnp.concatenate(parts, axis=-1) * _silu(ga)
    ya = _mm(on.astype(BF16), w["w_a"][...])

    yp = []
    for g in range(len(POOL_WINDOWS)):
        zg = z[:, g * POOL_GW:(g + 1) * POOL_GW].astype(BF16)
        yp.append(_mm(zg, w["pool_w"][g]))
    y_pool = jnp.concatenate(yp, axis=-1) * w["pool_scale"][...]
    yb = _mm((y_pool * _silu(gb)).astype(BF16), w["w_b"][...])

    gates = _sigmoid(mg + w["b_merge"][...])
    merged = gates[:, :D_MODEL] * ya + gates[:, D_MODEL:] * yb
    h_new = x + _mm(merged.astype(BF16), w["w_out"][...])
    y = _rmsnorm(h_new, w["final_g"][...]) if is_last else None
    return h_new, y


def _window_sums(e):
    p1 = e + pltpu.roll(e, 1, 0)
    t = p1[:, POOL_GW:]
    p2 = t + pltpu.roll(t, 2, 0)
    t = p2[:, POOL_GW:]
    p3 = t + pltpu.roll(t, 4, 0)
    t = p3[:, POOL_GW:]
    p4 = t + pltpu.roll(t, 8, 0)
    return jnp.concatenate([p1[:, :POOL_GW], p2[:, :POOL_GW], p3[:, :POOL_GW], p4], axis=-1)


def _split_hi_lo(x):
    hi = x.astype(BF16)
    lo = (x - hi.astype(F32)).astype(BF16)
    return hi, lo


def _cumsum_rows(x, n):
    pos = lax.broadcasted_iota(jnp.int32, (x.shape[0], 1), 0) % n
    step = 1
    while step < n:
        x = x + jnp.where(pos >= step, pltpu.roll(x, step, 0), 0.0)
        step *= 2
    return x


def _seq_kernel(*refs, tb, chunk, is_last):
    n_w = len(W_NAMES)
    h_ref, s0_ref, buf0_ref, cnt_ref = refs[:4]
    w = _weight_views(refs[4:4 + n_w])
    hy_ref, s_ref, bufn_ref, o_s, ext_s, oc_s, b_s, k_s = refs[4 + n_w:]

    t = pl.program_id(1)

    @pl.when(t == 0)
    def _():
        s_ref[0] = s0_ref[0]
        ext_s[0:POOL_PAD, :] = buf0_ref[0]

    x = h_ref[0]
    xn, q, k, v, u, a_gate = _gate_in(x, w)
    n_chunks = tb // chunk
    chunks = [(c, h) for c in range(n_chunks) for h in range(HEADS)]

    def sl(c, h):
        return (slice(c * chunk, (c + 1) * chunk), slice(h * DKH, (h + 1) * DKH),
                slice(h * DVH, (h + 1) * DVH))

    neg = jnp.minimum(a_gate, 0.0)
    steepest = jnp.sum(neg[0:chunk], axis=0, keepdims=True)
    for c in range(1, n_chunks):
        steepest = jnp.minimum(
            steepest, jnp.sum(neg[c * chunk:(c + 1) * chunk], axis=0, keepdims=True))
    decay_bound = (jnp.min(steepest) - chunk * np.log(2.0)) * (1.0 / GATE_TAU)
    factored_ok = decay_bound >= -MAX_FACTORED_LOG_DECAY
    la = _log_gate(a_gate)
    vb = v.astype(BF16)
    causal = (lax.broadcasted_iota(jnp.int32, (chunk, chunk), 0)
              >= lax.broadcasted_iota(jnp.int32, (chunk, chunk), 1))
    tril = causal.astype(BF16)
    eye = _eye(DKH)
    la_hi, la_lo = _split_hi_lo(la)
    b = jnp.concatenate(
        [_mm(tril, la_hi[c * chunk:(c + 1) * chunk])
         + _mm(tril, la_lo[c * chunk:(c + 1) * chunk]) for c in range(n_chunks)], axis=0)
    ga = _mm(xn, w["w_ga"][...])
    b_ends = [b[(c + 1) * chunk - 1:(c + 1) * chunk, :] for c in range(n_chunks)]
    to_end = jnp.concatenate(
        [b_ends[c] - b[c * chunk:(c + 1) * chunk, :] for c in range(n_chunks)], axis=0)
    qd = (q * jnp.exp(b)).astype(BF16)
    kd = (k * jnp.exp(to_end)).astype(BF16)

    kinv = (k * jnp.exp(-b)).astype(BF16)
    a_m = {}
    for c, h in chunks:
        rows, ks, _ = sl(c, h)
        a = _mm_nt(qd[rows, ks], kinv[rows, ks])
        a_m[c, h] = jnp.where(causal, a, 0.0).astype(BF16)
    gb = _mm(xn, w["w_gb"][...])
    o_in, d_s, dcol = {}, {}, {}
    for c, h in chunks:
        rows, ks, vs = sl(c, h)
        o_in[c, h] = _mm(a_m[c, h], vb[rows, vs])
        d_s[c, h] = _mm_tn(kd[rows, ks], vb[rows, vs])
        dcol[c, h] = _row_to_col(jnp.exp(b_ends[c])[:, ks], eye)
    mg = _mm(xn, w["w_mg"][...])

    s_cur = [s_ref[0, h] for h in range(HEADS)]
    for c, h in chunks:
        rows, ks, vs = sl(c, h)
        o_state = _mm(qd[rows, ks], s_cur[h].astype(BF16))
        oc_s[rows, vs] = o_state
        o_s[rows, vs] = o_in[c, h] + o_state
        s_cur[h] = dcol[c, h] * s_cur[h] + d_s[c, h]
    for h in range(HEADS):
        s_ref[0, h] = s_cur[h]

    ext_s[POOL_PAD:POOL_PAD + tb, :] = u
    z = _window_sums(ext_s[...])[POOL_PAD:, :] * cnt_ref[...] - u
    bufn_ref[0] = ext_s[tb:tb + POOL_PAD, :]
    ext_s[0:POOL_PAD, :] = ext_s[tb:tb + POOL_PAD, :]

    h_new, y = _dense_out(x, xn, o_s[...], z, w, is_last, pre=(ga, gb, mg))
    hy_ref[0] = y if is_last else h_new

    @pl.when(jnp.logical_not(factored_ok))
    def _():
        b_s[...] = b
        k_s[...] = k
        col_id = lax.broadcasted_iota(jnp.int32, (chunk, chunk), 1)
        for c, h in chunks:
            rows, ks, vs = sl(c, h)
            qh = q[rows, ks]
            bh = b[rows, ks]

            def columns(g, acc, c=c, ks=ks, qh=qh, bh=bh):
                r0 = pl.multiple_of(c * chunk + g * SUBLANES, SUBLANES)
                b_tile = b_s[pl.ds(r0, SUBLANES), ks]
                k_tile = k_s[pl.ds(r0, SUBLANES), ks]
                for r in range(SUBLANES):
                    decay = jnp.exp(jnp.minimum(bh - b_tile[r:r + 1, :], 0.0))
                    col = jnp.sum(qh * k_tile[r:r + 1, :] * decay, axis=1, keepdims=True)
                    acc = jnp.where(col_id == g * SUBLANES + r, col, acc)
                return acc

            a = lax.fori_loop(0, chunk // SUBLANES, columns, jnp.zeros((chunk, chunk), F32))
            a = jnp.where(causal, a, 0.0).astype(BF16)
            o_s[rows, vs] = oc_s[rows, vs] + _mm(a, vb[rows, vs])
        h_fix, y_fix = _dense_out(x, xn, o_s[...], z, w, is_last, pre=(ga, gb, mg))
        hy_ref[0] = y_fix if is_last else h_fix


def _const_spec(shape):
    nd = len(shape)
    return pl.BlockSpec(shape, lambda *_: (0,) * nd, pipeline_mode=pl.Buffered(1))


def _weight_specs(ws, layer_of):
    specs = []
    for name in W_NAMES:
        shape = ws[name].shape
        if name == "final_g":
            specs.append(_const_spec(shape))
        else:
            zeros = (0,) * (len(shape) - 1)
            specs.append(pl.BlockSpec((None,) + shape[1:],
                                      lambda *g, zeros=zeros: (layer_of(*g),) + zeros,
                                      pipeline_mode=pl.Buffered(1)))
    return specs


def _seq_layer(h, s_meta, buf_meta, inv_cnt, ws, layer, *, is_last):
    bn, t_len, _ = h.shape
    tb = min(SEQ_BLOCK, t_len)
    chunk = min(SEQ_CHUNK, tb)
    assert t_len % tb == 0 and tb % chunk == 0 and buf_meta.shape[1] == POOL_PAD
    weights = [ws[n] for n in W_NAMES]
    in_specs = [
        pl.BlockSpec((1, tb, D_MODEL), lambda b, t: (b, t, 0)),
        pl.BlockSpec((1, HEADS, DKH, DVH), lambda b, t: (layer, 0, 0, 0)),
        pl.BlockSpec((1, POOL_PAD, POOL_W), lambda b, t: (layer, 0, 0)),
        _const_spec(inv_cnt.shape),
    ] + _weight_specs(ws, lambda b, t: layer)
    out_shape = [jax.ShapeDtypeStruct(h.shape, F32),
                 jax.ShapeDtypeStruct((bn, HEADS, DKH, DVH), F32),
                 jax.ShapeDtypeStruct((bn, POOL_PAD, POOL_W), F32)]
    out_specs = [pl.BlockSpec((1, tb, D_MODEL), lambda b, t: (b, t, 0)),
                 pl.BlockSpec((1, HEADS, DKH, DVH), lambda b, t: (b, 0, 0, 0)),
                 pl.BlockSpec((1, POOL_PAD, POOL_W), lambda b, t: (b, 0, 0))]
    kern = functools.partial(_seq_kernel, tb=tb, chunk=chunk, is_last=is_last)
    return pl.pallas_call(
        kern,
        grid=(bn, t_len // tb),
        in_specs=in_specs,
        out_specs=out_specs,
        out_shape=out_shape,
        scratch_shapes=[pltpu.VMEM((tb, DV), F32), pltpu.VMEM((tb + POOL_PAD, POOL_W), F32),
                        pltpu.VMEM((tb, DV), F32),
                        pltpu.VMEM((tb, DK), F32), pltpu.VMEM((tb, DK), F32)],
        compiler_params=pltpu.CompilerParams(
            dimension_semantics=("arbitrary", "arbitrary"), vmem_limit_bytes=VMEM_LIMIT_BYTES),
        name="seq_layer",
    )(h, s_meta, buf_meta, inv_cnt, *weights)


def _intra_by_distance(q, k, v, b, tok, n_tok):
    o_parts = []
    for h in range(HEADS):
        ks = slice(h * DKH, (h + 1) * DKH)
        vs = slice(h * DVH, (h + 1) * DVH)
        o = jnp.sum(q[:, ks] * k[:, ks], axis=1, keepdims=True) * v[:, vs]
        for d in range(1, n_tok):
            kj = pltpu.roll(k[:, ks], d, 0)
            bj = pltpu.roll(b[:, ks], d, 0)
            score = jnp.sum(q[:, ks] * kj * jnp.exp(jnp.minimum(b[:, ks] - bj, 0.0)), axis=1,
                            keepdims=True)
            o = o + jnp.where(tok >= d, score, 0.0) * pltpu.roll(v[:, vs], d, 0)
        o_parts.append(o)
    return o_parts


def _meta_layer(xm, cnt, w):
    n = xm.shape[0]
    xn, q, k, v, u, a_gate = _gate_in(xm, w)
    b = _cumsum_rows(_log_gate(a_gate), n)
    tok = lax.broadcasted_iota(jnp.int32, (n, 1), 0)
    o = jnp.concatenate(_intra_by_distance(q, k, v, b, tok, n), axis=-1)
    kd = (k * jnp.exp(b[n - 1:n, :] - b)).astype(BF16)
    vb = v.astype(BF16)
    state = [_mm_tn(kd[:, h * DKH:(h + 1) * DKH], vb[:, h * DVH:(h + 1) * DVH])
             for h in range(HEADS)]
    ext = jnp.concatenate([jnp.zeros((POOL_PAD, POOL_W), F32), u], axis=0)
    z = _window_sums(ext)[POOL_PAD:, :] / cnt - u
    h_new, _ = _dense_out(xm, xn, o, z, w, False)
    return h_new, state, u
def _step_kernel(*refs, n_tok, sub, depth):
    n_w = len(W_NAMES)
    n_in = 6
    x_ref, s0_ref, buf0_ref, cnt_ref, meta_ref, cntm_ref = refs[:n_in]
    w = _weight_views(refs[n_in:n_in + n_w])
    y_ref, s_ref, bufn_ref, smeta_ref, bufmeta_ref = refs[n_in + n_w:n_in + n_w + 5]
    h_s, hm_s, xin_s, xn_s, q_s, k_s, v_s, la_s, u_s, o_s, z_s = refs[n_in + n_w + 5:]

    layer = pl.program_id(0)
    blk = pl.program_id(1)
    j = pl.program_id(2)
    nj = pl.num_programs(2)
    brows = xin_s.shape[0]
    blk_rows = pl.ds(pl.multiple_of(blk * brows, brows), brows)
    assert sub % SUBLANES == 0

    @pl.when((blk == 0) & (j == 0))
    def _():
        @pl.when(layer == 0)
        def _():
            hm_s[...] = meta_ref[...]

        hm_new, state, u_meta = _meta_layer(hm_s[...], cntm_ref[...], w)
        hm_s[...] = hm_new
        for h in range(HEADS):
            smeta_ref[h] = state[h]
        bufmeta_ref[...] = u_meta

    @pl.when(j == 0)
    def _():
        @pl.when(layer == 0)
        def _():
            xin_s[...] = x_ref[...]

        @pl.when(layer > 0)
        def _():
            xin_s[...] = h_s[blk_rows, :]

        xn, q, k, v, u, a_gate = _gate_in(xin_s[...], w)
        la = _log_gate(a_gate)
        xn_s[...] = xn
        q_s[...] = q
        k_s[...] = k
        v_s[...] = v
        la_s[...] = la
        for g in range(len(POOL_WINDOWS)):
            u_s[g] = u[:, g * POOL_GW:(g + 1) * POOL_GW]

    nr = sub * n_tok
    rows = pl.ds(pl.multiple_of(j * nr, nr), nr)
    row = lax.broadcasted_iota(jnp.int32, (nr, 1), 0)
    tok = row % n_tok
    seq = row // n_tok
    eye = _eye(DKH)
    b = _cumsum_rows(la_s[rows, :], n_tok)
    b_end = b
    for d in range(1, n_tok):
        b_end = jnp.where(tok == n_tok - 1 - d, pltpu.roll(b, nr - d, 0), b_end)
    q = q_s[rows, :]
    k = k_s[rows, :]
    v = v_s[rows, :]
    qd = (q * jnp.exp(b)).astype(BF16)
    kd = k * jnp.exp(b_end - b)
    dec = jnp.exp(b_end)
    vb = v.astype(BF16)
    o_parts = _intra_by_distance(q, k, v, b, tok, n_tok)
    for h in range(HEADS):
        ks = slice(h * DKH, (h + 1) * DKH)
        vs = slice(h * DVH, (h + 1) * DVH)
        o = o_parts[h]
        for s in range(sub):
            mine = seq == s
            s_old = s0_ref[s, h]
            o = o + jnp.where(mine, _mm(qd[:, ks], s_old.astype(BF16)), 0.0)
            kd_s = jnp.where(mine, kd[:, ks], 0.0).astype(BF16)
            dcol = _row_to_col(dec[s * n_tok:s * n_tok + 1, ks], eye)
            s_ref[s, h] = dcol * s_old + _mm_tn(kd_s, vb[:, vs])
        o_parts[h] = o
    o_s[rows, :] = jnp.concatenate(o_parts, axis=-1)

    inv_cnt = cnt_ref[...]
    base = pl.multiple_of(j * nr, nr)
    hist = [buf0_ref[i] for i in range(POOL_BUF)]
    new = []
    for t in range(n_tok):
        rows_t = pl.ds(base + t, sub, stride=n_tok)
        new.append(jnp.concatenate([u_s[g, rows_t, :] for g in range(len(POOL_WINDOWS))], axis=-1))
    ext = hist + new
    for t in range(n_tok):
        last = POOL_BUF + t
        rows_t = pl.ds(base + t, sub, stride=n_tok)
        for g, width in enumerate(POOL_WINDOWS):
            cols = slice(g * POOL_GW, (g + 1) * POOL_GW)
            acc = ext[last][:, cols]
            for back in range(1, width):
                acc = acc + ext[last - back][:, cols]
            z_s[g, rows_t, :] = acc * inv_cnt[:, cols] - new[t][:, cols]
    for i in range(POOL_BUF):
        bufn_ref[i] = ext[i + n_tok]

    @pl.when(j == nj - 1)
    def _():
        z = jnp.concatenate([z_s[g] for g in range(len(POOL_WINDOWS))], axis=-1)
        h_new, y = _dense_out(xin_s[...], xn_s[...], o_s[...], z, w, True)

        @pl.when(layer < depth - 1)
        def _():
            h_s[blk_rows, :] = h_new

        @pl.when(layer == depth - 1)
        def _():
            y_ref[...] = y


def _step_layers(x, s_all, buf_all, cnt, meta, cnt_meta, ws, *, n_tok):
    rows, _ = x.shape
    depth, bn = s_all.shape[:2]
    assert buf_all.shape == (depth, POOL_BUF, bn, POOL_W)
    sub, blk = SAMPLE_SUB, SAMPLE_BLOCK
    assert rows == bn * n_tok and bn % blk == 0 and blk % sub == 0
    n_blk, n_inner = bn // blk, blk // sub
    brows = blk * n_tok
    weights = [ws[n] for n in W_NAMES]

    def state_idx(l, i, j):
        return (l, i * n_inner + j, 0, 0, 0)

    def buf_idx(l, i, j):
        return (l, 0, i * n_inner + j, 0)

    in_specs = [
        pl.BlockSpec((brows, D_MODEL), lambda l, i, j: (i, 0)),
        pl.BlockSpec((None, sub, HEADS, DKH, DVH), state_idx),
        pl.BlockSpec((None, POOL_BUF, sub, POOL_W), buf_idx),
        _const_spec(cnt.shape),
        _const_spec(meta.shape),
        _const_spec(cnt_meta.shape),
    ] + _weight_specs(ws, lambda l, i, j: l)
    n_meta = meta.shape[0]
    out_shape = [jax.ShapeDtypeStruct(x.shape, F32),
                 jax.ShapeDtypeStruct(s_all.shape, F32),
                 jax.ShapeDtypeStruct(buf_all.shape, F32),
                 jax.ShapeDtypeStruct((depth, HEADS, DKH, DVH), F32),
                 jax.ShapeDtypeStruct((depth, n_meta, POOL_W), F32)]
    out_specs = [pl.BlockSpec((brows, D_MODEL),
                              lambda l, i, j: (jnp.where(l == depth - 1, i, 0), 0)),
                 pl.BlockSpec((None, sub, HEADS, DKH, DVH), state_idx),
                 pl.BlockSpec((None, POOL_BUF, sub, POOL_W), buf_idx),
                 pl.BlockSpec((None, HEADS, DKH, DVH), lambda l, i, j: (l, 0, 0, 0)),
                 pl.BlockSpec((None, n_meta, POOL_W), lambda l, i, j: (l, 0, 0))]
    kern = functools.partial(_step_kernel, n_tok=n_tok, sub=sub, depth=depth)
    scratch = [pltpu.VMEM((rows, D_MODEL), F32), pltpu.VMEM((n_meta, D_MODEL), F32),
               pltpu.VMEM((brows, D_MODEL), F32),
               pltpu.VMEM((brows, D_MODEL), BF16), pltpu.VMEM((brows, DK), F32),
               pltpu.VMEM((brows, DK), F32), pltpu.VMEM((brows, DV), F32),
               pltpu.VMEM((brows, DK), F32), pltpu.VMEM((len(POOL_WINDOWS), brows, POOL_GW), F32),
               pltpu.VMEM((brows, DV), F32), pltpu.VMEM((len(POOL_WINDOWS), brows, POOL_GW), F32)]
    return pl.pallas_call(
        kern,
        grid=(depth, n_blk, n_inner),
        in_specs=in_specs,
        out_specs=out_specs,
        out_shape=out_shape,
        scratch_shapes=scratch,
        compiler_params=pltpu.CompilerParams(
            dimension_semantics=("arbitrary", "arbitrary", "arbitrary"),
            vmem_limit_bytes=VMEM_LIMIT_BYTES),
        name="step_layers",
    )(x, s_all, buf_all, cnt, meta, cnt_meta, *weights)


def _prep_main_kernel(wt_ref, main_ref):
    main_ref[...] = wt_ref[...].T.astype(BF16)


def _prep_tail_kernel(wt_ref, al_ref, mg_ref):
    lane = lax.broadcasted_iota(jnp.int32, al_ref.shape, 1)
    al_ref[...] = jnp.where(lane < GATE_RANK, wt_ref[0, 0:LANES, :].T, 0.0).astype(BF16)
    mg_ref[...] = wt_ref[0, GATE_RANK:, :].T.astype(BF16)


def _prep_w_in(w_in):
    depth, rows, cols = w_in.shape
    n_tail = cols - MAIN_COLS
    wt = jnp.swapaxes(w_in, 1, 2)
    cb = PREP_COLS
    assert MAIN_COLS % cb == 0
    w_main = pl.pallas_call(
        _prep_main_kernel,
        grid=(depth, MAIN_COLS // cb),
        in_specs=[pl.BlockSpec((None, cb, rows), lambda l, i: (l, i, 0))],
        out_specs=pl.BlockSpec((None, rows, cb), lambda l, i: (l, 0, i)),
        out_shape=jax.ShapeDtypeStruct((depth, rows, MAIN_COLS), BF16),
        name="prep_w_main",
    )(wt)
    w_al, w_mg = pl.pallas_call(
        _prep_tail_kernel,
        grid=(depth,),
        in_specs=[pl.BlockSpec((pl.Element(1), pl.Element(n_tail), pl.Element(rows)),
                               lambda l: (l, MAIN_COLS, 0))],
        out_specs=[pl.BlockSpec((None, rows, LANES), lambda l: (l, 0, 0)),
                   pl.BlockSpec((None, rows, n_tail - GATE_RANK), lambda l: (l, 0, 0))],
        out_shape=[jax.ShapeDtypeStruct((depth, rows, LANES), BF16),
                   jax.ShapeDtypeStruct((depth, rows, n_tail - GATE_RANK), BF16)],
        compiler_params=pltpu.CompilerParams(vmem_limit_bytes=VMEM_LIMIT_BYTES),
        name="prep_w_tail",
    )(wt)
    return w_main, w_al, w_mg


def _stacked_weights(norm_g, w_in, w_alpha, b_alpha, gla_gain, w_a, pool_w, pool_scale, w_b,
                     b_merge, w_out, final_norm_g):
    w_main, w_al, w_mg = _prep_w_in(w_in)
    pad = LANES - GATE_RANK
    return {
        "norm_g": norm_g[:, None, :],
        "w_main": w_main, "w_al": w_al, "w_mg": w_mg,
        "w_alpha": jnp.pad(w_alpha.astype(BF16), ((0, 0), (0, pad), (0, 0))),
        "b_alpha": b_alpha[:, None, :],
        "gla_gain": gla_gain[:, None, :],
        "w_a": w_a.astype(BF16),
        "pool_w": pool_w.astype(BF16),
        "pool_scale": pool_scale[:, None, :],
        "w_b": w_b.astype(BF16),
        "b_merge": b_merge[:, None, :],
        "w_out": w_out.astype(BF16),
        "final_g": final_norm_g[None, :],
    }


def _window_row():
    return np.repeat(np.asarray(POOL_WINDOWS, np.float32), POOL_GW)[None, :]


def kernel(x_prompt, x_sample, state_gla, state_pool, meta_tokens, norm_g, w_in, w_alpha, b_alpha,
           gla_gain, w_a, pool_w, pool_scale, w_b, b_merge, w_out, final_norm_g):
    depth = w_in.shape[0]
    dec_b, dec_t, _ = x_sample.shape
    wrow = _window_row()
    cnt_meta = jnp.asarray(np.minimum(wrow, np.arange(1, N_META + 1, dtype=np.float32)[:, None]))
    inv_w = jnp.asarray(1.0 / wrow)

    ws = _stacked_weights(norm_g, w_in, w_alpha, b_alpha, gla_gain, w_a, pool_w, pool_scale, w_b,
                          b_merge, w_out, final_norm_g)
    buf_s = jnp.swapaxes(state_pool, 1, 2)
    y_s, sg_s, bufn_s, s_meta, buf_meta = _step_layers(
        x_sample.reshape(dec_b * dec_t, D_MODEL), state_gla, buf_s, inv_w,
        meta_tokens.astype(F32), cnt_meta, ws, n_tok=dec_t)

    h_p = x_prompt
    sg_p, sp_p = [], []
    for l in range(depth):
        h_p, s_p, b_p = _seq_layer(h_p, s_meta, buf_meta, inv_w, ws, l, is_last=l == depth - 1)
        sg_p.append(s_p)
        sp_p.append(b_p[:, POOL_PAD - POOL_BUF:, :])
    return (h_p, y_s.reshape(dec_b, dec_t, D_MODEL), jnp.stack(sg_p), jnp.stack(sp_p),
            sg_s, jnp.swapaxes(bufn_s, 1, 2))
```

```python
import functools

import numpy as np
import jax
import jax.numpy as jnp
from jax import lax
from jax.experimental import pallas as pl
from jax.experimental.pallas import tpu as pltpu

F32 = jnp.float32
BF16 = jnp.bfloat16

D_MODEL = 1024
N_META = 16
HEADS = 4
DKH = 128
DVH = 256
DK = HEADS * DKH
DV = HEADS * DVH
GATE_RANK = 16
GATE_TAU = 16.0
POOL_W = 512
POOL_WINDOWS = (2, 4, 8, 16)
POOL_GW = POOL_W // len(POOL_WINDOWS)
POOL_BUF = max(POOL_WINDOWS) - 1
POOL_PAD = POOL_BUF + 1
EPS = 1e-6
IN_SIZES = (DK, DK, DV, DV, POOL_W, POOL_W, GATE_RANK, 2 * D_MODEL)
IN_OFFS = tuple(int(o) for o in np.cumsum((0,) + IN_SIZES))

LANES = 128
SUBLANES = 8
VMEM_LIMIT_BYTES = 56 * 1024 * 1024

MAX_FACTORED_LOG_DECAY = 60.0

SEQ_BLOCK = 512
SEQ_CHUNK = 256
SAMPLE_BLOCK = 32
SAMPLE_SUB = 8
PREP_COLS = 512


def _mm(a, b):
    return jnp.dot(a, b, preferred_element_type=F32)


def _mm_nt(a, b):
    return lax.dot_general(a, b, (((1,), (1,)), ((), ())), preferred_element_type=F32)


def _mm_tn(a, b):
    return lax.dot_general(a, b, (((0,), (0,)), ((), ())), preferred_element_type=F32)


def _sigmoid(x):
    return 1.0 / (1.0 + jnp.exp(-x))


def _silu(x):
    return x * _sigmoid(x)


def _log_sigmoid(x):
    return jnp.minimum(x, 0.0) - jnp.log1p(jnp.exp(-jnp.abs(x)))


def _rmsnorm(x, g):
    ms = jnp.mean(x * x, axis=-1, keepdims=True)
    return x * lax.rsqrt(ms + EPS) * g


def _row_to_col(row, eye):
    return jnp.sum(eye * row, axis=1, keepdims=True)


def _eye(n):
    r = lax.broadcasted_iota(jnp.int32, (n, n), 0)
    c = lax.broadcasted_iota(jnp.int32, (n, n), 1)
    return (r == c).astype(F32)


W_NAMES = ("vec", "w_main", "w_al", "w_mg", "w_alpha", "w_a", "pool_w", "w_b", "w_out", "final_g")
MAIN_SEGS = ("w_q", "w_k", "w_v", "w_ga", "w_u", "w_gb")
MAIN_COLS = IN_OFFS[len(MAIN_SEGS)]
VEC_SEGS = (("norm_g", D_MODEL), ("b_alpha", DK), ("gla_gain", DV), ("pool_scale", POOL_W),
            ("b_merge", 2 * D_MODEL))


def _weight_views(refs):
    w = dict(zip(W_NAMES, refs))
    for i, name in enumerate(MAIN_SEGS):
        w[name] = w["w_main"].at[:, IN_OFFS[i]:IN_OFFS[i + 1]]
    off = 0
    for name, width in VEC_SEGS:
        w[name] = w["vec"].at[:, off:off + width]
        off += width
    return w


def _gate_in(x, w):
    xn = _rmsnorm(x, w["norm_g"][...]).astype(BF16)
    alow = _mm(xn, w["w_al"][...])
    q = _mm(xn, w["w_q"][...]) * (DKH ** -0.5)
    k = _mm(xn, w["w_k"][...])
    a = _mm(alow.astype(BF16), w["w_alpha"][...]) + w["b_alpha"][...]
    v = _mm(xn, w["w_v"][...])
    u = _mm(xn, w["w_u"][...])
    return xn, q, k, v, u, a


def _log_gate(a):
    return _log_sigmoid(a) * (1.0 / GATE_TAU)


def _dense_out(x, xn, o, z, w, is_last, pre=None):
    if pre is None:
        pre = (_mm(xn, w["w_ga"][...]), _mm(xn, w["w_gb"][...]), _mm(xn, w["w_mg"][...]))
    ga, gb, mg = pre
    gain = w["gla_gain"][...]
    parts = []
    for h in range(HEADS):
        oh = o[:, h * DVH:(h + 1) * DVH]
        ms = jnp.mean(oh * oh, axis=-1, keepdims=True)
        parts.append(oh * lax.rsqrt(ms + EPS) * gain[:, h * DVH:(h + 1) * DVH])
    on = jnp.concatenate(parts, axis=-1) * _silu(ga)
    ya = _mm(on.astype(BF16), w["w_a"][...])

    yp = []
    for g in range(len(POOL_WINDOWS)):
        zg = z[:, g * POOL_GW:(g + 1) * POOL_GW].astype(BF16)
        yp.append(_mm(zg, w["pool_w"][g]))
    y_pool = jnp.concatenate(yp, axis=-1) * w["pool_scale"][...]
    yb = _mm((y_pool * _silu(gb)).astype(BF16), w["w_b"][...])

    gates = _sigmoid(mg + w["b_merge"][...])
    merged = gates[:, :D_MODEL] * ya + gates[:, D_MODEL:] * yb
    h_new = x + _mm(merged.astype(BF16), w["w_out"][...])
    y = _rmsnorm(h_new, w["final_g"][...]) if is_last else None
    return h_new, y


def _window_sums(e):
    p1 = e + pltpu.roll(e, 1, 0)
    t = p1[:, POOL_GW:]
    p2 = t + pltpu.roll(t, 2, 0)
    t = p2[:, POOL_GW:]
    p3 = t + pltpu.roll(t, 4, 0)
    t = p3[:, POOL_GW:]
    p4 = t + pltpu.roll(t, 8, 0)
    return jnp.concatenate([p1[:, :POOL_GW], p2[:, :POOL_GW], p3[:, :POOL_GW], p4], axis=-1)


def _split_hi_lo(x):
    hi = x.astype(BF16)
    lo = (x - hi.astype(F32)).astype(BF16)
    return hi, lo


def _cumsum_rows(x, n):
    pos = lax.broadcasted_iota(jnp.int32, (x.shape[0], 1), 0) % n
    step = 1
    while step < n:
        x = x + jnp.where(pos >= step, pltpu.roll(x, step, 0), 0.0)
        step *= 2
    return x


def _seq_kernel(*refs, tb, chunk, is_last):
    n_w = len(W_NAMES)
    h_ref, s0_ref, buf0_ref, cnt_ref = refs[:4]
    w = _weight_views(refs[4:4 + n_w])
    hy_ref, s_ref, bufn_ref, o_s, ext_s, oc_s, b_s, k_s = refs[4 + n_w:]

    t = pl.program_id(1)

    @pl.when(t == 0)
    def _():
        s_ref[0] = s0_ref[0]
        ext_s[0:POOL_PAD, :] = buf0_ref[0]

    x = h_ref[0]
    xn, q, k, v, u, a_gate = _gate_in(x, w)
    n_chunks = tb // chunk
    chunks = [(c, h) for c in range(n_chunks) for h in range(HEADS)]

    def sl(c, h):
        return (slice(c * chunk, (c + 1) * chunk), slice(h * DKH, (h + 1) * DKH),
                slice(h * DVH, (h + 1) * DVH))

    neg = jnp.minimum(a_gate, 0.0)
    steepest = jnp.sum(neg[0:chunk], axis=0, keepdims=True)
    for c in range(1, n_chunks):
        steepest = jnp.minimum(
            steepest, jnp.sum(neg[c * chunk:(c + 1) * chunk], axis=0, keepdims=True))
    decay_bound = (jnp.min(steepest) - chunk * np.log(2.0)) * (1.0 / GATE_TAU)
    factored_ok = decay_bound >= -MAX_FACTORED_LOG_DECAY
    la = _log_gate(a_gate)
    vb = v.astype(BF16)
    causal = (lax.broadcasted_iota(jnp.int32, (chunk, chunk), 0)
              >= lax.broadcasted_iota(jnp.int32, (chunk, chunk), 1))
    tril = causal.astype(BF16)
    eye = _eye(DKH)
    la_hi, la_lo = _split_hi_lo(la)
    b = jnp.concatenate(
        [_mm(tril, la_hi[c * chunk:(c + 1) * chunk])
         + _mm(tril, la_lo[c * chunk:(c + 1) * chunk]) for c in range(n_chunks)], axis=0)
    ga = _mm(xn, w["w_ga"][...])
    b_ends = [b[(c + 1) * chunk - 1:(c + 1) * chunk, :] for c in range(n_chunks)]
    to_end = jnp.concatenate(
        [b_ends[c] - b[c * chunk:(c + 1) * chunk, :] for c in range(n_chunks)], axis=0)
    qd = (q * jnp.exp(b)).astype(BF16)
    kd = (k * jnp.exp(to_end)).astype(BF16)

    kinv = (k * jnp.exp(-b)).astype(BF16)
    a_m = {}
    for c, h in chunks:
        rows, ks, _ = sl(c, h)
        a = _mm_nt(qd[rows, ks], kinv[rows, ks])
        a_m[c, h] = jnp.where(causal, a, 0.0).astype(BF16)
    gb = _mm(xn, w["w_gb"][...])
    o_in, d_s, dcol = {}, {}, {}
    for c, h in chunks:
        rows, ks, vs = sl(c, h)
        o_in[c, h] = _mm(a_m[c, h], vb[rows, vs])
        d_s[c, h] = _mm_tn(kd[rows, ks], vb[rows, vs])
        dcol[c, h] = _row_to_col(jnp.exp(b_ends[c])[:, ks], eye)
    mg = _mm(xn, w["w_mg"][...])

    s_cur = [s_ref[0, h] for h in range(HEADS)]
    for c, h in chunks:
        rows, ks, vs = sl(c, h)
        o_state = _mm(qd[rows, ks], s_cur[h].astype(BF16))
        oc_s[rows, vs] = o_state
        o_s[rows, vs] = o_in[c, h] + o_state
        s_cur[h] = dcol[c, h] * s_cur[h] + d_s[c, h]
    for h in range(HEADS):
        s_ref[0, h] = s_cur[h]

    ext_s[POOL_PAD:POOL_PAD + tb, :] = u
    z = _window_sums(ext_s[...])[POOL_PAD:, :] * cnt_ref[...] - u
    bufn_ref[0] = ext_s[tb:tb + POOL_PAD, :]
    ext_s[0:POOL_PAD, :] = ext_s[tb:tb + POOL_PAD, :]

    h_new, y = _dense_out(x, xn, o_s[...], z, w, is_last, pre=(ga, gb, mg))
    hy_ref[0] = y if is_last else h_new

    @pl.when(jnp.logical_not(factored_ok))
    def _():
        b_s[...] = b
        k_s[...] = k
        col_id = lax.broadcasted_iota(jnp.int32, (chunk, chunk), 1)
        for c, h in chunks:
            rows, ks, vs = sl(c, h)
            qh = q[rows, ks]
            bh = b[rows, ks]

            def columns(g, acc, c=c, ks=ks, qh=qh, bh=bh):
                r0 = pl.multiple_of(c * chunk + g * SUBLANES, SUBLANES)
                b_tile = b_s[pl.ds(r0, SUBLANES), ks]
                k_tile = k_s[pl.ds(r0, SUBLANES), ks]
                for r in range(SUBLANES):
                    decay = jnp.exp(jnp.minimum(bh - b_tile[r:r + 1, :], 0.0))
                    col = jnp.sum(qh * k_tile[r:r + 1, :] * decay, axis=1, keepdims=True)
                    acc = jnp.where(col_id == g * SUBLANES + r, col, acc)
                return acc

            a = lax.fori_loop(0, chunk // SUBLANES, columns, jnp.zeros((chunk, chunk), F32))
            a = jnp.where(causal, a, 0.0).astype(BF16)
            o_s[rows, vs] = oc_s[rows, vs] + _mm(a, vb[rows, vs])
        h_fix, y_fix = _dense_out(x, xn, o_s[...], z, w, is_last, pre=(ga, gb, mg))
        hy_ref[0] = y_fix if is_last else h_fix


def _const_spec(shape):
    nd = len(shape)
    return pl.BlockSpec(shape, lambda *_: (0,) * nd, pipeline_mode=pl.Buffered(1))


def _weight_specs(ws, layer_of):
    specs = []
    for name in W_NAMES:
        shape = ws[name].shape
        if name == "final_g":
            specs.append(_const_spec(shape))
        else:
            zeros = (0,) * (len(shape) - 1)
            specs.append(pl.BlockSpec((None,) + shape[1:],
                                      lambda *g, zeros=zeros: (layer_of(*g),) + zeros,
                                      pipeline_mode=pl.Buffered(1)))
    return specs


def _seq_layer(h, s_meta, buf_meta, inv_cnt, ws, layer, *, is_last):
    bn, t_len, _ = h.shape
    tb = min(SEQ_BLOCK, t_len)
    chunk = min(SEQ_CHUNK, tb)
    assert t_len % tb == 0 and tb % chunk == 0 and buf_meta.shape[1] == POOL_PAD
    weights = [ws[n] for n in W_NAMES]
    in_specs = [
        pl.BlockSpec((1, tb, D_MODEL), lambda b, t: (b, t, 0)),
        pl.BlockSpec((1, HEADS, DKH, DVH), lambda b, t: (layer, 0, 0, 0)),
        pl.BlockSpec((1, POOL_PAD, POOL_W), lambda b, t: (layer, 0, 0)),
        _const_spec(inv_cnt.shape),
    ] + _weight_specs(ws, lambda b, t: layer)
    out_shape = [jax.ShapeDtypeStruct(h.shape, F32),
                 jax.ShapeDtypeStruct((bn, HEADS, DKH, DVH), F32),
                 jax.ShapeDtypeStruct((bn, POOL_PAD, POOL_W), F32)]
    out_specs = [pl.BlockSpec((1, tb, D_MODEL), lambda b, t: (b, t, 0)),
                 pl.BlockSpec((1, HEADS, DKH, DVH), lambda b, t: (b, 0, 0, 0)),
                 pl.BlockSpec((1, POOL_PAD, POOL_W), lambda b, t: (b, 0, 0))]
    kern = functools.partial(_seq_kernel, tb=tb, chunk=chunk, is_last=is_last)
    return pl.pallas_call(
        kern,
        grid=(bn, t_len // tb),
        in_specs=in_specs,
        out_specs=out_specs,
        out_shape=out_shape,
        scratch_shapes=[pltpu.VMEM((tb, DV), F32), pltpu.VMEM((tb + POOL_PAD, POOL_W), F32),
                        pltpu.VMEM((tb, DV), F32),
                        pltpu.VMEM((tb, DK), F32), pltpu.VMEM((tb, DK), F32)],
        compiler_params=pltpu.CompilerParams(
            dimension_semantics=("arbitrary", "arbitrary"), vmem_limit_bytes=VMEM_LIMIT_BYTES),
        name="seq_layer",
    )(h, s_meta, buf_meta, inv_cnt, *weights)


def _intra_by_distance(q, k, v, b, tok, n_tok):
    o_parts = []
    for h in range(HEADS):
        ks = slice(h * DKH, (h + 1) * DKH)
        vs = slice(h * DVH, (h + 1) * DVH)
        o = jnp.sum(q[:, ks] * k[:, ks], axis=1, keepdims=True) * v[:, vs]
        for d in range(1, n_tok):
            kj = pltpu.roll(k[:, ks], d, 0)
            bj = pltpu.roll(b[:, ks], d, 0)
            score = jnp.sum(q[:, ks] * kj * jnp.exp(jnp.minimum(b[:, ks] - bj, 0.0)), axis=1,
                            keepdims=True)
            o = o + jnp.where(tok >= d, score, 0.0) * pltpu.roll(v[:, vs], d, 0)
        o_parts.append(o)
    return o_parts


def _meta_layer(xm, cnt, w):
    n = xm.shape[0]
    xn, q, k, v, u, a_gate = _gate_in(xm, w)
    b = _cumsum_rows(_log_gate(a_gate), n)
    tok = lax.broadcasted_iota(jnp.int32, (n, 1), 0)
    o = jnp.concatenate(_intra_by_distance(q, k, v, b, tok, n), axis=-1)
    kd = (k * jnp.exp(b[n - 1:n, :] - b)).astype(BF16)
    vb = v.astype(BF16)
    state = [_mm_tn(kd[:, h * DKH:(h + 1) * DKH], vb[:, h * DVH:(h + 1) * DVH])
             for h in range(HEADS)]
    ext = jnp.concatenate([jnp.zeros((POOL_PAD, POOL_W), F32), u], axis=0)
    z = _window_sums(ext)[POOL_PAD:, :] / cnt - u
    h_new, _ = _dense_out(xm, xn, o, z, w, False)
    return h_new, state, u
def _step_kernel(*refs, n_tok, sub, depth):
    n_w = len(W_NAMES)
    n_in = 6
    x_ref, s0_ref, buf0_ref, cnt_ref, meta_ref, cntm_ref = refs[:n_in]
    w = _weight_views(refs[n_in:n_in + n_w])
    y_ref, s_ref, bufn_ref, smeta_ref, bufmeta_ref = refs[n_in + n_w:n_in + n_w + 5]
    h_s, hm_s, xin_s, xn_s, q_s, k_s, v_s, la_s, u_s, o_s, z_s = refs[n_in + n_w + 5:]

    layer = pl.program_id(0)
    blk = pl.program_id(1)
    j = pl.program_id(2)
    nj = pl.num_programs(2)
    brows = xin_s.shape[0]
    blk_rows = pl.ds(pl.multiple_of(blk * brows, brows), brows)
    assert sub % SUBLANES == 0

    @pl.when((blk == 0) & (j == 0))
    def _():
        @pl.when(layer == 0)
        def _():
            hm_s[...] = meta_ref[...]

        hm_new, state, u_meta = _meta_layer(hm_s[...], cntm_ref[...], w)
        hm_s[...] = hm_new
        for h in range(HEADS):
            smeta_ref[h] = state[h]
        bufmeta_ref[...] = u_meta

    @pl.when(j == 0)
    def _():
        @pl.when(layer == 0)
        def _():
            xin_s[...] = x_ref[...]

        @pl.when(layer > 0)
        def _():
            xin_s[...] = h_s[blk_rows, :]

        xn, q, k, v, u, a_gate = _gate_in(xin_s[...], w)
        la = _log_gate(a_gate)
        xn_s[...] = xn
        q_s[...] = q
        k_s[...] = k
        v_s[...] = v
        la_s[...] = la
        for g in range(len(POOL_WINDOWS)):
            u_s[g] = u[:, g * POOL_GW:(g + 1) * POOL_GW]

    nr = sub * n_tok
    rows = pl.ds(pl.multiple_of(j * nr, nr), nr)
    row = lax.broadcasted_iota(jnp.int32, (nr, 1), 0)
    tok = row % n_tok
    seq = row // n_tok
    eye = _eye(DKH)
    b = _cumsum_rows(la_s[rows, :], n_tok)
    b_end = b
    for d in range(1, n_tok):
        b_end = jnp.where(tok == n_tok - 1 - d, pltpu.roll(b, nr - d, 0), b_end)
    q = q_s[rows, :]
    k = k_s[rows, :]
    v = v_s[rows, :]
    qd = (q * jnp.exp(b)).astype(BF16)
    kd = k * jnp.exp(b_end - b)
    dec = jnp.exp(b_end)
    vb = v.astype(BF16)
    o_parts = _intra_by_distance(q, k, v, b, tok, n_tok)
    for h in range(HEADS):
        ks = slice(h * DKH, (h + 1) * DKH)
        vs = slice(h * DVH, (h + 1) * DVH)
        o = o_parts[h]
        for s in range(sub):
            mine = seq == s
            s_old = s0_ref[s, h]
            o = o + jnp.where(mine, _mm(qd[:, ks], s_old.astype(BF16)), 0.0)
            kd_s = jnp.where(mine, kd[:, ks], 0.0).astype(BF16)
            dcol = _row_to_col(dec[s * n_tok:s * n_tok + 1, ks], eye)
            s_ref[s, h] = dcol * s_old + _mm_tn(kd_s, vb[:, vs])
        o_parts[h] = o
    o_s[rows, :] = jnp.concatenate(o_parts, axis=-1)

    inv_cnt = cnt_ref[...]
    base = pl.multiple_of(j * nr, nr)
    hist = [buf0_ref[i] for i in range(POOL_BUF)]
    new = []
    for t in range(n_tok):
        rows_t = pl.ds(base + t, sub, stride=n_tok)
        new.append(jnp.concatenate([u_s[g, rows_t, :] for g in range(len(POOL_WINDOWS))], axis=-1))
    ext = hist + new
    for t in range(n_tok):
        last = POOL_BUF + t
        rows_t = pl.ds(base + t, sub, stride=n_tok)
        for g, width in enumerate(POOL_WINDOWS):
            cols = slice(g * POOL_GW, (g + 1) * POOL_GW)
            acc = ext[last][:, cols]
            for back in range(1, width):
                acc = acc + ext[last - back][:, cols]
            z_s[g, rows_t, :] = acc * inv_cnt[:, cols] - new[t][:, cols]
    for i in range(POOL_BUF):
        bufn_ref[i] = ext[i + n_tok]

    @pl.when(j == nj - 1)
    def _():
        z = jnp.concatenate([z_s[g] for g in range(len(POOL_WINDOWS))], axis=-1)
        h_new, y = _dense_out(xin_s[...], xn_s[...], o_s[...], z, w, True)

        @pl.when(layer < depth - 1)
        def _():
            h_s[blk_rows, :] = h_new

        @pl.when(layer == depth - 1)
        def _():
            y_ref[...] = y


def _step_layers(x, s_all, buf_all, cnt, meta, cnt_meta, ws, *, n_tok):
    rows, _ = x.shape
    depth, bn = s_all.shape[:2]
    assert buf_all.shape == (depth, POOL_BUF, bn, POOL_W)
    sub, blk = SAMPLE_SUB, SAMPLE_BLOCK
    assert rows == bn * n_tok and bn % blk == 0 and blk % sub == 0
    n_blk, n_inner = bn // blk, blk // sub
    brows = blk * n_tok
    weights = [ws[n] for n in W_NAMES]

    def state_idx(l, i, j):
        return (l, i * n_inner + j, 0, 0, 0)

    def buf_idx(l, i, j):
        return (l, 0, i * n_inner + j, 0)

    in_specs = [
        pl.BlockSpec((brows, D_MODEL), lambda l, i, j: (i, 0)),
        pl.BlockSpec((None, sub, HEADS, DKH, DVH), state_idx),
        pl.BlockSpec((None, POOL_BUF, sub, POOL_W), buf_idx),
        _const_spec(cnt.shape),
        _const_spec(meta.shape),
        _const_spec(cnt_meta.shape),
    ] + _weight_specs(ws, lambda l, i, j: l)
    n_meta = meta.shape[0]
    out_shape = [jax.ShapeDtypeStruct(x.shape, F32),
                 jax.ShapeDtypeStruct(s_all.shape, F32),
                 jax.ShapeDtypeStruct(buf_all.shape, F32),
                 jax.ShapeDtypeStruct((depth, HEADS, DKH, DVH), F32),
                 jax.ShapeDtypeStruct((depth, n_meta, POOL_W), F32)]
    out_specs = [pl.BlockSpec((brows, D_MODEL),
                              lambda l, i, j: (jnp.where(l == depth - 1, i, 0), 0)),
                 pl.BlockSpec((None, sub, HEADS, DKH, DVH), state_idx),
                 pl.BlockSpec((None, POOL_BUF, sub, POOL_W), buf_idx),
                 pl.BlockSpec((None, HEADS, DKH, DVH), lambda l, i, j: (l, 0, 0, 0)),
                 pl.BlockSpec((None, n_meta, POOL_W), lambda l, i, j: (l, 0, 0))]
    kern = functools.partial(_step_kernel, n_tok=n_tok, sub=sub, depth=depth)
    scratch = [pltpu.VMEM((rows, D_MODEL), F32), pltpu.VMEM((n_meta, D_MODEL), F32),
               pltpu.VMEM((brows, D_MODEL), F32),
               pltpu.VMEM((brows, D_MODEL), BF16), pltpu.VMEM((brows, DK), F32),
               pltpu.VMEM((brows, DK), F32), pltpu.VMEM((brows, DV), F32),
               pltpu.VMEM((brows, DK), F32), pltpu.VMEM((len(POOL_WINDOWS), brows, POOL_GW), F32),
               pltpu.VMEM((brows, DV), F32), pltpu.VMEM((len(POOL_WINDOWS), brows, POOL_GW), F32)]
    return pl.pallas_call(
        kern,
        grid=(depth, n_blk, n_inner),
        in_specs=in_specs,
        out_specs=out_specs,
        out_shape=out_shape,
        scratch_shapes=scratch,
        compiler_params=pltpu.CompilerParams(
            dimension_semantics=("arbitrary", "arbitrary", "arbitrary"),
            vmem_limit_bytes=VMEM_LIMIT_BYTES),
        name="step_layers",
    )(x, s_all, buf_all, cnt, meta, cnt_meta, *weights)


def _prep_main_kernel(wt_ref, main_ref):
    main_ref[...] = wt_ref[...].T.astype(BF16)


def _prep_tail_kernel(wt_ref, al_ref, mg_ref):
    lane = lax.broadcasted_iota(jnp.int32, al_ref.shape, 1)
    al_ref[...] = jnp.where(lane < GATE_RANK, wt_ref[0, 0:LANES, :].T, 0.0).astype(BF16)
    mg_ref[...] = wt_ref[0, GATE_RANK:, :].T.astype(BF16)


def _prep_w_in(w_in):
    depth, rows, cols = w_in.shape
    n_tail = cols - MAIN_COLS
    wt = jnp.swapaxes(w_in, 1, 2)
    cb = PREP_COLS
    assert MAIN_COLS % cb == 0
    w_main = pl.pallas_call(
        _prep_main_kernel,
        grid=(depth, MAIN_COLS // cb),
        in_specs=[pl.BlockSpec((None, cb, rows), lambda l, i: (l, i, 0))],
        out_specs=pl.BlockSpec((None, rows, cb), lambda l, i: (l, 0, i)),
        out_shape=jax.ShapeDtypeStruct((depth, rows, MAIN_COLS), BF16),
        name="prep_w_main",
    )(wt)
    w_al, w_mg = pl.pallas_call(
        _prep_tail_kernel,
        grid=(depth,),
        in_specs=[pl.BlockSpec((pl.Element(1), pl.Element(n_tail), pl.Element(rows)),
                               lambda l: (l, MAIN_COLS, 0))],
        out_specs=[pl.BlockSpec((None, rows, LANES), lambda l: (l, 0, 0)),
                   pl.BlockSpec((None, rows, n_tail - GATE_RANK), lambda l: (l, 0, 0))],
        out_shape=[jax.ShapeDtypeStruct((depth, rows, LANES), BF16),
                   jax.ShapeDtypeStruct((depth, rows, n_tail - GATE_RANK), BF16)],
        compiler_params=pltpu.CompilerParams(vmem_limit_bytes=VMEM_LIMIT_BYTES),
        name="prep_w_tail",
    )(wt)
    return w_main, w_al, w_mg


def _stacked_weights(norm_g, w_in, w_alpha, b_alpha, gla_gain, w_a, pool_w, pool_scale, w_b,
                     b_merge, w_out, final_norm_g):
    w_main, w_al, w_mg = _prep_w_in(w_in)
    pad = LANES - GATE_RANK
    vectors = {"norm_g": norm_g, "b_alpha": b_alpha, "gla_gain": gla_gain,
               "pool_scale": pool_scale, "b_merge": b_merge}
    assert all(vectors[name].shape[-1] == width for name, width in VEC_SEGS)
    return {
        "vec": jnp.concatenate([vectors[name] for name, _ in VEC_SEGS], axis=-1)[:, None, :],
        "w_main": w_main, "w_al": w_al, "w_mg": w_mg,
        "w_alpha": jnp.pad(w_alpha.astype(BF16), ((0, 0), (0, pad), (0, 0))),
        "w_a": w_a.astype(BF16),
        "pool_w": pool_w.astype(BF16),
        "w_b": w_b.astype(BF16),
        "w_out": w_out.astype(BF16),
        "final_g": final_norm_g[None, :],
    }


def _window_row():
    return np.repeat(np.asarray(POOL_WINDOWS, np.float32), POOL_GW)[None, :]


def kernel(x_prompt, x_sample, state_gla, state_pool, meta_tokens, norm_g, w_in, w_alpha, b_alpha,
           gla_gain, w_a, pool_w, pool_scale, w_b, b_merge, w_out, final_norm_g):
    depth = w_in.shape[0]
    dec_b, dec_t, _ = x_sample.shape
    wrow = _window_row()
    cnt_meta = jnp.asarray(np.minimum(wrow, np.arange(1, N_META + 1, dtype=np.float32)[:, None]))
    inv_w = jnp.asarray(1.0 / wrow)

    ws = _stacked_weights(norm_g, w_in, w_alpha, b_alpha, gla_gain, w_a, pool_w, pool_scale, w_b,
                          b_merge, w_out, final_norm_g)
    buf_s = jnp.swapaxes(state_pool, 1, 2)
    y_s, sg_s, bufn_s, s_meta, buf_meta = _step_layers(
        x_sample.reshape(dec_b * dec_t, D_MODEL), state_gla, buf_s, inv_w,
        meta_tokens.astype(F32), cnt_meta, ws, n_tok=dec_t)

    h_p = x_prompt
    sg_p, sp_p = [], []
    for l in range(depth):
        h_p, s_p, b_p = _seq_layer(h_p, s_meta, buf_meta, inv_w, ws, l, is_last=l == depth - 1)
        sg_p.append(s_p)
        sp_p.append(b_p[:, POOL_PAD - POOL_BUF:, :])
    return (h_p, y_s.reshape(dec_b, dec_t, D_MODEL), jnp.stack(sg_p), jnp.stack(sp_p),
            sg_s, jnp.swapaxes(bufn_s, 1, 2))
```

```python
import functools

import numpy as np
import jax
import jax.numpy as jnp
from jax import lax
from jax.experimental import pallas as pl
from jax.experimental.pallas import tpu as pltpu

F32 = jnp.float32
BF16 = jnp.bfloat16

D_MODEL = 1024
N_META = 16
HEADS = 4
DKH = 128
DVH = 256
DK = HEADS * DKH
DV = HEADS * DVH
GATE_RANK = 16
GATE_TAU = 16.0
POOL_W = 512
POOL_WINDOWS = (2, 4, 8, 16)
POOL_GW = POOL_W // len(POOL_WINDOWS)
POOL_BUF = max(POOL_WINDOWS) - 1
POOL_PAD = POOL_BUF + 1
EPS = 1e-6
IN_SIZES = (DK, DK, DV, DV, POOL_W, POOL_W, GATE_RANK, 2 * D_MODEL)
IN_OFFS = tuple(int(o) for o in np.cumsum((0,) + IN_SIZES))

LANES = 128
SUBLANES = 8
VMEM_LIMIT_BYTES = 56 * 1024 * 1024

MAX_FACTORED_LOG_DECAY = 60.0

SEQ_BLOCK = 512
SEQ_CHUNK = 256
SAMPLE_BLOCK = 32
SAMPLE_SUB = 8
PREP_COLS = 512


def _mm(a, b):
    return jnp.dot(a, b, preferred_element_type=F32)


def _mm_nt(a, b):
    return lax.dot_general(a, b, (((1,), (1,)), ((), ())), preferred_element_type=F32)


def _mm_tn(a, b):
    return lax.dot_general(a, b, (((0,), (0,)), ((), ())), preferred_element_type=F32)


def _sigmoid(x):
    return 1.0 / (1.0 + jnp.exp(-x))


def _silu(x):
    return x * _sigmoid(x)


def _log_sigmoid(x):
    return jnp.minimum(x, 0.0) - jnp.log1p(jnp.exp(-jnp.abs(x)))


def _rmsnorm(x, g):
    ms = jnp.mean(x * x, axis=-1, keepdims=True)
    return x * lax.rsqrt(ms + EPS) * g


def _row_to_col(row, eye):
    return jnp.sum(eye * row, axis=1, keepdims=True)


def _eye(n):
    r = lax.broadcasted_iota(jnp.int32, (n, n), 0)
    c = lax.broadcasted_iota(jnp.int32, (n, n), 1)
    return (r == c).astype(F32)


W_NAMES = ("vec", "w_main", "w_al", "w_mg", "w_alpha", "w_a", "pool_w", "w_b", "w_out", "final_g")
MAIN_SEGS = ("w_q", "w_k", "w_v", "w_ga", "w_u", "w_gb")
MAIN_COLS = IN_OFFS[len(MAIN_SEGS)]
VEC_SEGS = (("norm_g", D_MODEL), ("b_alpha", DK), ("gla_gain", DV), ("pool_scale", POOL_W),
            ("b_merge", 2 * D_MODEL))


def _weight_views(refs):
    w = dict(zip(W_NAMES, refs))
    for i, name in enumerate(MAIN_SEGS):
        w[name] = w["w_main"].at[:, IN_OFFS[i]:IN_OFFS[i + 1]]
    off = 0
    for name, width in VEC_SEGS:
        w[name] = w["vec"].at[:, off:off + width]
        off += width
    return w


def _gate_in(x, w):
    xn = _rmsnorm(x, w["norm_g"][...]).astype(BF16)
    alow = _mm(xn, w["w_al"][...])
    q = _mm(xn, w["w_q"][...]) * (DKH ** -0.5)
    k = _mm(xn, w["w_k"][...])
    a = _mm(alow.astype(BF16), w["w_alpha"][...]) + w["b_alpha"][...]
    v = _mm(xn, w["w_v"][...])
    u = _mm(xn, w["w_u"][...])
    return xn, q, k, v, u, a


def _log_gate(a):
    return _log_sigmoid(a) * (1.0 / GATE_TAU)


def _dense_out(x, xn, o, z, w, is_last, pre=None):
    if pre is None:
        pre = (_mm(xn, w["w_ga"][...]), _mm(xn, w["w_gb"][...]), _mm(xn, w["w_mg"][...]))
    ga, gb, mg = pre
    gain = w["gla_gain"][...]
    parts = []
    for h in range(HEADS):
        oh = o[:, h * DVH:(h + 1) * DVH]
        ms = jnp.mean(oh * oh, axis=-1, keepdims=True)
        parts.append(oh * lax.rsqrt(ms + EPS) * gain[:, h * DVH:(h + 1) * DVH])
    on = jnp.concatenate(parts, axis=-1) * _silu(ga)
    ya = _mm(on.astype(BF16), w["w_a"][...])

    yp = []
    for g in range(len(POOL_WINDOWS)):
        zg = z[:, g * POOL_GW:(g + 1) * POOL_GW].astype(BF16)
        yp.append(_mm(zg, w["pool_w"][g]))
    y_pool = jnp.concatenate(yp, axis=-1) * w["pool_scale"][...]
    yb = _mm((y_pool * _silu(gb)).astype(BF16), w["w_b"][...])

    gates = _sigmoid(mg + w["b_merge"][...])
    merged = gates[:, :D_MODEL] * ya + gates[:, D_MODEL:] * yb
    h_new = x + _mm(merged.astype(BF16), w["w_out"][...])
    y = _rmsnorm(h_new, w["final_g"][...]) if is_last else None
    return h_new, y


def _window_sums(e):
    p1 = e + pltpu.roll(e, 1, 0)
    t = p1[:, POOL_GW:]
    p2 = t + pltpu.roll(t, 2, 0)
    t = p2[:, POOL_GW:]
    p3 = t + pltpu.roll(t, 4, 0)
    t = p3[:, POOL_GW:]
    p4 = t + pltpu.roll(t, 8, 0)
    return jnp.concatenate([p1[:, :POOL_GW], p2[:, :POOL_GW], p3[:, :POOL_GW], p4], axis=-1)


def _split_hi_lo(x):
    hi = x.astype(BF16)
    lo = (x - hi.astype(F32)).astype(BF16)
    return hi, lo


def _cumsum_rows(x, n):
    pos = lax.broadcasted_iota(jnp.int32, (x.shape[0], 1), 0) % n
    step = 1
    while step < n:
        x = x + jnp.where(pos >= step, pltpu.roll(x, step, 0), 0.0)
        step *= 2
    return x


def _seq_kernel(*refs, tb, chunk, is_last):
    n_w = len(W_NAMES)
    h_ref, s0_ref, buf0_ref, cnt_ref = refs[:4]
    w = _weight_views(refs[4:4 + n_w])
    hy_ref, s_ref, bufn_ref, o_s, ext_s, oc_s, b_s, k_s = refs[4 + n_w:]

    t = pl.program_id(1)

    @pl.when(t == 0)
    def _():
        s_ref[0] = s0_ref[0]
        ext_s[0:POOL_PAD, :] = buf0_ref[0]

    x = h_ref[0]
    xn, q, k, v, u, a_gate = _gate_in(x, w)
    n_chunks = tb // chunk
    chunks = [(c, h) for c in range(n_chunks) for h in range(HEADS)]

    def sl(c, h):
        return (slice(c * chunk, (c + 1) * chunk), slice(h * DKH, (h + 1) * DKH),
                slice(h * DVH, (h + 1) * DVH))

    neg = jnp.minimum(a_gate, 0.0)
    steepest = jnp.sum(neg[0:chunk], axis=0, keepdims=True)
    for c in range(1, n_chunks):
        steepest = jnp.minimum(
            steepest, jnp.sum(neg[c * chunk:(c + 1) * chunk], axis=0, keepdims=True))
    decay_bound = (jnp.min(steepest) - chunk * np.log(2.0)) * (1.0 / GATE_TAU)
    factored_ok = decay_bound >= -MAX_FACTORED_LOG_DECAY
    la = _log_gate(a_gate)
    vb = v.astype(BF16)
    causal = (lax.broadcasted_iota(jnp.int32, (chunk, chunk), 0)
              >= lax.broadcasted_iota(jnp.int32, (chunk, chunk), 1))
    tril = causal.astype(BF16)
    eye = _eye(DKH)
    la_hi, la_lo = _split_hi_lo(la)
    b = jnp.concatenate(
        [_mm(tril, la_hi[c * chunk:(c + 1) * chunk])
         + _mm(tril, la_lo[c * chunk:(c + 1) * chunk]) for c in range(n_chunks)], axis=0)
    ga = _mm(xn, w["w_ga"][...])
    b_ends = [b[(c + 1) * chunk - 1:(c + 1) * chunk, :] for c in range(n_chunks)]
    to_end = jnp.concatenate(
        [b_ends[c] - b[c * chunk:(c + 1) * chunk, :] for c in range(n_chunks)], axis=0)
    qd = (q * jnp.exp(b)).astype(BF16)
    kd = (k * jnp.exp(to_end)).astype(BF16)

    kinv = (k * jnp.exp(-b)).astype(BF16)
    a_m = {}
    for c, h in chunks:
        rows, ks, _ = sl(c, h)
        a = _mm_nt(qd[rows, ks], kinv[rows, ks])
        a_m[c, h] = jnp.where(causal, a, 0.0).astype(BF16)
    gb = _mm(xn, w["w_gb"][...])
    o_in, d_s, dcol = {}, {}, {}
    for c, h in chunks:
        rows, ks, vs = sl(c, h)
        o_in[c, h] = _mm(a_m[c, h], vb[rows, vs])
        d_s[c, h] = _mm_tn(kd[rows, ks], vb[rows, vs])
        dcol[c, h] = _row_to_col(jnp.exp(b_ends[c])[:, ks], eye)
    mg = _mm(xn, w["w_mg"][...])

    s_cur = [s_ref[0, h] for h in range(HEADS)]
    for c, h in chunks:
        rows, ks, vs = sl(c, h)
        o_state = _mm(qd[rows, ks], s_cur[h].astype(BF16))
        oc_s[rows, vs] = o_state
        o_s[rows, vs] = o_in[c, h] + o_state
        s_cur[h] = dcol[c, h] * s_cur[h] + d_s[c, h]
    for h in range(HEADS):
        s_ref[0, h] = s_cur[h]

    ext_s[POOL_PAD:POOL_PAD + tb, :] = u
    z = _window_sums(ext_s[...])[POOL_PAD:, :] * cnt_ref[...] - u
    bufn_ref[0] = ext_s[tb:tb + POOL_PAD, :]
    ext_s[0:POOL_PAD, :] = ext_s[tb:tb + POOL_PAD, :]

    h_new, y = _dense_out(x, xn, o_s[...], z, w, is_last, pre=(ga, gb, mg))
    hy_ref[0] = y if is_last else h_new

    @pl.when(jnp.logical_not(factored_ok))
    def _():
        b_s[...] = b
        k_s[...] = k
        col_id = lax.broadcasted_iota(jnp.int32, (chunk, chunk), 1)
        for c, h in chunks:
            rows, ks, vs = sl(c, h)
            qh = q[rows, ks]
            bh = b[rows, ks]

            def columns(g, acc, c=c, ks=ks, qh=qh, bh=bh):
                r0 = pl.multiple_of(c * chunk + g * SUBLANES, SUBLANES)
                b_tile = b_s[pl.ds(r0, SUBLANES), ks]
                k_tile = k_s[pl.ds(r0, SUBLANES), ks]
                for r in range(SUBLANES):
                    decay = jnp.exp(jnp.minimum(bh - b_tile[r:r + 1, :], 0.0))
                    col = jnp.sum(qh * k_tile[r:r + 1, :] * decay, axis=1, keepdims=True)
                    acc = jnp.where(col_id == g * SUBLANES + r, col, acc)
                return acc

            a = lax.fori_loop(0, chunk // SUBLANES, columns, jnp.zeros((chunk, chunk), F32))
            a = jnp.where(causal, a, 0.0).astype(BF16)
            o_s[rows, vs] = oc_s[rows, vs] + _mm(a, vb[rows, vs])
        h_fix, y_fix = _dense_out(x, xn, o_s[...], z, w, is_last, pre=(ga, gb, mg))
        hy_ref[0] = y_fix if is_last else h_fix


def _const_spec(shape):
    nd = len(shape)
    return pl.BlockSpec(shape, lambda *_: (0,) * nd, pipeline_mode=pl.Buffered(1))


def _weight_specs(ws, layer_of):
    specs = []
    for name in W_NAMES:
        shape = ws[name].shape
        if name == "final_g":
            specs.append(_const_spec(shape))
        else:
            zeros = (0,) * (len(shape) - 1)
            specs.append(pl.BlockSpec((None,) + shape[1:],
                                      lambda *g, zeros=zeros: (layer_of(*g),) + zeros,
                                      pipeline_mode=pl.Buffered(1)))
    return specs


def _seq_layer(h, s_meta, buf_meta, inv_cnt, ws, layer, *, is_last):
    bn, t_len, _ = h.shape
    tb = min(SEQ_BLOCK, t_len)
    chunk = min(SEQ_CHUNK, tb)
    assert t_len % tb == 0 and tb % chunk == 0 and buf_meta.shape[1] == POOL_PAD
    weights = [ws[n] for n in W_NAMES]
    in_specs = [
        pl.BlockSpec((1, tb, D_MODEL), lambda b, t: (b, t, 0)),
        pl.BlockSpec((1, HEADS, DKH, DVH), lambda b, t: (layer, 0, 0, 0)),
        pl.BlockSpec((1, POOL_PAD, POOL_W), lambda b, t: (layer, 0, 0)),
        _const_spec(inv_cnt.shape),
    ] + _weight_specs(ws, lambda b, t: layer)
    out_shape = [jax.ShapeDtypeStruct(h.shape, F32),
                 jax.ShapeDtypeStruct((bn, HEADS, DKH, DVH), F32),
                 jax.ShapeDtypeStruct((bn, POOL_PAD, POOL_W), F32)]
    out_specs = [pl.BlockSpec((1, tb, D_MODEL), lambda b, t: (b, t, 0)),
                 pl.BlockSpec((1, HEADS, DKH, DVH), lambda b, t: (b, 0, 0, 0)),
                 pl.BlockSpec((1, POOL_PAD, POOL_W), lambda b, t: (b, 0, 0))]
    kern = functools.partial(_seq_kernel, tb=tb, chunk=chunk, is_last=is_last)
    return pl.pallas_call(
        kern,
        grid=(bn, t_len // tb),
        in_specs=in_specs,
        out_specs=out_specs,
        out_shape=out_shape,
        scratch_shapes=[pltpu.VMEM((tb, DV), F32), pltpu.VMEM((tb + POOL_PAD, POOL_W), F32),
                        pltpu.VMEM((tb, DV), F32),
                        pltpu.VMEM((tb, DK), F32), pltpu.VMEM((tb, DK), F32)],
        compiler_params=pltpu.CompilerParams(
            dimension_semantics=("arbitrary", "arbitrary"), vmem_limit_bytes=VMEM_LIMIT_BYTES),
        name="seq_layer",
    )(h, s_meta, buf_meta, inv_cnt, *weights)


def _intra_by_distance(q, k, v, b, tok, n_tok):
    o_parts = []
    for h in range(HEADS):
        ks = slice(h * DKH, (h + 1) * DKH)
        vs = slice(h * DVH, (h + 1) * DVH)
        o = jnp.sum(q[:, ks] * k[:, ks], axis=1, keepdims=True) * v[:, vs]
        for d in range(1, n_tok):
            kj = pltpu.roll(k[:, ks], d, 0)
            bj = pltpu.roll(b[:, ks], d, 0)
            score = jnp.sum(q[:, ks] * kj * jnp.exp(jnp.minimum(b[:, ks] - bj, 0.0)), axis=1,
                            keepdims=True)
            o = o + jnp.where(tok >= d, score, 0.0) * pltpu.roll(v[:, vs], d, 0)
        o_parts.append(o)
    return o_parts


def _meta_layer(xm, cnt, w):
    n = xm.shape[0]
    xn, q, k, v, u, a_gate = _gate_in(xm, w)
    b = _cumsum_rows(_log_gate(a_gate), n)
    tok = lax.broadcasted_iota(jnp.int32, (n, 1), 0)
    o = jnp.concatenate(_intra_by_distance(q, k, v, b, tok, n), axis=-1)
    kd = (k * jnp.exp(b[n - 1:n, :] - b)).astype(BF16)
    vb = v.astype(BF16)
    state = [_mm_tn(kd[:, h * DKH:(h + 1) * DKH], vb[:, h * DVH:(h + 1) * DVH])
             for h in range(HEADS)]
    ext = jnp.concatenate([jnp.zeros((POOL_PAD, POOL_W), F32), u], axis=0)
    z = _window_sums(ext)[POOL_PAD:, :] / cnt - u
    h_new, _ = _dense_out(xm, xn, o, z, w, False)
    return h_new, state, u
def _step_kernel(*refs, n_tok, sub, depth):
    n_w = len(W_NAMES)
    n_in = 6
    x_ref, s0_ref, buf0_ref, cnt_ref, meta_ref, cntm_ref = refs[:n_in]
    w = _weight_views(refs[n_in:n_in + n_w])
    y_ref, s_ref, bufn_ref, smeta_ref, bufmeta_ref = refs[n_in + n_w:n_in + n_w + 5]
    h_s, hm_s, xin_s, xn_s, q_s, k_s, v_s, la_s, u_s, o_s, z_s = refs[n_in + n_w + 5:]

    layer = pl.program_id(0)
    blk = pl.program_id(1)
    j = pl.program_id(2)
    nj = pl.num_programs(2)
    brows = xin_s.shape[0]
    blk_rows = pl.ds(pl.multiple_of(blk * brows, brows), brows)
    assert sub % SUBLANES == 0

    @pl.when((blk == 0) & (j == 0))
    def _():
        @pl.when(layer == 0)
        def _():
            hm_s[...] = meta_ref[...]

        hm_new, state, u_meta = _meta_layer(hm_s[...], cntm_ref[...], w)
        hm_s[...] = hm_new
        for h in range(HEADS):
            smeta_ref[h] = state[h]
        bufmeta_ref[...] = u_meta

    @pl.when(j == 0)
    def _():
        @pl.when(layer == 0)
        def _():
            xin_s[...] = x_ref[...].reshape(xin_s.shape)

        @pl.when(layer > 0)
        def _():
            xin_s[...] = h_s[blk_rows, :]

        xn, q, k, v, u, a_gate = _gate_in(xin_s[...], w)
        la = _log_gate(a_gate)
        xn_s[...] = xn
        q_s[...] = q
        k_s[...] = k
        v_s[...] = v
        la_s[...] = la
        for g in range(len(POOL_WINDOWS)):
            u_s[g] = u[:, g * POOL_GW:(g + 1) * POOL_GW]

    nr = sub * n_tok
    rows = pl.ds(pl.multiple_of(j * nr, nr), nr)
    row = lax.broadcasted_iota(jnp.int32, (nr, 1), 0)
    tok = row % n_tok
    seq = row // n_tok
    eye = _eye(DKH)
    b = _cumsum_rows(la_s[rows, :], n_tok)
    b_end = b
    for d in range(1, n_tok):
        b_end = jnp.where(tok == n_tok - 1 - d, pltpu.roll(b, nr - d, 0), b_end)
    q = q_s[rows, :]
    k = k_s[rows, :]
    v = v_s[rows, :]
    qd = (q * jnp.exp(b)).astype(BF16)
    kd = k * jnp.exp(b_end - b)
    dec = jnp.exp(b_end)
    vb = v.astype(BF16)
    o_parts = _intra_by_distance(q, k, v, b, tok, n_tok)
    for h in range(HEADS):
        ks = slice(h * DKH, (h + 1) * DKH)
        vs = slice(h * DVH, (h + 1) * DVH)
        o = o_parts[h]
        for s in range(sub):
            mine = seq == s
            s_old = s0_ref[s, h]
            o = o + jnp.where(mine, _mm(qd[:, ks], s_old.astype(BF16)), 0.0)
            kd_s = jnp.where(mine, kd[:, ks], 0.0).astype(BF16)
            dcol = _row_to_col(dec[s * n_tok:s * n_tok + 1, ks], eye)
            s_ref[s, h] = dcol * s_old + _mm_tn(kd_s, vb[:, vs])
        o_parts[h] = o
    o_s[rows, :] = jnp.concatenate(o_parts, axis=-1)

    inv_cnt = cnt_ref[...]
    base = pl.multiple_of(j * nr, nr)
    hist = [buf0_ref[i] for i in range(POOL_BUF)]
    new = []
    for t in range(n_tok):
        rows_t = pl.ds(base + t, sub, stride=n_tok)
        new.append(jnp.concatenate([u_s[g, rows_t, :] for g in range(len(POOL_WINDOWS))], axis=-1))
    ext = hist + new
    for t in range(n_tok):
        last = POOL_BUF + t
        rows_t = pl.ds(base + t, sub, stride=n_tok)
        for g, width in enumerate(POOL_WINDOWS):
            cols = slice(g * POOL_GW, (g + 1) * POOL_GW)
            acc = ext[last][:, cols]
            for back in range(1, width):
                acc = acc + ext[last - back][:, cols]
            z_s[g, rows_t, :] = acc * inv_cnt[:, cols] - new[t][:, cols]
    for i in range(POOL_BUF):
        bufn_ref[i] = ext[i + n_tok]

    @pl.when(j == nj - 1)
    def _():
        z = jnp.concatenate([z_s[g] for g in range(len(POOL_WINDOWS))], axis=-1)
        h_new, y = _dense_out(xin_s[...], xn_s[...], o_s[...], z, w, True)

        @pl.when(layer < depth - 1)
        def _():
            h_s[blk_rows, :] = h_new

        @pl.when(layer == depth - 1)
        def _():
            y_ref[...] = y.reshape(y_ref.shape)


def _step_layers(x, s_all, buf_all, cnt, meta, cnt_meta, ws):
    bn, n_tok, _ = x.shape
    rows = bn * n_tok
    depth = s_all.shape[0]
    assert s_all.shape[1] == bn and buf_all.shape == (depth, POOL_BUF, bn, POOL_W)
    sub, blk = SAMPLE_SUB, SAMPLE_BLOCK
    assert bn % blk == 0 and blk % sub == 0
    n_blk, n_inner = bn // blk, blk // sub
    brows = blk * n_tok
    weights = [ws[n] for n in W_NAMES]

    def state_idx(l, i, j):
        return (l, i * n_inner + j, 0, 0, 0)

    def buf_idx(l, i, j):
        return (l, 0, i * n_inner + j, 0)

    in_specs = [
        pl.BlockSpec((blk, n_tok, D_MODEL), lambda l, i, j: (i, 0, 0)),
        pl.BlockSpec((None, sub, HEADS, DKH, DVH), state_idx),
        pl.BlockSpec((None, POOL_BUF, sub, POOL_W), buf_idx),
        _const_spec(cnt.shape),
        _const_spec(meta.shape),
        _const_spec(cnt_meta.shape),
    ] + _weight_specs(ws, lambda l, i, j: l)
    n_meta = meta.shape[0]
    out_shape = [jax.ShapeDtypeStruct(x.shape, F32),
                 jax.ShapeDtypeStruct(s_all.shape, F32),
                 jax.ShapeDtypeStruct(buf_all.shape, F32),
                 jax.ShapeDtypeStruct((depth, HEADS, DKH, DVH), F32),
                 jax.ShapeDtypeStruct((depth, n_meta, POOL_W), F32)]
    out_specs = [pl.BlockSpec((blk, n_tok, D_MODEL),
                              lambda l, i, j: (jnp.where(l == depth - 1, i, 0), 0, 0)),
                 pl.BlockSpec((None, sub, HEADS, DKH, DVH), state_idx),
                 pl.BlockSpec((None, POOL_BUF, sub, POOL_W), buf_idx),
                 pl.BlockSpec((None, HEADS, DKH, DVH), lambda l, i, j: (l, 0, 0, 0)),
                 pl.BlockSpec((None, n_meta, POOL_W), lambda l, i, j: (l, 0, 0))]
    kern = functools.partial(_step_kernel, n_tok=n_tok, sub=sub, depth=depth)
    scratch = [pltpu.VMEM((rows, D_MODEL), F32), pltpu.VMEM((n_meta, D_MODEL), F32),
               pltpu.VMEM((brows, D_MODEL), F32),
               pltpu.VMEM((brows, D_MODEL), BF16), pltpu.VMEM((brows, DK), F32),
               pltpu.VMEM((brows, DK), F32), pltpu.VMEM((brows, DV), F32),
               pltpu.VMEM((brows, DK), F32), pltpu.VMEM((len(POOL_WINDOWS), brows, POOL_GW), F32),
               pltpu.VMEM((brows, DV), F32), pltpu.VMEM((len(POOL_WINDOWS), brows, POOL_GW), F32)]
    return pl.pallas_call(
        kern,
        grid=(depth, n_blk, n_inner),
        in_specs=in_specs,
        out_specs=out_specs,
        out_shape=out_shape,
        scratch_shapes=scratch,
        compiler_params=pltpu.CompilerParams(
            dimension_semantics=("arbitrary", "arbitrary", "arbitrary"),
            vmem_limit_bytes=VMEM_LIMIT_BYTES),
        name="step_layers",
    )(x, s_all, buf_all, cnt, meta, cnt_meta, *weights)


def _prep_main_kernel(wt_ref, main_ref):
    main_ref[...] = wt_ref[...].T.astype(BF16)


def _prep_tail_kernel(wt_ref, al_ref, mg_ref):
    lane = lax.broadcasted_iota(jnp.int32, al_ref.shape, 1)
    al_ref[...] = jnp.where(lane < GATE_RANK, wt_ref[0, 0:LANES, :].T, 0.0).astype(BF16)
    mg_ref[...] = wt_ref[0, GATE_RANK:, :].T.astype(BF16)


def _prep_w_in(w_in):
    depth, rows, cols = w_in.shape
    n_tail = cols - MAIN_COLS
    wt = jnp.swapaxes(w_in, 1, 2)
    cb = PREP_COLS
    assert MAIN_COLS % cb == 0
    w_main = pl.pallas_call(
        _prep_main_kernel,
        grid=(depth, MAIN_COLS // cb),
        in_specs=[pl.BlockSpec((None, cb, rows), lambda l, i: (l, i, 0))],
        out_specs=pl.BlockSpec((None, rows, cb), lambda l, i: (l, 0, i)),
        out_shape=jax.ShapeDtypeStruct((depth, rows, MAIN_COLS), BF16),
        name="prep_w_main",
    )(wt)
    w_al, w_mg = pl.pallas_call(
        _prep_tail_kernel,
        grid=(depth,),
        in_specs=[pl.BlockSpec((pl.Element(1), pl.Element(n_tail), pl.Element(rows)),
                               lambda l: (l, MAIN_COLS, 0))],
        out_specs=[pl.BlockSpec((None, rows, LANES), lambda l: (l, 0, 0)),
                   pl.BlockSpec((None, rows, n_tail - GATE_RANK), lambda l: (l, 0, 0))],
        out_shape=[jax.ShapeDtypeStruct((depth, rows, LANES), BF16),
                   jax.ShapeDtypeStruct((depth, rows, n_tail - GATE_RANK), BF16)],
        compiler_params=pltpu.CompilerParams(vmem_limit_bytes=VMEM_LIMIT_BYTES),
        name="prep_w_tail",
    )(wt)
    return w_main, w_al, w_mg


def _stacked_weights(norm_g, w_in, w_alpha, b_alpha, gla_gain, w_a, pool_w, pool_scale, w_b,
                     b_merge, w_out, final_norm_g):
    w_main, w_al, w_mg = _prep_w_in(w_in)
    pad = LANES - GATE_RANK
    vectors = {"norm_g": norm_g, "b_alpha": b_alpha, "gla_gain": gla_gain,
               "pool_scale": pool_scale, "b_merge": b_merge}
    assert all(vectors[name].shape[-1] == width for name, width in VEC_SEGS)
    return {
        "vec": jnp.concatenate([vectors[name] for name, _ in VEC_SEGS], axis=-1)[:, None, :],
        "w_main": w_main, "w_al": w_al, "w_mg": w_mg,
        "w_alpha": jnp.pad(w_alpha.astype(BF16), ((0, 0), (0, pad), (0, 0))),
        "w_a": w_a.astype(BF16),
        "pool_w": pool_w.astype(BF16),
        "w_b": w_b.astype(BF16),
        "w_out": w_out.astype(BF16),
        "final_g": final_norm_g[None, :],
    }


def _window_row():
    return np.repeat(np.asarray(POOL_WINDOWS, np.float32), POOL_GW)[None, :]


def kernel(x_prompt, x_sample, state_gla, state_pool, meta_tokens, norm_g, w_in, w_alpha, b_alpha,
           gla_gain, w_a, pool_w, pool_scale, w_b, b_merge, w_out, final_norm_g):
    depth = w_in.shape[0]
    wrow = _window_row()
    cnt_meta = jnp.asarray(np.minimum(wrow, np.arange(1, N_META + 1, dtype=np.float32)[:, None]))
    inv_w = jnp.asarray(1.0 / wrow)

    ws = _stacked_weights(norm_g, w_in, w_alpha, b_alpha, gla_gain, w_a, pool_w, pool_scale, w_b,
                          b_merge, w_out, final_norm_g)
    buf_s = jnp.swapaxes(state_pool, 1, 2)
    y_s, sg_s, bufn_s, s_meta, buf_meta = _step_layers(
        x_sample, state_gla, buf_s, inv_w, meta_tokens.astype(F32), cnt_meta, ws)

    h_p = x_prompt
    sg_p, sp_p = [], []
    for l in range(depth):
        h_p, s_p, b_p = _seq_layer(h_p, s_meta, buf_meta, inv_w, ws, l, is_last=l == depth - 1)
        sg_p.append(s_p)
        sp_p.append(b_p[:, POOL_PAD - POOL_BUF:, :])
    return (h_p, y_s, jnp.stack(sg_p), jnp.stack(sp_p), sg_s, jnp.swapaxes(bufn_s, 1, 2))
```

```python
import functools

import numpy as np
import jax
import jax.numpy as jnp
from jax import lax
from jax.experimental import pallas as pl
from jax.experimental.pallas import tpu as pltpu

F32 = jnp.float32
BF16 = jnp.bfloat16

D_MODEL = 1024
N_META = 16
HEADS = 4
DKH = 128
DVH = 256
DK = HEADS * DKH
DV = HEADS * DVH
GATE_RANK = 16
GATE_TAU = 16.0
POOL_W = 512
POOL_WINDOWS = (2, 4, 8, 16)
POOL_GW = POOL_W // len(POOL_WINDOWS)
POOL_BUF = max(POOL_WINDOWS) - 1
POOL_PAD = POOL_BUF + 1
EPS = 1e-6
IN_SIZES = (DK, DK, DV, DV, POOL_W, POOL_W, GATE_RANK, 2 * D_MODEL)
IN_OFFS = tuple(int(o) for o in np.cumsum((0,) + IN_SIZES))

LANES = 128
SUBLANES = 8
VMEM_LIMIT_BYTES = 56 * 1024 * 1024

MAX_FACTORED_LOG_DECAY = 60.0

SEQ_BLOCK = 512
SEQ_CHUNK = 256
SAMPLE_BLOCK = 32
SAMPLE_SUB = 8
PREP_COLS = 512


def _mm(a, b):
    return jnp.dot(a, b, preferred_element_type=F32)


def _mm_nt(a, b):
    return lax.dot_general(a, b, (((1,), (1,)), ((), ())), preferred_element_type=F32)


def _mm_tn(a, b):
    return lax.dot_general(a, b, (((0,), (0,)), ((), ())), preferred_element_type=F32)


def _sigmoid(x):
    return 1.0 / (1.0 + jnp.exp(-x))


def _silu(x):
    return x * _sigmoid(x)


def _log_sigmoid(x):
    return jnp.minimum(x, 0.0) - jnp.log1p(jnp.exp(-jnp.abs(x)))


def _rmsnorm(x, g):
    ms = jnp.mean(x * x, axis=-1, keepdims=True)
    return x * lax.rsqrt(ms + EPS) * g


def _row_to_col(row, eye):
    return jnp.sum(eye * row, axis=1, keepdims=True)


def _eye(n):
    r = lax.broadcasted_iota(jnp.int32, (n, n), 0)
    c = lax.broadcasted_iota(jnp.int32, (n, n), 1)
    return (r == c).astype(F32)


W_NAMES = ("vec", "w_main", "w_al", "w_mg", "w_alpha", "w_a", "pool_w", "w_b", "w_out", "final_g")
MAIN_SEGS = ("w_q", "w_k", "w_v", "w_ga", "w_u", "w_gb")
MAIN_COLS = IN_OFFS[len(MAIN_SEGS)]
VEC_SEGS = (("norm_g", D_MODEL), ("b_alpha", DK), ("gla_gain", DV), ("pool_scale", POOL_W),
            ("b_merge", 2 * D_MODEL))


def _weight_views(refs):
    w = dict(zip(W_NAMES, refs))
    for i, name in enumerate(MAIN_SEGS):
        w[name] = w["w_main"].at[:, IN_OFFS[i]:IN_OFFS[i + 1]]
    off = 0
    for name, width in VEC_SEGS:
        w[name] = w["vec"].at[:, off:off + width]
        off += width
    return w


def _gate_in(x, w):
    xn = _rmsnorm(x, w["norm_g"][...]).astype(BF16)
    alow = _mm(xn, w["w_al"][...])
    q = _mm(xn, w["w_q"][...]) * (DKH ** -0.5)
    k = _mm(xn, w["w_k"][...])
    a = _mm(alow.astype(BF16), w["w_alpha"][...]) + w["b_alpha"][...]
    v = _mm(xn, w["w_v"][...])
    u = _mm(xn, w["w_u"][...])
    return xn, q, k, v, u, a


def _log_gate(a):
    return _log_sigmoid(a) * (1.0 / GATE_TAU)


def _dense_out(x, xn, o, z, w, is_last, pre=None):
    if pre is None:
        pre = (_mm(xn, w["w_ga"][...]), _mm(xn, w["w_gb"][...]), _mm(xn, w["w_mg"][...]))
    ga, gb, mg = pre
    gain = w["gla_gain"][...]
    parts = []
    for h in range(HEADS):
        oh = o[:, h * DVH:(h + 1) * DVH]
        ms = jnp.mean(oh * oh, axis=-1, keepdims=True)
        parts.append(oh * lax.rsqrt(ms + EPS) * gain[:, h * DVH:(h + 1) * DVH])
    on = jnp.concatenate(parts, axis=-1) * _silu(ga)
    ya = _mm(on.astype(BF16), w["w_a"][...])

    yp = []
    for g in range(len(POOL_WINDOWS)):
        zg = z[:, g * POOL_GW:(g + 1) * POOL_GW].astype(BF16)
        yp.append(_mm(zg, w["pool_w"][g]))
    y_pool = jnp.concatenate(yp, axis=-1) * w["pool_scale"][...]
    yb = _mm((y_pool * _silu(gb)).astype(BF16), w["w_b"][...])

    gates = _sigmoid(mg + w["b_merge"][...])
    merged = gates[:, :D_MODEL] * ya + gates[:, D_MODEL:] * yb
    h_new = x + _mm(merged.astype(BF16), w["w_out"][...])
    y = _rmsnorm(h_new, w["final_g"][...]) if is_last else None
    return h_new, y


def _window_sums(e):
    p1 = e + pltpu.roll(e, 1, 0)
    t = p1[:, POOL_GW:]
    p2 = t + pltpu.roll(t, 2, 0)
    t = p2[:, POOL_GW:]
    p3 = t + pltpu.roll(t, 4, 0)
    t = p3[:, POOL_GW:]
    p4 = t + pltpu.roll(t, 8, 0)
    return jnp.concatenate([p1[:, :POOL_GW], p2[:, :POOL_GW], p3[:, :POOL_GW], p4], axis=-1)


def _split_hi_lo(x):
    hi = x.astype(BF16)
    lo = (x - hi.astype(F32)).astype(BF16)
    return hi, lo


def _cumsum_rows(x, n):
    pos = lax.broadcasted_iota(jnp.int32, (x.shape[0], 1), 0) % n
    step = 1
    while step < n:
        x = x + jnp.where(pos >= step, pltpu.roll(x, step, 0), 0.0)
        step *= 2
    return x


def _seq_kernel(*refs, tb, chunk, is_last):
    n_w = len(W_NAMES)
    h_ref, s0_ref, buf0_ref, cnt_ref = refs[:4]
    w = _weight_views(refs[4:4 + n_w])
    hy_ref, s_ref, bufn_ref, o_s, ext_s, sc_s, b_s, k_s = refs[4 + n_w:]

    t = pl.program_id(1)

    @pl.when(t == 0)
    def _():
        s_ref[0] = s0_ref[0]
        ext_s[0:POOL_PAD, :] = buf0_ref[0]

    x = h_ref[0]
    xn, q, k, v, u, a_gate = _gate_in(x, w)
    n_chunks = tb // chunk
    chunks = [(c, h) for c in range(n_chunks) for h in range(HEADS)]

    def sl(c, h):
        return (slice(c * chunk, (c + 1) * chunk), slice(h * DKH, (h + 1) * DKH),
                slice(h * DVH, (h + 1) * DVH))

    neg = jnp.minimum(a_gate, 0.0)
    steepest = jnp.sum(neg[0:chunk], axis=0, keepdims=True)
    for c in range(1, n_chunks):
        steepest = jnp.minimum(
            steepest, jnp.sum(neg[c * chunk:(c + 1) * chunk], axis=0, keepdims=True))
    decay_bound = (jnp.min(steepest) - chunk * np.log(2.0)) * (1.0 / GATE_TAU)
    factored_ok = decay_bound >= -MAX_FACTORED_LOG_DECAY
    la = _log_gate(a_gate)
    vb = v.astype(BF16)
    causal = (lax.broadcasted_iota(jnp.int32, (chunk, chunk), 0)
              >= lax.broadcasted_iota(jnp.int32, (chunk, chunk), 1))
    tril = causal.astype(BF16)
    eye = _eye(DKH)
    la_hi, la_lo = _split_hi_lo(la)
    b = jnp.concatenate(
        [_mm(tril, la_hi[c * chunk:(c + 1) * chunk])
         + _mm(tril, la_lo[c * chunk:(c + 1) * chunk]) for c in range(n_chunks)], axis=0)
    ga = _mm(xn, w["w_ga"][...])
    b_ends = [b[(c + 1) * chunk - 1:(c + 1) * chunk, :] for c in range(n_chunks)]
    to_end = jnp.concatenate(
        [b_ends[c] - b[c * chunk:(c + 1) * chunk, :] for c in range(n_chunks)], axis=0)
    qd = (q * jnp.exp(b)).astype(BF16)
    kd = (k * jnp.exp(to_end)).astype(BF16)

    kinv = (k * jnp.exp(-b)).astype(BF16)
    a_m = {}
    for c, h in chunks:
        rows, ks, _ = sl(c, h)
        a = _mm_nt(qd[rows, ks], kinv[rows, ks])
        a_m[c, h] = jnp.where(causal, a, 0.0).astype(BF16)
    gb = _mm(xn, w["w_gb"][...])
    d_s, dcol = {}, {}
    for c, h in chunks:
        rows, ks, vs = sl(c, h)
        d_s[c, h] = _mm_tn(kd[rows, ks], vb[rows, vs])
        dcol[c, h] = _row_to_col(jnp.exp(b_ends[c])[:, ks], eye)
    mg = _mm(xn, w["w_mg"][...])

    s_cur = [s_ref[0, h] for h in range(HEADS)]
    for c, h in chunks:
        rows, ks, vs = sl(c, h)
        s_in = s_cur[h].astype(BF16)
        sc_s[c, h] = s_in
        lhs = jnp.concatenate([a_m[c, h], qd[rows, ks]], axis=1)
        rhs = jnp.concatenate([vb[rows, vs], s_in], axis=0)
        o_s[rows, vs] = _mm(lhs, rhs)
        s_cur[h] = dcol[c, h] * s_cur[h] + d_s[c, h]
    for h in range(HEADS):
        s_ref[0, h] = s_cur[h]

    ext_s[POOL_PAD:POOL_PAD + tb, :] = u
    z = _window_sums(ext_s[...])[POOL_PAD:, :] * cnt_ref[...] - u
    bufn_ref[0] = ext_s[tb:tb + POOL_PAD, :]
    ext_s[0:POOL_PAD, :] = ext_s[tb:tb + POOL_PAD, :]

    h_new, y = _dense_out(x, xn, o_s[...], z, w, is_last, pre=(ga, gb, mg))
    hy_ref[0] = y if is_last else h_new

    @pl.when(jnp.logical_not(factored_ok))
    def _():
        b_s[...] = b
        k_s[...] = k
        col_id = lax.broadcasted_iota(jnp.int32, (chunk, chunk), 1)
        for c, h in chunks:
            rows, ks, vs = sl(c, h)
            qh = q[rows, ks]
            bh = b[rows, ks]

            def columns(g, acc, c=c, ks=ks, qh=qh, bh=bh):
                r0 = pl.multiple_of(c * chunk + g * SUBLANES, SUBLANES)
                b_tile = b_s[pl.ds(r0, SUBLANES), ks]
                k_tile = k_s[pl.ds(r0, SUBLANES), ks]
                for r in range(SUBLANES):
                    decay = jnp.exp(jnp.minimum(bh - b_tile[r:r + 1, :], 0.0))
                    col = jnp.sum(qh * k_tile[r:r + 1, :] * decay, axis=1, keepdims=True)
                    acc = jnp.where(col_id == g * SUBLANES + r, col, acc)
                return acc

            a = lax.fori_loop(0, chunk // SUBLANES, columns, jnp.zeros((chunk, chunk), F32))
            a = jnp.where(causal, a, 0.0).astype(BF16)
            o_s[rows, vs] = _mm(qd[rows, ks], sc_s[c, h]) + _mm(a, vb[rows, vs])
        h_fix, y_fix = _dense_out(x, xn, o_s[...], z, w, is_last, pre=(ga, gb, mg))
        hy_ref[0] = y_fix if is_last else h_fix


def _const_spec(shape):
    nd = len(shape)
    return pl.BlockSpec(shape, lambda *_: (0,) * nd, pipeline_mode=pl.Buffered(1))


def _weight_specs(ws, layer_of):
    specs = []
    for name in W_NAMES:
        shape = ws[name].shape
        if name == "final_g":
            specs.append(_const_spec(shape))
        else:
            zeros = (0,) * (len(shape) - 1)
            specs.append(pl.BlockSpec((None,) + shape[1:],
                                      lambda *g, zeros=zeros: (layer_of(*g),) + zeros,
                                      pipeline_mode=pl.Buffered(1)))
    return specs


def _seq_layer(h, s_meta, buf_meta, inv_cnt, ws, layer, *, is_last):
    bn, t_len, _ = h.shape
    tb = min(SEQ_BLOCK, t_len)
    chunk = min(SEQ_CHUNK, tb)
    assert t_len % tb == 0 and tb % chunk == 0 and buf_meta.shape[1] == POOL_PAD
    weights = [ws[n] for n in W_NAMES]
    in_specs = [
        pl.BlockSpec((1, tb, D_MODEL), lambda b, t: (b, t, 0)),
        pl.BlockSpec((1, HEADS, DKH, DVH), lambda b, t: (layer, 0, 0, 0)),
        pl.BlockSpec((1, POOL_PAD, POOL_W), lambda b, t: (layer, 0, 0)),
        _const_spec(inv_cnt.shape),
    ] + _weight_specs(ws, lambda b, t: layer)
    out_shape = [jax.ShapeDtypeStruct(h.shape, F32),
                 jax.ShapeDtypeStruct((bn, HEADS, DKH, DVH), F32),
                 jax.ShapeDtypeStruct((bn, POOL_PAD, POOL_W), F32)]
    out_specs = [pl.BlockSpec((1, tb, D_MODEL), lambda b, t: (b, t, 0)),
                 pl.BlockSpec((1, HEADS, DKH, DVH), lambda b, t: (b, 0, 0, 0)),
                 pl.BlockSpec((1, POOL_PAD, POOL_W), lambda b, t: (b, 0, 0))]
    kern = functools.partial(_seq_kernel, tb=tb, chunk=chunk, is_last=is_last)
    return pl.pallas_call(
        kern,
        grid=(bn, t_len // tb),
        in_specs=in_specs,
        out_specs=out_specs,
        out_shape=out_shape,
        scratch_shapes=[pltpu.VMEM((tb, DV), F32), pltpu.VMEM((tb + POOL_PAD, POOL_W), F32),
                        pltpu.VMEM((tb // chunk, HEADS, DKH, DVH), BF16),
                        pltpu.VMEM((tb, DK), F32), pltpu.VMEM((tb, DK), F32)],
        compiler_params=pltpu.CompilerParams(
            dimension_semantics=("arbitrary", "arbitrary"), vmem_limit_bytes=VMEM_LIMIT_BYTES),
        name="seq_layer",
    )(h, s_meta, buf_meta, inv_cnt, *weights)


def _intra_by_distance(q, k, v, b, tok, n_tok):
    o_parts = []
    for h in range(HEADS):
        ks = slice(h * DKH, (h + 1) * DKH)
        vs = slice(h * DVH, (h + 1) * DVH)
        o = jnp.sum(q[:, ks] * k[:, ks], axis=1, keepdims=True) * v[:, vs]
        for d in range(1, n_tok):
            kj = pltpu.roll(k[:, ks], d, 0)
            bj = pltpu.roll(b[:, ks], d, 0)
            score = jnp.sum(q[:, ks] * kj * jnp.exp(jnp.minimum(b[:, ks] - bj, 0.0)), axis=1,
                            keepdims=True)
            o = o + jnp.where(tok >= d, score, 0.0) * pltpu.roll(v[:, vs], d, 0)
        o_parts.append(o)
    return o_parts


def _meta_layer(xm, cnt, w):
    n = xm.shape[0]
    xn, q, k, v, u, a_gate = _gate_in(xm, w)
    b = _cumsum_rows(_log_gate(a_gate), n)
    tok = lax.broadcasted_iota(jnp.int32, (n, 1), 0)
    o = jnp.concatenate(_intra_by_distance(q, k, v, b, tok, n), axis=-1)
    kd = (k * jnp.exp(b[n - 1:n, :] - b)).astype(BF16)
    vb = v.astype(BF16)
    state = [_mm_tn(kd[:, h * DKH:(h + 1) * DKH], vb[:, h * DVH:(h + 1) * DVH])
             for h in range(HEADS)]
    ext = jnp.concatenate([jnp.zeros((POOL_PAD, POOL_W), F32), u], axis=0)
    z = _window_sums(ext)[POOL_PAD:, :] / cnt - u
    h_new, _ = _dense_out(xm, xn, o, z, w, False)
    return h_new, state, u
def _step_kernel(*refs, n_tok, sub, depth):
    n_w = len(W_NAMES)
    n_in = 6
    x_ref, s0_ref, buf0_ref, cnt_ref, meta_ref, cntm_ref = refs[:n_in]
    w = _weight_views(refs[n_in:n_in + n_w])
    y_ref, s_ref, bufn_ref, smeta_ref, bufmeta_ref = refs[n_in + n_w:n_in + n_w + 5]
    h_s, hm_s, xin_s, xn_s, q_s, k_s, v_s, la_s, u_s, o_s, z_s = refs[n_in + n_w + 5:]

    layer = pl.program_id(0)
    blk = pl.program_id(1)
    j = pl.program_id(2)
    nj = pl.num_programs(2)
    brows = xin_s.shape[0]
    blk_rows = pl.ds(pl.multiple_of(blk * brows, brows), brows)
    assert sub % SUBLANES == 0

    @pl.when((blk == 0) & (j == 0))
    def _():
        @pl.when(layer == 0)
        def _():
            hm_s[...] = meta_ref[...]

        hm_new, state, u_meta = _meta_layer(hm_s[...], cntm_ref[...], w)
        hm_s[...] = hm_new
        for h in range(HEADS):
            smeta_ref[h] = state[h]
        bufmeta_ref[...] = u_meta

    @pl.when(j == 0)
    def _():
        @pl.when(layer == 0)
        def _():
            xin_s[...] = x_ref[...].reshape(xin_s.shape)

        @pl.when(layer > 0)
        def _():
            xin_s[...] = h_s[blk_rows, :]

        xn, q, k, v, u, a_gate = _gate_in(xin_s[...], w)
        la = _log_gate(a_gate)
        xn_s[...] = xn
        q_s[...] = q
        k_s[...] = k
        v_s[...] = v
        la_s[...] = la
        for g in range(len(POOL_WINDOWS)):
            u_s[g] = u[:, g * POOL_GW:(g + 1) * POOL_GW]

    nr = sub * n_tok
    rows = pl.ds(pl.multiple_of(j * nr, nr), nr)
    row = lax.broadcasted_iota(jnp.int32, (nr, 1), 0)
    tok = row % n_tok
    seq = row // n_tok
    eye = _eye(DKH)
    b = _cumsum_rows(la_s[rows, :], n_tok)
    b_end = b
    for d in range(1, n_tok):
        b_end = jnp.where(tok == n_tok - 1 - d, pltpu.roll(b, nr - d, 0), b_end)
    q = q_s[rows, :]
    k = k_s[rows, :]
    v = v_s[rows, :]
    qd = (q * jnp.exp(b)).astype(BF16)
    kd = k * jnp.exp(b_end - b)
    dec = jnp.exp(b_end)
    vb = v.astype(BF16)
    o_parts = _intra_by_distance(q, k, v, b, tok, n_tok)
    for h in range(HEADS):
        ks = slice(h * DKH, (h + 1) * DKH)
        vs = slice(h * DVH, (h + 1) * DVH)
        o = o_parts[h]
        for s in range(sub):
            mine = seq == s
            s_old = s0_ref[s, h]
            o = o + jnp.where(mine, _mm(qd[:, ks], s_old.astype(BF16)), 0.0)
            kd_s = jnp.where(mine, kd[:, ks], 0.0).astype(BF16)
            dcol = _row_to_col(dec[s * n_tok:s * n_tok + 1, ks], eye)
            s_ref[s, h] = dcol * s_old + _mm_tn(kd_s, vb[:, vs])
        o_parts[h] = o
    o_s[rows, :] = jnp.concatenate(o_parts, axis=-1)

    inv_cnt = cnt_ref[...]
    base = pl.multiple_of(j * nr, nr)
    hist = [buf0_ref[i] for i in range(POOL_BUF)]
    new = []
    for t in range(n_tok):
        rows_t = pl.ds(base + t, sub, stride=n_tok)
        new.append(jnp.concatenate([u_s[g, rows_t, :] for g in range(len(POOL_WINDOWS))], axis=-1))
    ext = hist + new
    for t in range(n_tok):
        last = POOL_BUF + t
        rows_t = pl.ds(base + t, sub, stride=n_tok)
        for g, width in enumerate(POOL_WINDOWS):
            cols = slice(g * POOL_GW, (g + 1) * POOL_GW)
            acc = ext[last][:, cols]
            for back in range(1, width):
                acc = acc + ext[last - back][:, cols]
            z_s[g, rows_t, :] = acc * inv_cnt[:, cols] - new[t][:, cols]
    for i in range(POOL_BUF):
        bufn_ref[i] = ext[i + n_tok]

    @pl.when(j == nj - 1)
    def _():
        z = jnp.concatenate([z_s[g] for g in range(len(POOL_WINDOWS))], axis=-1)
        h_new, y = _dense_out(xin_s[...], xn_s[...], o_s[...], z, w, True)

        @pl.when(layer < depth - 1)
        def _():
            h_s[blk_rows, :] = h_new

        @pl.when(layer == depth - 1)
        def _():
            y_ref[...] = y.reshape(y_ref.shape)


def _step_layers(x, s_all, buf_all, cnt, meta, cnt_meta, ws):
    bn, n_tok, _ = x.shape
    rows = bn * n_tok
    depth = s_all.shape[0]
    assert s_all.shape[1] == bn and buf_all.shape == (depth, POOL_BUF, bn, POOL_W)
    sub, blk = SAMPLE_SUB, SAMPLE_BLOCK
    assert bn % blk == 0 and blk % sub == 0
    n_blk, n_inner = bn // blk, blk // sub
    brows = blk * n_tok
    weights = [ws[n] for n in W_NAMES]

    def state_idx(l, i, j):
        return (l, i * n_inner + j, 0, 0, 0)

    def buf_idx(l, i, j):
        return (l, 0, i * n_inner + j, 0)

    in_specs = [
        pl.BlockSpec((blk, n_tok, D_MODEL), lambda l, i, j: (i, 0, 0)),
        pl.BlockSpec((None, sub, HEADS, DKH, DVH), state_idx),
        pl.BlockSpec((None, POOL_BUF, sub, POOL_W), buf_idx),
        _const_spec(cnt.shape),
        _const_spec(meta.shape),
        _const_spec(cnt_meta.shape),
    ] + _weight_specs(ws, lambda l, i, j: l)
    n_meta = meta.shape[0]
    out_shape = [jax.ShapeDtypeStruct(x.shape, F32),
                 jax.ShapeDtypeStruct(s_all.shape, F32),
                 jax.ShapeDtypeStruct(buf_all.shape, F32),
                 jax.ShapeDtypeStruct((depth, HEADS, DKH, DVH), F32),
                 jax.ShapeDtypeStruct((depth, n_meta, POOL_W), F32)]
    out_specs = [pl.BlockSpec((blk, n_tok, D_MODEL),
                              lambda l, i, j: (jnp.where(l == depth - 1, i, 0), 0, 0)),
                 pl.BlockSpec((None, sub, HEADS, DKH, DVH), state_idx),
                 pl.BlockSpec((None, POOL_BUF, sub, POOL_W), buf_idx),
                 pl.BlockSpec((None, HEADS, DKH, DVH), lambda l, i, j: (l, 0, 0, 0)),
                 pl.BlockSpec((None, n_meta, POOL_W), lambda l, i, j: (l, 0, 0))]
    kern = functools.partial(_step_kernel, n_tok=n_tok, sub=sub, depth=depth)
    scratch = [pltpu.VMEM((rows, D_MODEL), F32), pltpu.VMEM((n_meta, D_MODEL), F32),
               pltpu.VMEM((brows, D_MODEL), F32),
               pltpu.VMEM((brows, D_MODEL), BF16), pltpu.VMEM((brows, DK), F32),
               pltpu.VMEM((brows, DK), F32), pltpu.VMEM((brows, DV), F32),
               pltpu.VMEM((brows, DK), F32), pltpu.VMEM((len(POOL_WINDOWS), brows, POOL_GW), F32),
               pltpu.VMEM((brows, DV), F32), pltpu.VMEM((len(POOL_WINDOWS), brows, POOL_GW), F32)]
    return pl.pallas_call(
        kern,
        grid=(depth, n_blk, n_inner),
        in_specs=in_specs,
        out_specs=out_specs,
        out_shape=out_shape,
        scratch_shapes=scratch,
        compiler_params=pltpu.CompilerParams(
            dimension_semantics=("arbitrary", "arbitrary", "arbitrary"),
            vmem_limit_bytes=VMEM_LIMIT_BYTES),
        name="step_layers",
    )(x, s_all, buf_all, cnt, meta, cnt_meta, *weights)


def _prep_main_kernel(wt_ref, main_ref):
    main_ref[...] = wt_ref[...].T.astype(BF16)


def _prep_tail_kernel(wt_ref, al_ref, mg_ref):
    lane = lax.broadcasted_iota(jnp.int32, al_ref.shape, 1)
    al_ref[...] = jnp.where(lane < GATE_RANK, wt_ref[0, 0:LANES, :].T, 0.0).astype(BF16)
    mg_ref[...] = wt_ref[0, GATE_RANK:, :].T.astype(BF16)


def _prep_w_in(w_in):
    depth, rows, cols = w_in.shape
    n_tail = cols - MAIN_COLS
    wt = jnp.swapaxes(w_in, 1, 2)
    cb = PREP_COLS
    assert MAIN_COLS % cb == 0
    w_main = pl.pallas_call(
        _prep_main_kernel,
        grid=(depth, MAIN_COLS // cb),
        in_specs=[pl.BlockSpec((None, cb, rows), lambda l, i: (l, i, 0))],
        out_specs=pl.BlockSpec((None, rows, cb), lambda l, i: (l, 0, i)),
        out_shape=jax.ShapeDtypeStruct((depth, rows, MAIN_COLS), BF16),
        name="prep_w_main",
    )(wt)
    w_al, w_mg = pl.pallas_call(
        _prep_tail_kernel,
        grid=(depth,),
        in_specs=[pl.BlockSpec((pl.Element(1), pl.Element(n_tail), pl.Element(rows)),
                               lambda l: (l, MAIN_COLS, 0))],
        out_specs=[pl.BlockSpec((None, rows, LANES), lambda l: (l, 0, 0)),
                   pl.BlockSpec((None, rows, n_tail - GATE_RANK), lambda l: (l, 0, 0))],
        out_shape=[jax.ShapeDtypeStruct((depth, rows, LANES), BF16),
                   jax.ShapeDtypeStruct((depth, rows, n_tail - GATE_RANK), BF16)],
        compiler_params=pltpu.CompilerParams(vmem_limit_bytes=VMEM_LIMIT_BYTES),
        name="prep_w_tail",
    )(wt)
    return w_main, w_al, w_mg


def _stacked_weights(norm_g, w_in, w_alpha, b_alpha, gla_gain, w_a, pool_w, pool_scale, w_b,
                     b_merge, w_out, final_norm_g):
    w_main, w_al, w_mg = _prep_w_in(w_in)
    pad = LANES - GATE_RANK
    vectors = {"norm_g": norm_g, "b_alpha": b_alpha, "gla_gain": gla_gain,
               "pool_scale": pool_scale, "b_merge": b_merge}
    assert all(vectors[name].shape[-1] == width for name, width in VEC_SEGS)
    return {
        "vec": jnp.concatenate([vectors[name] for name, _ in VEC_SEGS], axis=-1)[:, None, :],
        "w_main": w_main, "w_al": w_al, "w_mg": w_mg,
        "w_alpha": jnp.pad(w_alpha.astype(BF16), ((0, 0), (0, pad), (0, 0))),
        "w_a": w_a.astype(BF16),
        "pool_w": pool_w.astype(BF16),
        "w_b": w_b.astype(BF16),
        "w_out": w_out.astype(BF16),
        "final_g": final_norm_g[None, :],
    }


def _window_row():
    return np.repeat(np.asarray(POOL_WINDOWS, np.float32), POOL_GW)[None, :]


def kernel(x_prompt, x_sample, state_gla, state_pool, meta_tokens, norm_g, w_in, w_alpha, b_alpha,
           gla_gain, w_a, pool_w, pool_scale, w_b, b_merge, w_out, final_norm_g):
    depth = w_in.shape[0]
    wrow = _window_row()
    cnt_meta = jnp.asarray(np.minimum(wrow, np.arange(1, N_META + 1, dtype=np.float32)[:, None]))
    inv_w = jnp.asarray(1.0 / wrow)

    ws = _stacked_weights(norm_g, w_in, w_alpha, b_alpha, gla_gain, w_a, pool_w, pool_scale, w_b,
                          b_merge, w_out, final_norm_g)
    buf_s = jnp.swapaxes(state_pool, 1, 2)
    y_s, sg_s, bufn_s, s_meta, buf_meta = _step_layers(
        x_sample, state_gla, buf_s, inv_w, meta_tokens.astype(F32), cnt_meta, ws)

    h_p = x_prompt
    sg_p, sp_p = [], []
    for l in range(depth):
        h_p, s_p, b_p = _seq_layer(h_p, s_meta, buf_meta, inv_w, ws, l, is_last=l == depth - 1)
        sg_p.append(s_p)
        sp_p.append(b_p[:, POOL_PAD - POOL_BUF:, :])
    return (h_p, y_s, jnp.stack(sg_p), jnp.stack(sp_p), sg_s, jnp.swapaxes(bufn_s, 1, 2))
```

```python
import functools

import numpy as np
import jax
import jax.numpy as jnp
from jax import lax
from jax.experimental import pallas as pl
from jax.experimental.pallas import tpu as pltpu

F32 = jnp.float32
BF16 = jnp.bfloat16

D_MODEL = 1024
N_META = 16
HEADS = 4
DKH = 128
DVH = 256
DK = HEADS * DKH
DV = HEADS * DVH
GATE_RANK = 16
GATE_TAU = 16.0
POOL_W = 512
POOL_WINDOWS = (2, 4, 8, 16)
POOL_GW = POOL_W // len(POOL_WINDOWS)
POOL_BUF = max(POOL_WINDOWS) - 1
POOL_PAD = POOL_BUF + 1
EPS = 1e-6
IN_SIZES = (DK, DK, DV, DV, POOL_W, POOL_W, GATE_RANK, 2 * D_MODEL)
IN_OFFS = tuple(int(o) for o in np.cumsum((0,) + IN_SIZES))

LANES = 128
SUBLANES = 8
VMEM_LIMIT_BYTES = 56 * 1024 * 1024

MAX_FACTORED_LOG_DECAY = 60.0

SEQ_BLOCK = 512
SEQ_CHUNK = 256
SAMPLE_BLOCK = 32
SAMPLE_SUB = 8
PREP_COLS = 512


def _mm(a, b):
    return jnp.dot(a, b, preferred_element_type=F32)


def _mm_nt(a, b):
    return lax.dot_general(a, b, (((1,), (1,)), ((), ())), preferred_element_type=F32)


def _mm_tn(a, b):
    return lax.dot_general(a, b, (((0,), (0,)), ((), ())), preferred_element_type=F32)


def _sigmoid(x):
    return 1.0 / (1.0 + jnp.exp(-x))


def _silu(x):
    return x * _sigmoid(x)


def _log_sigmoid(x):
    return jnp.minimum(x, 0.0) - jnp.log1p(jnp.exp(-jnp.abs(x)))


def _rmsnorm(x, g):
    ms = jnp.mean(x * x, axis=-1, keepdims=True)
    return x * lax.rsqrt(ms + EPS) * g


def _row_to_col(row, eye):
    return jnp.sum(eye * row, axis=1, keepdims=True)


def _eye(n):
    r = lax.broadcasted_iota(jnp.int32, (n, n), 0)
    c = lax.broadcasted_iota(jnp.int32, (n, n), 1)
    return (r == c).astype(F32)


W_NAMES = ("vec", "w_main", "w_al", "w_mg", "w_alpha", "w_a", "pool_w", "w_b", "w_out", "final_g")
MAIN_SEGS = ("w_q", "w_k", "w_v", "w_ga", "w_u", "w_gb")
MAIN_COLS = IN_OFFS[len(MAIN_SEGS)]
VEC_SEGS = (("norm_g", D_MODEL), ("b_alpha", DK), ("gla_gain", DV), ("pool_scale", POOL_W),
            ("b_merge", 2 * D_MODEL))


def _weight_views(refs):
    w = dict(zip(W_NAMES, refs))
    for i, name in enumerate(MAIN_SEGS):
        w[name] = w["w_main"].at[:, IN_OFFS[i]:IN_OFFS[i + 1]]
    off = 0
    for name, width in VEC_SEGS:
        w[name] = w["vec"].at[:, off:off + width]
        off += width
    return w


def _gate_in(x, w):
    xn = _rmsnorm(x, w["norm_g"][...]).astype(BF16)
    alow = _mm(xn, w["w_al"][...])
    q = _mm(xn, w["w_q"][...]) * (DKH ** -0.5)
    k = _mm(xn, w["w_k"][...])
    a = _mm(alow.astype(BF16), w["w_alpha"][...]) + w["b_alpha"][...]
    v = _mm(xn, w["w_v"][...])
    u = _mm(xn, w["w_u"][...])
    return xn, q, k, v, u, a


def _log_gate(a):
    return _log_sigmoid(a) * (1.0 / GATE_TAU)


def _dense_out(x, xn, o, z, w, is_last, pre=None):
    if pre is None:
        pre = (_mm(xn, w["w_ga"][...]), _mm(xn, w["w_gb"][...]), _mm(xn, w["w_mg"][...]))
    ga, gb, mg = pre
    gain = w["gla_gain"][...]
    parts = []
    for h in range(HEADS):
        oh = o[:, h * DVH:(h + 1) * DVH]
        ms = jnp.mean(oh * oh, axis=-1, keepdims=True)
        parts.append(oh * lax.rsqrt(ms + EPS) * gain[:, h * DVH:(h + 1) * DVH])
    on = jnp.concatenate(parts, axis=-1) * _silu(ga)
    ya = _mm(on.astype(BF16), w["w_a"][...])

    yp = []
    for g in range(len(POOL_WINDOWS)):
        zg = z[:, g * POOL_GW:(g + 1) * POOL_GW].astype(BF16)
        yp.append(_mm(zg, w["pool_w"][g]))
    y_pool = jnp.concatenate(yp, axis=-1) * w["pool_scale"][...]
    yb = _mm((y_pool * _silu(gb)).astype(BF16), w["w_b"][...])

    gates = _sigmoid(mg + w["b_merge"][...])
    merged = gates[:, :D_MODEL] * ya + gates[:, D_MODEL:] * yb
    h_new = x + _mm(merged.astype(BF16), w["w_out"][...])
    y = _rmsnorm(h_new, w["final_g"][...]) if is_last else None
    return h_new, y


def _window_sums(e):
    p1 = e + pltpu.roll(e, 1, 0)
    t = p1[:, POOL_GW:]
    p2 = t + pltpu.roll(t, 2, 0)
    t = p2[:, POOL_GW:]
    p3 = t + pltpu.roll(t, 4, 0)
    t = p3[:, POOL_GW:]
    p4 = t + pltpu.roll(t, 8, 0)
    return jnp.concatenate([p1[:, :POOL_GW], p2[:, :POOL_GW], p3[:, :POOL_GW], p4], axis=-1)


def _split_hi_lo(x):
    hi = x.astype(BF16)
    lo = (x - hi.astype(F32)).astype(BF16)
    return hi, lo


def _cumsum_rows(x, n):
    pos = lax.broadcasted_iota(jnp.int32, (x.shape[0], 1), 0) % n
    step = 1
    while step < n:
        x = x + jnp.where(pos >= step, pltpu.roll(x, step, 0), 0.0)
        step *= 2
    return x


def _seq_kernel(*refs, tb, chunk, layer, is_last):
    n_w = len(W_NAMES)
    n_in = 5 if layer else 4
    h_ref, s0_ref, buf0_ref, cnt_ref = refs[:4]
    w = _weight_views(refs[n_in:n_in + n_w])
    hy_ref, sall_ref, bufn_ref, o_s, ext_s, sc_s, b_s, k_s = refs[n_in + n_w:]
    s_ref = sall_ref.at[layer]

    t = pl.program_id(1)

    @pl.when(t == 0)
    def _():
        s_ref[0] = s0_ref[0]
        ext_s[0:POOL_PAD, :] = buf0_ref[0]
        if layer:
            sall_ref[0:layer] = refs[4][...]

    x = h_ref[0]
    xn, q, k, v, u, a_gate = _gate_in(x, w)
    n_chunks = tb // chunk
    chunks = [(c, h) for c in range(n_chunks) for h in range(HEADS)]

    def sl(c, h):
        return (slice(c * chunk, (c + 1) * chunk), slice(h * DKH, (h + 1) * DKH),
                slice(h * DVH, (h + 1) * DVH))

    neg = jnp.minimum(a_gate, 0.0)
    steepest = jnp.sum(neg[0:chunk], axis=0, keepdims=True)
    for c in range(1, n_chunks):
        steepest = jnp.minimum(
            steepest, jnp.sum(neg[c * chunk:(c + 1) * chunk], axis=0, keepdims=True))
    decay_bound = (jnp.min(steepest) - chunk * np.log(2.0)) * (1.0 / GATE_TAU)
    factored_ok = decay_bound >= -MAX_FACTORED_LOG_DECAY
    la = _log_gate(a_gate)
    vb = v.astype(BF16)
    causal = (lax.broadcasted_iota(jnp.int32, (chunk, chunk), 0)
              >= lax.broadcasted_iota(jnp.int32, (chunk, chunk), 1))
    tril = causal.astype(BF16)
    eye = _eye(DKH)
    la_hi, la_lo = _split_hi_lo(la)
    b = jnp.concatenate(
        [_mm(tril, la_hi[c * chunk:(c + 1) * chunk])
         + _mm(tril, la_lo[c * chunk:(c + 1) * chunk]) for c in range(n_chunks)], axis=0)
    ga = _mm(xn, w["w_ga"][...])
    b_ends = [b[(c + 1) * chunk - 1:(c + 1) * chunk, :] for c in range(n_chunks)]
    to_end = jnp.concatenate(
        [b_ends[c] - b[c * chunk:(c + 1) * chunk, :] for c in range(n_chunks)], axis=0)
    qd = (q * jnp.exp(b)).astype(BF16)
    kd = (k * jnp.exp(to_end)).astype(BF16)

    kinv = (k * jnp.exp(-b)).astype(BF16)
    a_m = {}
    for c, h in chunks:
        rows, ks, _ = sl(c, h)
        a = _mm_nt(qd[rows, ks], kinv[rows, ks])
        a_m[c, h] = jnp.where(causal, a, 0.0).astype(BF16)
    gb = _mm(xn, w["w_gb"][...])
    d_s, dcol = {}, {}
    for c, h in chunks:
        rows, ks, vs = sl(c, h)
        d_s[c, h] = _mm_tn(kd[rows, ks], vb[rows, vs])
        dcol[c, h] = _row_to_col(jnp.exp(b_ends[c])[:, ks], eye)
    mg = _mm(xn, w["w_mg"][...])

    s_cur = [s_ref[0, h] for h in range(HEADS)]
    for c, h in chunks:
        rows, ks, vs = sl(c, h)
        s_in = s_cur[h].astype(BF16)
        sc_s[c, h] = s_in
        lhs = jnp.concatenate([a_m[c, h], qd[rows, ks]], axis=1)
        rhs = jnp.concatenate([vb[rows, vs], s_in], axis=0)
        o_s[rows, vs] = _mm(lhs, rhs)
        s_cur[h] = dcol[c, h] * s_cur[h] + d_s[c, h]
    for h in range(HEADS):
        s_ref[0, h] = s_cur[h]

    ext_s[POOL_PAD:POOL_PAD + tb, :] = u
    z = _window_sums(ext_s[...])[POOL_PAD:, :] * cnt_ref[...] - u
    bufn_ref[0] = ext_s[tb:tb + POOL_PAD, :]
    ext_s[0:POOL_PAD, :] = ext_s[tb:tb + POOL_PAD, :]

    h_new, y = _dense_out(x, xn, o_s[...], z, w, is_last, pre=(ga, gb, mg))
    hy_ref[0] = y if is_last else h_new

    @pl.when(jnp.logical_not(factored_ok))
    def _():
        b_s[...] = b
        k_s[...] = k
        col_id = lax.broadcasted_iota(jnp.int32, (chunk, chunk), 1)
        for c, h in chunks:
            rows, ks, vs = sl(c, h)
            qh = q[rows, ks]
            bh = b[rows, ks]

            def columns(g, acc, c=c, ks=ks, qh=qh, bh=bh):
                r0 = pl.multiple_of(c * chunk + g * SUBLANES, SUBLANES)
                b_tile = b_s[pl.ds(r0, SUBLANES), ks]
                k_tile = k_s[pl.ds(r0, SUBLANES), ks]
                for r in range(SUBLANES):
                    decay = jnp.exp(jnp.minimum(bh - b_tile[r:r + 1, :], 0.0))
                    col = jnp.sum(qh * k_tile[r:r + 1, :] * decay, axis=1, keepdims=True)
                    acc = jnp.where(col_id == g * SUBLANES + r, col, acc)
                return acc

            a = lax.fori_loop(0, chunk // SUBLANES, columns, jnp.zeros((chunk, chunk), F32))
            a = jnp.where(causal, a, 0.0).astype(BF16)
            o_s[rows, vs] = _mm(qd[rows, ks], sc_s[c, h]) + _mm(a, vb[rows, vs])
        h_fix, y_fix = _dense_out(x, xn, o_s[...], z, w, is_last, pre=(ga, gb, mg))
        hy_ref[0] = y_fix if is_last else h_fix


def _const_spec(shape):
    nd = len(shape)
    return pl.BlockSpec(shape, lambda *_: (0,) * nd, pipeline_mode=pl.Buffered(1))


def _weight_specs(ws, layer_of):
    specs = []
    for name in W_NAMES:
        shape = ws[name].shape
        if name == "final_g":
            specs.append(_const_spec(shape))
        else:
            zeros = (0,) * (len(shape) - 1)
            specs.append(pl.BlockSpec((None,) + shape[1:],
                                      lambda *g, zeros=zeros: (layer_of(*g),) + zeros,
                                      pipeline_mode=pl.Buffered(1)))
    return specs


def _seq_layer(h, s_meta, buf_meta, inv_cnt, s_prev, ws, layer, *, is_last):
    bn, t_len, _ = h.shape
    tb = min(SEQ_BLOCK, t_len)
    chunk = min(SEQ_CHUNK, tb)
    assert t_len % tb == 0 and tb % chunk == 0 and buf_meta.shape[1] == POOL_PAD
    weights = [ws[n] for n in W_NAMES]
    in_specs = [
        pl.BlockSpec((1, tb, D_MODEL), lambda b, t: (b, t, 0)),
        pl.BlockSpec((1, HEADS, DKH, DVH), lambda b, t: (layer, 0, 0, 0)),
        pl.BlockSpec((1, POOL_PAD, POOL_W), lambda b, t: (layer, 0, 0)),
        _const_spec(inv_cnt.shape),
    ]
    args = [h, s_meta, buf_meta, inv_cnt]
    if layer:
        in_specs.append(pl.BlockSpec((layer, 1, HEADS, DKH, DVH), lambda b, t: (0, b, 0, 0, 0)))
        args.append(s_prev)
    in_specs += _weight_specs(ws, lambda b, t: layer)
    out_shape = [jax.ShapeDtypeStruct(h.shape, F32),
                 jax.ShapeDtypeStruct((layer + 1, bn, HEADS, DKH, DVH), F32),
                 jax.ShapeDtypeStruct((bn, POOL_PAD, POOL_W), F32)]
    out_specs = [pl.BlockSpec((1, tb, D_MODEL), lambda b, t: (b, t, 0)),
                 pl.BlockSpec((layer + 1, 1, HEADS, DKH, DVH), lambda b, t: (0, b, 0, 0, 0)),
                 pl.BlockSpec((1, POOL_PAD, POOL_W), lambda b, t: (b, 0, 0))]
    kern = functools.partial(_seq_kernel, tb=tb, chunk=chunk, layer=layer, is_last=is_last)
    return pl.pallas_call(
        kern,
        grid=(bn, t_len // tb),
        in_specs=in_specs,
        out_specs=out_specs,
        out_shape=out_shape,
        scratch_shapes=[pltpu.VMEM((tb, DV), F32), pltpu.VMEM((tb + POOL_PAD, POOL_W), F32),
                        pltpu.VMEM((tb // chunk, HEADS, DKH, DVH), BF16),
                        pltpu.VMEM((tb, DK), F32), pltpu.VMEM((tb, DK), F32)],
        compiler_params=pltpu.CompilerParams(
            dimension_semantics=("arbitrary", "arbitrary"), vmem_limit_bytes=VMEM_LIMIT_BYTES),
        name="seq_layer",
    )(*args, *weights)


def _intra_by_distance(q, k, v, b, tok, n_tok):
    o_parts = []
    for h in range(HEADS):
        ks = slice(h * DKH, (h + 1) * DKH)
        vs = slice(h * DVH, (h + 1) * DVH)
        o = jnp.sum(q[:, ks] * k[:, ks], axis=1, keepdims=True) * v[:, vs]
        for d in range(1, n_tok):
            kj = pltpu.roll(k[:, ks], d, 0)
            bj = pltpu.roll(b[:, ks], d, 0)
            score = jnp.sum(q[:, ks] * kj * jnp.exp(jnp.minimum(b[:, ks] - bj, 0.0)), axis=1,
                            keepdims=True)
            o = o + jnp.where(tok >= d, score, 0.0) * pltpu.roll(v[:, vs], d, 0)
        o_parts.append(o)
    return o_parts


def _meta_layer(xm, cnt, w):
    n = xm.shape[0]
    xn, q, k, v, u, a_gate = _gate_in(xm, w)
    b = _cumsum_rows(_log_gate(a_gate), n)
    tok = lax.broadcasted_iota(jnp.int32, (n, 1), 0)
    o = jnp.concatenate(_intra_by_distance(q, k, v, b, tok, n), axis=-1)
    kd = (k * jnp.exp(b[n - 1:n, :] - b)).astype(BF16)
    vb = v.astype(BF16)
    state = [_mm_tn(kd[:, h * DKH:(h + 1) * DKH], vb[:, h * DVH:(h + 1) * DVH])
             for h in range(HEADS)]
    ext = jnp.concatenate([jnp.zeros((POOL_PAD, POOL_W), F32), u], axis=0)
    z = _window_sums(ext)[POOL_PAD:, :] / cnt - u
    h_new, _ = _dense_out(xm, xn, o, z, w, False)
    return h_new, state, u
def _step_kernel(*refs, n_tok, sub, depth):
    n_w = len(W_NAMES)
    n_in = 6
    x_ref, s0_ref, buf0_ref, cnt_ref, meta_ref, cntm_ref = refs[:n_in]
    w = _weight_views(refs[n_in:n_in + n_w])
    y_ref, s_ref, bufn_ref, smeta_ref, bufmeta_ref = refs[n_in + n_w:n_in + n_w + 5]
    h_s, hm_s, xin_s, xn_s, q_s, k_s, v_s, la_s, u_s, o_s, z_s = refs[n_in + n_w + 5:]

    layer = pl.program_id(0)
    blk = pl.program_id(1)
    j = pl.program_id(2)
    nj = pl.num_programs(2)
    brows = xin_s.shape[0]
    blk_rows = pl.ds(pl.multiple_of(blk * brows, brows), brows)
    assert sub % SUBLANES == 0

    @pl.when((blk == 0) & (j == 0))
    def _():
        @pl.when(layer == 0)
        def _():
            hm_s[...] = meta_ref[...]

        hm_new, state, u_meta = _meta_layer(hm_s[...], cntm_ref[...], w)
        hm_s[...] = hm_new
        for h in range(HEADS):
            smeta_ref[h] = state[h]
        bufmeta_ref[...] = u_meta

    @pl.when(j == 0)
    def _():
        @pl.when(layer == 0)
        def _():
            xin_s[...] = x_ref[...].reshape(xin_s.shape)

        @pl.when(layer > 0)
        def _():
            xin_s[...] = h_s[blk_rows, :]

        xn, q, k, v, u, a_gate = _gate_in(xin_s[...], w)
        la = _log_gate(a_gate)
        xn_s[...] = xn
        q_s[...] = q
        k_s[...] = k
        v_s[...] = v
        la_s[...] = la
        for g in range(len(POOL_WINDOWS)):
            u_s[g] = u[:, g * POOL_GW:(g + 1) * POOL_GW]

    nr = sub * n_tok
    rows = pl.ds(pl.multiple_of(j * nr, nr), nr)
    row = lax.broadcasted_iota(jnp.int32, (nr, 1), 0)
    tok = row % n_tok
    seq = row // n_tok
    eye = _eye(DKH)
    b = _cumsum_rows(la_s[rows, :], n_tok)
    b_end = b
    for d in range(1, n_tok):
        b_end = jnp.where(tok == n_tok - 1 - d, pltpu.roll(b, nr - d, 0), b_end)
    q = q_s[rows, :]
    k = k_s[rows, :]
    v = v_s[rows, :]
    qd = (q * jnp.exp(b)).astype(BF16)
    kd = k * jnp.exp(b_end - b)
    dec = jnp.exp(b_end)
    vb = v.astype(BF16)
    o_parts = _intra_by_distance(q, k, v, b, tok, n_tok)
    for h in range(HEADS):
        ks = slice(h * DKH, (h + 1) * DKH)
        vs = slice(h * DVH, (h + 1) * DVH)
        o = o_parts[h]
        for s in range(sub):
            mine = seq == s
            s_old = s0_ref[s, h]
            o = o + jnp.where(mine, _mm(qd[:, ks], s_old.astype(BF16)), 0.0)
            kd_s = jnp.where(mine, kd[:, ks], 0.0).astype(BF16)
            dcol = _row_to_col(dec[s * n_tok:s * n_tok + 1, ks], eye)
            s_ref[s, h] = dcol * s_old + _mm_tn(kd_s, vb[:, vs])
        o_parts[h] = o
    o_s[rows, :] = jnp.concatenate(o_parts, axis=-1)

    inv_cnt = cnt_ref[...]
    base = pl.multiple_of(j * nr, nr)
    hist = [buf0_ref[i] for i in range(POOL_BUF)]
    new = []
    for t in range(n_tok):
        rows_t = pl.ds(base + t, sub, stride=n_tok)
        new.append(jnp.concatenate([u_s[g, rows_t, :] for g in range(len(POOL_WINDOWS))], axis=-1))
    ext = hist + new
    for t in range(n_tok):
        last = POOL_BUF + t
        rows_t = pl.ds(base + t, sub, stride=n_tok)
        for g, width in enumerate(POOL_WINDOWS):
            cols = slice(g * POOL_GW, (g + 1) * POOL_GW)
            acc = ext[last][:, cols]
            for back in range(1, width):
                acc = acc + ext[last - back][:, cols]
            z_s[g, rows_t, :] = acc * inv_cnt[:, cols] - new[t][:, cols]
    for i in range(POOL_BUF):
        bufn_ref[i] = ext[i + n_tok]

    @pl.when(j == nj - 1)
    def _():
        z = jnp.concatenate([z_s[g] for g in range(len(POOL_WINDOWS))], axis=-1)
        h_new, y = _dense_out(xin_s[...], xn_s[...], o_s[...], z, w, True)

        @pl.when(layer < depth - 1)
        def _():
            h_s[blk_rows, :] = h_new

        @pl.when(layer == depth - 1)
        def _():
            y_ref[...] = y.reshape(y_ref.shape)


def _step_layers(x, s_all, buf_all, cnt, meta, cnt_meta, ws):
    bn, n_tok, _ = x.shape
    rows = bn * n_tok
    depth = s_all.shape[0]
    assert s_all.shape[1] == bn and buf_all.shape == (depth, POOL_BUF, bn, POOL_W)
    sub, blk = SAMPLE_SUB, SAMPLE_BLOCK
    assert bn % blk == 0 and blk % sub == 0
    n_blk, n_inner = bn // blk, blk // sub
    brows = blk * n_tok
    weights = [ws[n] for n in W_NAMES]

    def state_idx(l, i, j):
        return (l, i * n_inner + j, 0, 0, 0)

    def buf_idx(l, i, j):
        return (l, 0, i * n_inner + j, 0)

    in_specs = [
        pl.BlockSpec((blk, n_tok, D_MODEL), lambda l, i, j: (i, 0, 0)),
        pl.BlockSpec((None, sub, HEADS, DKH, DVH), state_idx),
        pl.BlockSpec((None, POOL_BUF, sub, POOL_W), buf_idx),
        _const_spec(cnt.shape),
        _const_spec(meta.shape),
        _const_spec(cnt_meta.shape),
    ] + _weight_specs(ws, lambda l, i, j: l)
    n_meta = meta.shape[0]
    out_shape = [jax.ShapeDtypeStruct(x.shape, F32),
                 jax.ShapeDtypeStruct(s_all.shape, F32),
                 jax.ShapeDtypeStruct(buf_all.shape, F32),
                 jax.ShapeDtypeStruct((depth, HEADS, DKH, DVH), F32),
                 jax.ShapeDtypeStruct((depth, n_meta, POOL_W), F32)]
    out_specs = [pl.BlockSpec((blk, n_tok, D_MODEL),
                              lambda l, i, j: (jnp.where(l == depth - 1, i, 0), 0, 0)),
                 pl.BlockSpec((None, sub, HEADS, DKH, DVH), state_idx),
                 pl.BlockSpec((None, POOL_BUF, sub, POOL_W), buf_idx),
                 pl.BlockSpec((None, HEADS, DKH, DVH), lambda l, i, j: (l, 0, 0, 0)),
                 pl.BlockSpec((None, n_meta, POOL_W), lambda l, i, j: (l, 0, 0))]
    kern = functools.partial(_step_kernel, n_tok=n_tok, sub=sub, depth=depth)
    scratch = [pltpu.VMEM((rows, D_MODEL), F32), pltpu.VMEM((n_meta, D_MODEL), F32),
               pltpu.VMEM((brows, D_MODEL), F32),
               pltpu.VMEM((brows, D_MODEL), BF16), pltpu.VMEM((brows, DK), F32),
               pltpu.VMEM((brows, DK), F32), pltpu.VMEM((brows, DV), F32),
               pltpu.VMEM((brows, DK), F32), pltpu.VMEM((len(POOL_WINDOWS), brows, POOL_GW), F32),
               pltpu.VMEM((brows, DV), F32), pltpu.VMEM((len(POOL_WINDOWS), brows, POOL_GW), F32)]
    return pl.pallas_call(
        kern,
        grid=(depth, n_blk, n_inner),
        in_specs=in_specs,
        out_specs=out_specs,
        out_shape=out_shape,
        scratch_shapes=scratch,
        compiler_params=pltpu.CompilerParams(
            dimension_semantics=("arbitrary", "arbitrary", "arbitrary"),
            vmem_limit_bytes=VMEM_LIMIT_BYTES),
        name="step_layers",
    )(x, s_all, buf_all, cnt, meta, cnt_meta, *weights)


def _prep_main_kernel(wt_ref, main_ref):
    main_ref[...] = wt_ref[...].T.astype(BF16)


def _prep_tail_kernel(wt_ref, al_ref, mg_ref):
    lane = lax.broadcasted_iota(jnp.int32, al_ref.shape, 1)
    al_ref[...] = jnp.where(lane < GATE_RANK, wt_ref[0, 0:LANES, :].T, 0.0).astype(BF16)
    mg_ref[...] = wt_ref[0, GATE_RANK:, :].T.astype(BF16)


def _prep_w_in(w_in):
    depth, rows, cols = w_in.shape
    n_tail = cols - MAIN_COLS
    wt = jnp.swapaxes(w_in, 1, 2)
    cb = PREP_COLS
    assert MAIN_COLS % cb == 0
    w_main = pl.pallas_call(
        _prep_main_kernel,
        grid=(depth, MAIN_COLS // cb),
        in_specs=[pl.BlockSpec((None, cb, rows), lambda l, i: (l, i, 0))],
        out_specs=pl.BlockSpec((None, rows, cb), lambda l, i: (l, 0, i)),
        out_shape=jax.ShapeDtypeStruct((depth, rows, MAIN_COLS), BF16),
        name="prep_w_main",
    )(wt)
    w_al, w_mg = pl.pallas_call(
        _prep_tail_kernel,
        grid=(depth,),
        in_specs=[pl.BlockSpec((pl.Element(1), pl.Element(n_tail), pl.Element(rows)),
                               lambda l: (l, MAIN_COLS, 0))],
        out_specs=[pl.BlockSpec((None, rows, LANES), lambda l: (l, 0, 0)),
                   pl.BlockSpec((None, rows, n_tail - GATE_RANK), lambda l: (l, 0, 0))],
        out_shape=[jax.ShapeDtypeStruct((depth, rows, LANES), BF16),
                   jax.ShapeDtypeStruct((depth, rows, n_tail - GATE_RANK), BF16)],
        compiler_params=pltpu.CompilerParams(vmem_limit_bytes=VMEM_LIMIT_BYTES),
        name="prep_w_tail",
    )(wt)
    return w_main, w_al, w_mg


def _stacked_weights(norm_g, w_in, w_alpha, b_alpha, gla_gain, w_a, pool_w, pool_scale, w_b,
                     b_merge, w_out, final_norm_g):
    w_main, w_al, w_mg = _prep_w_in(w_in)
    pad = LANES - GATE_RANK
    vectors = {"norm_g": norm_g, "b_alpha": b_alpha, "gla_gain": gla_gain,
               "pool_scale": pool_scale, "b_merge": b_merge}
    assert all(vectors[name].shape[-1] == width for name, width in VEC_SEGS)
    return {
        "vec": jnp.concatenate([vectors[name] for name, _ in VEC_SEGS], axis=-1)[:, None, :],
        "w_main": w_main, "w_al": w_al, "w_mg": w_mg,
        "w_alpha": jnp.pad(w_alpha.astype(BF16), ((0, 0), (0, pad), (0, 0))),
        "w_a": w_a.astype(BF16),
        "pool_w": pool_w.astype(BF16),
        "w_b": w_b.astype(BF16),
        "w_out": w_out.astype(BF16),
        "final_g": final_norm_g[None, :],
    }


def _window_row():
    return np.repeat(np.asarray(POOL_WINDOWS, np.float32), POOL_GW)[None, :]


def kernel(x_prompt, x_sample, state_gla, state_pool, meta_tokens, norm_g, w_in, w_alpha, b_alpha,
           gla_gain, w_a, pool_w, pool_scale, w_b, b_merge, w_out, final_norm_g):
    depth = w_in.shape[0]
    wrow = _window_row()
    cnt_meta = jnp.asarray(np.minimum(wrow, np.arange(1, N_META + 1, dtype=np.float32)[:, None]))
    inv_w = jnp.asarray(1.0 / wrow)

    ws = _stacked_weights(norm_g, w_in, w_alpha, b_alpha, gla_gain, w_a, pool_w, pool_scale, w_b,
                          b_merge, w_out, final_norm_g)
    buf_s = jnp.swapaxes(state_pool, 1, 2)
    y_s, sg_s, bufn_s, s_meta, buf_meta = _step_layers(
        x_sample, state_gla, buf_s, inv_w, meta_tokens.astype(F32), cnt_meta, ws)

    h_p = x_prompt
    sg_p, sp_p = None, []
    for l in range(depth):
        h_p, sg_p, b_p = _seq_layer(h_p, s_meta, buf_meta, inv_w, sg_p, ws, l,
                                    is_last=l == depth - 1)
        sp_p.append(b_p[:, POOL_PAD - POOL_BUF:, :])
    return (h_p, y_s, sg_p, jnp.stack(sp_p), sg_s, jnp.swapaxes(bufn_s, 1, 2))
```

```python
import functools

import numpy as np
import jax
import jax.numpy as jnp
from jax import lax
from jax.experimental import pallas as pl
from jax.experimental.pallas import tpu as pltpu

F32 = jnp.float32
BF16 = jnp.bfloat16

D_MODEL = 1024
N_META = 16
HEADS = 4
DKH = 128
DVH = 256
DK = HEADS * DKH
DV = HEADS * DVH
GATE_RANK = 16
GATE_TAU = 16.0
POOL_W = 512
POOL_WINDOWS = (2, 4, 8, 16)
POOL_GW = POOL_W // len(POOL_WINDOWS)
POOL_BUF = max(POOL_WINDOWS) - 1
POOL_PAD = POOL_BUF + 1
EPS = 1e-6
IN_SIZES = (DK, DK, DV, DV, POOL_W, POOL_W, GATE_RANK, 2 * D_MODEL)
IN_OFFS = tuple(int(o) for o in np.cumsum((0,) + IN_SIZES))

LANES = 128
SUBLANES = 8
VMEM_LIMIT_BYTES = 56 * 1024 * 1024

MAX_FACTORED_LOG_DECAY = 60.0

SEQ_BLOCK = 512
SEQ_CHUNK = 256
SAMPLE_BLOCK = 64
SAMPLE_SUB = 8
PREP_COLS = 512


def _mm(a, b):
    return jnp.dot(a, b, preferred_element_type=F32)


def _mm_nt(a, b):
    return lax.dot_general(a, b, (((1,), (1,)), ((), ())), preferred_element_type=F32)


def _mm_tn(a, b):
    return lax.dot_general(a, b, (((0,), (0,)), ((), ())), preferred_element_type=F32)


def _sigmoid(x):
    return 1.0 / (1.0 + jnp.exp(-x))


def _silu(x):
    return x * _sigmoid(x)


def _log_sigmoid(x):
    return jnp.minimum(x, 0.0) - jnp.log1p(jnp.exp(-jnp.abs(x)))


def _rmsnorm(x, g):
    ms = jnp.mean(x * x, axis=-1, keepdims=True)
    return x * lax.rsqrt(ms + EPS) * g


def _row_to_col(row, eye):
    return jnp.sum(eye * row, axis=1, keepdims=True)


def _eye(n):
    r = lax.broadcasted_iota(jnp.int32, (n, n), 0)
    c = lax.broadcasted_iota(jnp.int32, (n, n), 1)
    return (r == c).astype(F32)


W_NAMES = ("vec", "w_main", "w_al", "w_mg", "w_alpha", "w_a", "pool_w", "w_b", "w_out", "final_g")
MAIN_SEGS = ("w_q", "w_k", "w_v", "w_ga", "w_u", "w_gb")
MAIN_COLS = IN_OFFS[len(MAIN_SEGS)]
VEC_SEGS = (("norm_g", D_MODEL), ("b_alpha", DK), ("gla_gain", DV), ("pool_scale", POOL_W),
            ("b_merge", 2 * D_MODEL))


def _weight_views(refs):
    w = dict(zip(W_NAMES, refs))
    for i, name in enumerate(MAIN_SEGS):
        w[name] = w["w_main"].at[:, IN_OFFS[i]:IN_OFFS[i + 1]]
    off = 0
    for name, width in VEC_SEGS:
        w[name] = w["vec"].at[:, off:off + width]
        off += width
    return w


def _gate_in(x, w):
    xn = _rmsnorm(x, w["norm_g"][...]).astype(BF16)
    alow = _mm(xn, w["w_al"][...])
    q = _mm(xn, w["w_q"][...]) * (DKH ** -0.5)
    k = _mm(xn, w["w_k"][...])
    a = _mm(alow.astype(BF16), w["w_alpha"][...]) + w["b_alpha"][...]
    v = _mm(xn, w["w_v"][...])
    u = _mm(xn, w["w_u"][...])
    return xn, q, k, v, u, a


def _log_gate(a):
    return _log_sigmoid(a) * (1.0 / GATE_TAU)


def _dense_out(x, xn, o, z, w, is_last, pre=None):
    if pre is None:
        pre = (_mm(xn, w["w_ga"][...]), _mm(xn, w["w_gb"][...]), _mm(xn, w["w_mg"][...]))
    ga, gb, mg = pre
    gain = w["gla_gain"][...]
    parts = []
    for h in range(HEADS):
        oh = o[:, h * DVH:(h + 1) * DVH]
        ms = jnp.mean(oh * oh, axis=-1, keepdims=True)
        parts.append(oh * lax.rsqrt(ms + EPS) * gain[:, h * DVH:(h + 1) * DVH])
    on = jnp.concatenate(parts, axis=-1) * _silu(ga)
    ya = _mm(on.astype(BF16), w["w_a"][...])

    yp = []
    for g in range(len(POOL_WINDOWS)):
        zg = z[:, g * POOL_GW:(g + 1) * POOL_GW].astype(BF16)
        yp.append(_mm(zg, w["pool_w"][g]))
    y_pool = jnp.concatenate(yp, axis=-1) * w["pool_scale"][...]
    yb = _mm((y_pool * _silu(gb)).astype(BF16), w["w_b"][...])

    gates = _sigmoid(mg + w["b_merge"][...])
    merged = gates[:, :D_MODEL] * ya + gates[:, D_MODEL:] * yb
    h_new = x + _mm(merged.astype(BF16), w["w_out"][...])
    y = _rmsnorm(h_new, w["final_g"][...]) if is_last else None
    return h_new, y


def _window_sums(e):
    p1 = e + pltpu.roll(e, 1, 0)
    t = p1[:, POOL_GW:]
    p2 = t + pltpu.roll(t, 2, 0)
    t = p2[:, POOL_GW:]
    p3 = t + pltpu.roll(t, 4, 0)
    t = p3[:, POOL_GW:]
    p4 = t + pltpu.roll(t, 8, 0)
    return jnp.concatenate([p1[:, :POOL_GW], p2[:, :POOL_GW], p3[:, :POOL_GW], p4], axis=-1)


def _split_hi_lo(x):
    hi = x.astype(BF16)
    lo = (x - hi.astype(F32)).astype(BF16)
    return hi, lo


def _cumsum_rows(x, n):
    pos = lax.broadcasted_iota(jnp.int32, (x.shape[0], 1), 0) % n
    step = 1
    while step < n:
        x = x + jnp.where(pos >= step, pltpu.roll(x, step, 0), 0.0)
        step *= 2
    return x


def _seq_kernel(*refs, tb, chunk, layer, is_last):
    n_w = len(W_NAMES)
    n_in = 5 if layer else 4
    h_ref, s0_ref, buf0_ref, cnt_ref = refs[:4]
    w = _weight_views(refs[n_in:n_in + n_w])
    hy_ref, sall_ref, bufn_ref, o_s, ext_s, sc_s, b_s, k_s = refs[n_in + n_w:]
    s_ref = sall_ref.at[layer]

    t = pl.program_id(1)

    @pl.when(t == 0)
    def _():
        s_ref[0] = s0_ref[0]
        ext_s[0:POOL_PAD, :] = buf0_ref[0]
        if layer:
            sall_ref[0:layer] = refs[4][...]

    x = h_ref[0]
    xn, q, k, v, u, a_gate = _gate_in(x, w)
    n_chunks = tb // chunk
    chunks = [(c, h) for c in range(n_chunks) for h in range(HEADS)]

    def sl(c, h):
        return (slice(c * chunk, (c + 1) * chunk), slice(h * DKH, (h + 1) * DKH),
                slice(h * DVH, (h + 1) * DVH))

    neg = jnp.minimum(a_gate, 0.0)
    steepest = jnp.sum(neg[0:chunk], axis=0, keepdims=True)
    for c in range(1, n_chunks):
        steepest = jnp.minimum(
            steepest, jnp.sum(neg[c * chunk:(c + 1) * chunk], axis=0, keepdims=True))
    decay_bound = (jnp.min(steepest) - chunk * np.log(2.0)) * (1.0 / GATE_TAU)
    factored_ok = decay_bound >= -MAX_FACTORED_LOG_DECAY
    la = _log_gate(a_gate)
    vb = v.astype(BF16)
    causal = (lax.broadcasted_iota(jnp.int32, (chunk, chunk), 0)
              >= lax.broadcasted_iota(jnp.int32, (chunk, chunk), 1))
    tril = causal.astype(BF16)
    eye = _eye(DKH)
    la_hi, la_lo = _split_hi_lo(la)
    b = jnp.concatenate(
        [_mm(tril, la_hi[c * chunk:(c + 1) * chunk])
         + _mm(tril, la_lo[c * chunk:(c + 1) * chunk]) for c in range(n_chunks)], axis=0)
    ga = _mm(xn, w["w_ga"][...])
    b_ends = [b[(c + 1) * chunk - 1:(c + 1) * chunk, :] for c in range(n_chunks)]
    to_end = jnp.concatenate(
        [b_ends[c] - b[c * chunk:(c + 1) * chunk, :] for c in range(n_chunks)], axis=0)
    qd = (q * jnp.exp(b)).astype(BF16)
    kd = (k * jnp.exp(to_end)).astype(BF16)

    kinv = (k * jnp.exp(-b)).astype(BF16)
    a_m = {}
    for c, h in chunks:
        rows, ks, _ = sl(c, h)
        a = _mm_nt(qd[rows, ks], kinv[rows, ks])
        a_m[c, h] = jnp.where(causal, a, 0.0).astype(BF16)
    gb = _mm(xn, w["w_gb"][...])
    d_s, dcol = {}, {}
    for c, h in chunks:
        rows, ks, vs = sl(c, h)
        d_s[c, h] = _mm_tn(kd[rows, ks], vb[rows, vs])
        dcol[c, h] = _row_to_col(jnp.exp(b_ends[c])[:, ks], eye)
    mg = _mm(xn, w["w_mg"][...])

    s_cur = [s_ref[0, h] for h in range(HEADS)]
    for c, h in chunks:
        rows, ks, vs = sl(c, h)
        s_in = s_cur[h].astype(BF16)
        sc_s[c, h] = s_in
        lhs = jnp.concatenate([a_m[c, h], qd[rows, ks]], axis=1)
        rhs = jnp.concatenate([vb[rows, vs], s_in], axis=0)
        o_s[rows, vs] = _mm(lhs, rhs)
        s_cur[h] = dcol[c, h] * s_cur[h] + d_s[c, h]
    for h in range(HEADS):
        s_ref[0, h] = s_cur[h]

    ext_s[POOL_PAD:POOL_PAD + tb, :] = u
    z = _window_sums(ext_s[...])[POOL_PAD:, :] * cnt_ref[...] - u
    bufn_ref[0] = ext_s[tb:tb + POOL_PAD, :]
    ext_s[0:POOL_PAD, :] = ext_s[tb:tb + POOL_PAD, :]

    h_new, y = _dense_out(x, xn, o_s[...], z, w, is_last, pre=(ga, gb, mg))
    hy_ref[0] = y if is_last else h_new

    @pl.when(jnp.logical_not(factored_ok))
    def _():
        b_s[...] = b
        k_s[...] = k
        col_id = lax.broadcasted_iota(jnp.int32, (chunk, chunk), 1)
        for c, h in chunks:
            rows, ks, vs = sl(c, h)
            qh = q[rows, ks]
            bh = b[rows, ks]

            def columns(g, acc, c=c, ks=ks, qh=qh, bh=bh):
                r0 = pl.multiple_of(c * chunk + g * SUBLANES, SUBLANES)
                b_tile = b_s[pl.ds(r0, SUBLANES), ks]
                k_tile = k_s[pl.ds(r0, SUBLANES), ks]
                for r in range(SUBLANES):
                    decay = jnp.exp(jnp.minimum(bh - b_tile[r:r + 1, :], 0.0))
                    col = jnp.sum(qh * k_tile[r:r + 1, :] * decay, axis=1, keepdims=True)
                    acc = jnp.where(col_id == g * SUBLANES + r, col, acc)
                return acc

            a = lax.fori_loop(0, chunk // SUBLANES, columns, jnp.zeros((chunk, chunk), F32))
            a = jnp.where(causal, a, 0.0).astype(BF16)
            o_s[rows, vs] = _mm(qd[rows, ks], sc_s[c, h]) + _mm(a, vb[rows, vs])
        h_fix, y_fix = _dense_out(x, xn, o_s[...], z, w, is_last, pre=(ga, gb, mg))
        hy_ref[0] = y_fix if is_last else h_fix


def _const_spec(shape):
    nd = len(shape)
    return pl.BlockSpec(shape, lambda *_: (0,) * nd, pipeline_mode=pl.Buffered(1))


def _weight_specs(ws, layer_of):
    specs = []
    for name in W_NAMES:
        shape = ws[name].shape
        if name == "final_g":
            specs.append(_const_spec(shape))
        else:
            zeros = (0,) * (len(shape) - 1)
            specs.append(pl.BlockSpec((None,) + shape[1:],
                                      lambda *g, zeros=zeros: (layer_of(*g),) + zeros,
                                      pipeline_mode=pl.Buffered(1)))
    return specs


def _seq_layer(h, s_meta, buf_meta, inv_cnt, s_prev, ws, layer, *, is_last):
    bn, t_len, _ = h.shape
    tb = min(SEQ_BLOCK, t_len)
    chunk = min(SEQ_CHUNK, tb)
    assert t_len % tb == 0 and tb % chunk == 0 and buf_meta.shape[1] == POOL_PAD
    weights = [ws[n] for n in W_NAMES]
    in_specs = [
        pl.BlockSpec((1, tb, D_MODEL), lambda b, t: (b, t, 0)),
        pl.BlockSpec((1, HEADS, DKH, DVH), lambda b, t: (layer, 0, 0, 0)),
        pl.BlockSpec((1, POOL_PAD, POOL_W), lambda b, t: (layer, 0, 0)),
        _const_spec(inv_cnt.shape),
    ]
    args = [h, s_meta, buf_meta, inv_cnt]
    if layer:
        in_specs.append(pl.BlockSpec((layer, 1, HEADS, DKH, DVH), lambda b, t: (0, b, 0, 0, 0)))
        args.append(s_prev)
    in_specs += _weight_specs(ws, lambda b, t: layer)
    out_shape = [jax.ShapeDtypeStruct(h.shape, F32),
                 jax.ShapeDtypeStruct((layer + 1, bn, HEADS, DKH, DVH), F32),
                 jax.ShapeDtypeStruct((bn, POOL_PAD, POOL_W), F32)]
    out_specs = [pl.BlockSpec((1, tb, D_MODEL), lambda b, t: (b, t, 0)),
                 pl.BlockSpec((layer + 1, 1, HEADS, DKH, DVH), lambda b, t: (0, b, 0, 0, 0)),
                 pl.BlockSpec((1, POOL_PAD, POOL_W), lambda b, t: (b, 0, 0))]
    kern = functools.partial(_seq_kernel, tb=tb, chunk=chunk, layer=layer, is_last=is_last)
    return pl.pallas_call(
        kern,
        grid=(bn, t_len // tb),
        in_specs=in_specs,
        out_specs=out_specs,
        out_shape=out_shape,
        scratch_shapes=[pltpu.VMEM((tb, DV), F32), pltpu.VMEM((tb + POOL_PAD, POOL_W), F32),
                        pltpu.VMEM((tb // chunk, HEADS, DKH, DVH), BF16),
                        pltpu.VMEM((tb, DK), F32), pltpu.VMEM((tb, DK), F32)],
        compiler_params=pltpu.CompilerParams(
            dimension_semantics=("arbitrary", "arbitrary"), vmem_limit_bytes=VMEM_LIMIT_BYTES),
        name="seq_layer",
    )(*args, *weights)


def _intra_by_distance(q, k, v, b, tok, n_tok):
    o_parts = []
    for h in range(HEADS):
        ks = slice(h * DKH, (h + 1) * DKH)
        vs = slice(h * DVH, (h + 1) * DVH)
        o = jnp.sum(q[:, ks] * k[:, ks], axis=1, keepdims=True) * v[:, vs]
        for d in range(1, n_tok):
            kj = pltpu.roll(k[:, ks], d, 0)
            bj = pltpu.roll(b[:, ks], d, 0)
            score = jnp.sum(q[:, ks] * kj * jnp.exp(jnp.minimum(b[:, ks] - bj, 0.0)), axis=1,
                            keepdims=True)
            o = o + jnp.where(tok >= d, score, 0.0) * pltpu.roll(v[:, vs], d, 0)
        o_parts.append(o)
    return o_parts


def _meta_layer(xm, cnt, w):
    n = xm.shape[0]
    xn, q, k, v, u, a_gate = _gate_in(xm, w)
    b = _cumsum_rows(_log_gate(a_gate), n)
    tok = lax.broadcasted_iota(jnp.int32, (n, 1), 0)
    o = jnp.concatenate(_intra_by_distance(q, k, v, b, tok, n), axis=-1)
    kd = (k * jnp.exp(b[n - 1:n, :] - b)).astype(BF16)
    vb = v.astype(BF16)
    state = [_mm_tn(kd[:, h * DKH:(h + 1) * DKH], vb[:, h * DVH:(h + 1) * DVH])
             for h in range(HEADS)]
    ext = jnp.concatenate([jnp.zeros((POOL_PAD, POOL_W), F32), u], axis=0)
    z = _window_sums(ext)[POOL_PAD:, :] / cnt - u
    h_new, _ = _dense_out(xm, xn, o, z, w, False)
    return h_new, state, u
def _step_kernel(*refs, n_tok, sub, depth):
    n_w = len(W_NAMES)
    n_in = 6
    x_ref, s0_ref, buf0_ref, cnt_ref, meta_ref, cntm_ref = refs[:n_in]
    w = _weight_views(refs[n_in:n_in + n_w])
    y_ref, s_ref, bufn_ref, smeta_ref, bufmeta_ref = refs[n_in + n_w:n_in + n_w + 5]
    h_s, hm_s, xin_s, xn_s, q_s, k_s, v_s, la_s, u_s, o_s, z_s = refs[n_in + n_w + 5:]

    layer = pl.program_id(0)
    blk = pl.program_id(1)
    j = pl.program_id(2)
    nj = pl.num_programs(2)
    brows = xin_s.shape[0]
    blk_rows = pl.ds(pl.multiple_of(blk * brows, brows), brows)
    assert sub % SUBLANES == 0

    @pl.when((blk == 0) & (j == 0))
    def _():
        @pl.when(layer == 0)
        def _():
            hm_s[...] = meta_ref[...]

        hm_new, state, u_meta = _meta_layer(hm_s[...], cntm_ref[...], w)
        hm_s[...] = hm_new
        for h in range(HEADS):
            smeta_ref[h] = state[h]
        bufmeta_ref[...] = u_meta

    @pl.when(j == 0)
    def _():
        @pl.when(layer == 0)
        def _():
            xin_s[...] = x_ref[...].reshape(xin_s.shape)

        @pl.when(layer > 0)
        def _():
            xin_s[...] = h_s[blk_rows, :]

        xn, q, k, v, u, a_gate = _gate_in(xin_s[...], w)
        la = _log_gate(a_gate)
        xn_s[...] = xn
        q_s[...] = q
        k_s[...] = k
        v_s[...] = v
        la_s[...] = la
        for g in range(len(POOL_WINDOWS)):
            u_s[g] = u[:, g * POOL_GW:(g + 1) * POOL_GW]

    nr = sub * n_tok
    rows = pl.ds(pl.multiple_of(j * nr, nr), nr)
    row = lax.broadcasted_iota(jnp.int32, (nr, 1), 0)
    tok = row % n_tok
    seq = row // n_tok
    eye = _eye(DKH)
    b = _cumsum_rows(la_s[rows, :], n_tok)
    b_end = b
    for d in range(1, n_tok):
        b_end = jnp.where(tok == n_tok - 1 - d, pltpu.roll(b, nr - d, 0), b_end)
    q = q_s[rows, :]
    k = k_s[rows, :]
    v = v_s[rows, :]
    qd = (q * jnp.exp(b)).astype(BF16)
    kd = k * jnp.exp(b_end - b)
    dec = jnp.exp(b_end)
    vb = v.astype(BF16)
    o_parts = _intra_by_distance(q, k, v, b, tok, n_tok)
    for h in range(HEADS):
        ks = slice(h * DKH, (h + 1) * DKH)
        vs = slice(h * DVH, (h + 1) * DVH)
        o = o_parts[h]
        for s in range(sub):
            mine = seq == s
            s_old = s0_ref[s, h]
            o = o + jnp.where(mine, _mm(qd[:, ks], s_old.astype(BF16)), 0.0)
            kd_s = jnp.where(mine, kd[:, ks], 0.0).astype(BF16)
            dcol = _row_to_col(dec[s * n_tok:s * n_tok + 1, ks], eye)
            s_ref[s, h] = dcol * s_old + _mm_tn(kd_s, vb[:, vs])
        o_parts[h] = o
    o_s[rows, :] = jnp.concatenate(o_parts, axis=-1)

    inv_cnt = cnt_ref[...]
    base = pl.multiple_of(j * nr, nr)
    hist = [buf0_ref[i] for i in range(POOL_BUF)]
    new = []
    for t in range(n_tok):
        rows_t = pl.ds(base + t, sub, stride=n_tok)
        new.append(jnp.concatenate([u_s[g, rows_t, :] for g in range(len(POOL_WINDOWS))], axis=-1))
    ext = hist + new
    for t in range(n_tok):
        last = POOL_BUF + t
        rows_t = pl.ds(base + t, sub, stride=n_tok)
        for g, width in enumerate(POOL_WINDOWS):
            cols = slice(g * POOL_GW, (g + 1) * POOL_GW)
            acc = ext[last][:, cols]
            for back in range(1, width):
                acc = acc + ext[last - back][:, cols]
            z_s[g, rows_t, :] = acc * inv_cnt[:, cols] - new[t][:, cols]
    for i in range(POOL_BUF):
        bufn_ref[i] = ext[i + n_tok]

    @pl.when(j == nj - 1)
    def _():
        z = jnp.concatenate([z_s[g] for g in range(len(POOL_WINDOWS))], axis=-1)
        h_new, y = _dense_out(xin_s[...], xn_s[...], o_s[...], z, w, True)

        @pl.when(layer < depth - 1)
        def _():
            h_s[blk_rows, :] = h_new

        @pl.when(layer == depth - 1)
        def _():
            y_ref[...] = y.reshape(y_ref.shape)


def _step_layers(x, s_all, buf_all, cnt, meta, cnt_meta, ws):
    bn, n_tok, _ = x.shape
    rows = bn * n_tok
    depth = s_all.shape[0]
    assert s_all.shape[1] == bn and buf_all.shape == (depth, POOL_BUF, bn, POOL_W)
    sub, blk = SAMPLE_SUB, SAMPLE_BLOCK
    assert bn % blk == 0 and blk % sub == 0
    n_blk, n_inner = bn // blk, blk // sub
    brows = blk * n_tok
    weights = [ws[n] for n in W_NAMES]

    def state_idx(l, i, j):
        return (l, i * n_inner + j, 0, 0, 0)

    def buf_idx(l, i, j):
        return (l, 0, i * n_inner + j, 0)

    in_specs = [
        pl.BlockSpec((blk, n_tok, D_MODEL), lambda l, i, j: (i, 0, 0)),
        pl.BlockSpec((None, sub, HEADS, DKH, DVH), state_idx),
        pl.BlockSpec((None, POOL_BUF, sub, POOL_W), buf_idx),
        _const_spec(cnt.shape),
        _const_spec(meta.shape),
        _const_spec(cnt_meta.shape),
    ] + _weight_specs(ws, lambda l, i, j: l)
    n_meta = meta.shape[0]
    out_shape = [jax.ShapeDtypeStruct(x.shape, F32),
                 jax.ShapeDtypeStruct(s_all.shape, F32),
                 jax.ShapeDtypeStruct(buf_all.shape, F32),
                 jax.ShapeDtypeStruct((depth, HEADS, DKH, DVH), F32),
                 jax.ShapeDtypeStruct((depth, n_meta, POOL_W), F32)]
    out_specs = [pl.BlockSpec((blk, n_tok, D_MODEL),
                              lambda l, i, j: (jnp.where(l == depth - 1, i, 0), 0, 0)),
                 pl.BlockSpec((None, sub, HEADS, DKH, DVH), state_idx),
                 pl.BlockSpec((None, POOL_BUF, sub, POOL_W), buf_idx),
                 pl.BlockSpec((None, HEADS, DKH, DVH), lambda l, i, j: (l, 0, 0, 0)),
                 pl.BlockSpec((None, n_meta, POOL_W), lambda l, i, j: (l, 0, 0))]
    kern = functools.partial(_step_kernel, n_tok=n_tok, sub=sub, depth=depth)
    scratch = [pltpu.VMEM((rows, D_MODEL), F32), pltpu.VMEM((n_meta, D_MODEL), F32),
               pltpu.VMEM((brows, D_MODEL), F32),
               pltpu.VMEM((brows, D_MODEL), BF16), pltpu.VMEM((brows, DK), F32),
               pltpu.VMEM((brows, DK), F32), pltpu.VMEM((brows, DV), F32),
               pltpu.VMEM((brows, DK), F32), pltpu.VMEM((len(POOL_WINDOWS), brows, POOL_GW), F32),
               pltpu.VMEM((brows, DV), F32), pltpu.VMEM((len(POOL_WINDOWS), brows, POOL_GW), F32)]
    return pl.pallas_call(
        kern,
        grid=(depth, n_blk, n_inner),
        in_specs=in_specs,
        out_specs=out_specs,
        out_shape=out_shape,
        scratch_shapes=scratch,
        compiler_params=pltpu.CompilerParams(
            dimension_semantics=("arbitrary", "arbitrary", "arbitrary"),
            vmem_limit_bytes=VMEM_LIMIT_BYTES),
        name="step_layers",
    )(x, s_all, buf_all, cnt, meta, cnt_meta, *weights)


def _prep_main_kernel(wt_ref, main_ref):
    main_ref[...] = wt_ref[...].T.astype(BF16)


def _prep_tail_kernel(wt_ref, al_ref, mg_ref):
    lane = lax.broadcasted_iota(jnp.int32, al_ref.shape, 1)
    al_ref[...] = jnp.where(lane < GATE_RANK, wt_ref[0, 0:LANES, :].T, 0.0).astype(BF16)
    mg_ref[...] = wt_ref[0, GATE_RANK:, :].T.astype(BF16)


def _prep_w_in(w_in):
    depth, rows, cols = w_in.shape
    n_tail = cols - MAIN_COLS
    wt = jnp.swapaxes(w_in, 1, 2)
    cb = PREP_COLS
    assert MAIN_COLS % cb == 0
    w_main = pl.pallas_call(
        _prep_main_kernel,
        grid=(depth, MAIN_COLS // cb),
        in_specs=[pl.BlockSpec((None, cb, rows), lambda l, i: (l, i, 0))],
        out_specs=pl.BlockSpec((None, rows, cb), lambda l, i: (l, 0, i)),
        out_shape=jax.ShapeDtypeStruct((depth, rows, MAIN_COLS), BF16),
        name="prep_w_main",
    )(wt)
    w_al, w_mg = pl.pallas_call(
        _prep_tail_kernel,
        grid=(depth,),
        in_specs=[pl.BlockSpec((pl.Element(1), pl.Element(n_tail), pl.Element(rows)),
                               lambda l: (l, MAIN_COLS, 0))],
        out_specs=[pl.BlockSpec((None, rows, LANES), lambda l: (l, 0, 0)),
                   pl.BlockSpec((None, rows, n_tail - GATE_RANK), lambda l: (l, 0, 0))],
        out_shape=[jax.ShapeDtypeStruct((depth, rows, LANES), BF16),
                   jax.ShapeDtypeStruct((depth, rows, n_tail - GATE_RANK), BF16)],
        compiler_params=pltpu.CompilerParams(vmem_limit_bytes=VMEM_LIMIT_BYTES),
        name="prep_w_tail",
    )(wt)
    return w_main, w_al, w_mg


def _stacked_weights(norm_g, w_in, w_alpha, b_alpha, gla_gain, w_a, pool_w, pool_scale, w_b,
                     b_merge, w_out, final_norm_g):
    w_main, w_al, w_mg = _prep_w_in(w_in)
    pad = LANES - GATE_RANK
    vectors = {"norm_g": norm_g, "b_alpha": b_alpha, "gla_gain": gla_gain,
               "pool_scale": pool_scale, "b_merge": b_merge}
    assert all(vectors[name].shape[-1] == width for name, width in VEC_SEGS)
    return {
        "vec": jnp.concatenate([vectors[name] for name, _ in VEC_SEGS], axis=-1)[:, None, :],
        "w_main": w_main, "w_al": w_al, "w_mg": w_mg,
        "w_alpha": jnp.pad(w_alpha.astype(BF16), ((0, 0), (0, pad), (0, 0))),
        "w_a": w_a.astype(BF16),
        "pool_w": pool_w.astype(BF16),
        "w_b": w_b.astype(BF16),
        "w_out": w_out.astype(BF16),
        "final_g": final_norm_g[None, :],
    }


def _window_row():
    return np.repeat(np.asarray(POOL_WINDOWS, np.float32), POOL_GW)[None, :]


def kernel(x_prompt, x_sample, state_gla, state_pool, meta_tokens, norm_g, w_in, w_alpha, b_alpha,
           gla_gain, w_a, pool_w, pool_scale, w_b, b_merge, w_out, final_norm_g):
    depth = w_in.shape[0]
    wrow = _window_row()
    cnt_meta = jnp.asarray(np.minimum(wrow, np.arange(1, N_META + 1, dtype=np.float32)[:, None]))
    inv_w = jnp.asarray(1.0 / wrow)

    ws = _stacked_weights(norm_g, w_in, w_alpha, b_alpha, gla_gain, w_a, pool_w, pool_scale, w_b,
                          b_merge, w_out, final_norm_g)
    buf_s = jnp.swapaxes(state_pool, 1, 2)
    y_s, sg_s, bufn_s, s_meta, buf_meta = _step_layers(
        x_sample, state_gla, buf_s, inv_w, meta_tokens.astype(F32), cnt_meta, ws)

    h_p = x_prompt
    sg_p, sp_p = None, []
    for l in range(depth):
        h_p, sg_p, b_p = _seq_layer(h_p, s_meta, buf_meta, inv_w, sg_p, ws, l,
                                    is_last=l == depth - 1)
        sp_p.append(b_p[:, POOL_PAD - POOL_BUF:, :])
    return (h_p, y_s, sg_p, jnp.stack(sp_p), sg_s, jnp.swapaxes(bufn_s, 1, 2))
```

```python
import functools

import numpy as np
import jax
import jax.numpy as jnp
from jax import lax
from jax.experimental import pallas as pl
from jax.experimental.pallas import tpu as pltpu

F32 = jnp.float32
BF16 = jnp.bfloat16

D_MODEL = 1024
N_META = 16
HEADS = 4
DKH = 128
DVH = 256
DK = HEADS * DKH
DV = HEADS * DVH
GATE_RANK = 16
GATE_TAU = 16.0
POOL_W = 512
POOL_WINDOWS = (2, 4, 8, 16)
POOL_GW = POOL_W // len(POOL_WINDOWS)
POOL_BUF = max(POOL_WINDOWS) - 1
POOL_PAD = POOL_BUF + 1
EPS = 1e-6
IN_SIZES = (DK, DK, DV, DV, POOL_W, POOL_W, GATE_RANK, 2 * D_MODEL)
IN_OFFS = tuple(int(o) for o in np.cumsum((0,) + IN_SIZES))

LANES = 128
SUBLANES = 8
VMEM_LIMIT_BYTES = 56 * 1024 * 1024

MAX_FACTORED_LOG_DECAY = 60.0

SEQ_BLOCK = 512
SEQ_CHUNK = 256
SAMPLE_BLOCK = 32
SAMPLE_SUB = 8
PREP_COLS = 512


def _mm(a, b):
    return jnp.dot(a, b, preferred_element_type=F32)


def _mm_nt(a, b):
    return lax.dot_general(a, b, (((1,), (1,)), ((), ())), preferred_element_type=F32)


def _mm_tn(a, b):
    return lax.dot_general(a, b, (((0,), (0,)), ((), ())), preferred_element_type=F32)


def _sigmoid(x):
    return 1.0 / (1.0 + jnp.exp(-x))


def _silu(x):
    return x * _sigmoid(x)


def _log_sigmoid(x):
    return jnp.minimum(x, 0.0) - jnp.log1p(jnp.exp(-jnp.abs(x)))


def _rmsnorm(x, g):
    ms = jnp.mean(x * x, axis=-1, keepdims=True)
    return x * lax.rsqrt(ms + EPS) * g


def _row_to_col(row, eye):
    return jnp.sum(eye * row, axis=1, keepdims=True)


def _eye(n):
    r = lax.broadcasted_iota(jnp.int32, (n, n), 0)
    c = lax.broadcasted_iota(jnp.int32, (n, n), 1)
    return (r == c).astype(F32)


W_NAMES = ("vec", "w_main", "w_al", "w_mg", "w_alpha", "w_a", "pool_w", "w_b", "w_out", "final_g")
MAIN_SEGS = ("w_q", "w_k", "w_v", "w_ga", "w_u", "w_gb")
MAIN_COLS = IN_OFFS[len(MAIN_SEGS)]
VEC_SEGS = (("norm_g", D_MODEL), ("b_alpha", DK), ("gla_gain", DV), ("pool_scale", POOL_W),
            ("b_merge", 2 * D_MODEL))


def _weight_views(refs):
    w = dict(zip(W_NAMES, refs))
    for i, name in enumerate(MAIN_SEGS):
        w[name] = w["w_main"].at[:, IN_OFFS[i]:IN_OFFS[i + 1]]
    off = 0
    for name, width in VEC_SEGS:
        w[name] = w["vec"].at[:, off:off + width]
        off += width
    return w


def _gate_in(x, w):
    xn = _rmsnorm(x, w["norm_g"][...]).astype(BF16)
    alow = _mm(xn, w["w_al"][...])
    q = _mm(xn, w["w_q"][...]) * (DKH ** -0.5)
    k = _mm(xn, w["w_k"][...])
    a = _mm(alow.astype(BF16), w["w_alpha"][...]) + w["b_alpha"][...]
    v = _mm(xn, w["w_v"][...])
    u = _mm(xn, w["w_u"][...])
    return xn, q, k, v, u, a


def _log_gate(a):
    return _log_sigmoid(a) * (1.0 / GATE_TAU)


def _dense_out(x, xn, o, z, w, is_last, pre=None):
    if pre is None:
        pre = (_mm(xn, w["w_ga"][...]), _mm(xn, w["w_gb"][...]), _mm(xn, w["w_mg"][...]))
    ga, gb, mg = pre
    gain = w["gla_gain"][...]
    parts = []
    for h in range(HEADS):
        oh = o[:, h * DVH:(h + 1) * DVH]
        ms = jnp.mean(oh * oh, axis=-1, keepdims=True)
        parts.append(oh * lax.rsqrt(ms + EPS) * gain[:, h * DVH:(h + 1) * DVH])
    on = jnp.concatenate(parts, axis=-1) * _silu(ga)
    ya = _mm(on.astype(BF16), w["w_a"][...])

    yp = []
    for g in range(len(POOL_WINDOWS)):
        zg = z[:, g * POOL_GW:(g + 1) * POOL_GW].astype(BF16)
        yp.append(_mm(zg, w["pool_w"][g]))
    y_pool = jnp.concatenate(yp, axis=-1) * w["pool_scale"][...]
    yb = _mm((y_pool * _silu(gb)).astype(BF16), w["w_b"][...])

    gates = _sigmoid(mg + w["b_merge"][...])
    merged = gates[:, :D_MODEL] * ya + gates[:, D_MODEL:] * yb
    h_new = x + _mm(merged.astype(BF16), w["w_out"][...])
    y = _rmsnorm(h_new, w["final_g"][...]) if is_last else None
    return h_new, y


def _window_sums(e):
    p1 = e + pltpu.roll(e, 1, 0)
    t = p1[:, POOL_GW:]
    p2 = t + pltpu.roll(t, 2, 0)
    t = p2[:, POOL_GW:]
    p3 = t + pltpu.roll(t, 4, 0)
    t = p3[:, POOL_GW:]
    p4 = t + pltpu.roll(t, 8, 0)
    return jnp.concatenate([p1[:, :POOL_GW], p2[:, :POOL_GW], p3[:, :POOL_GW], p4], axis=-1)


def _split_hi_lo(x):
    hi = x.astype(BF16)
    lo = (x - hi.astype(F32)).astype(BF16)
    return hi, lo


def _cumsum_rows(x, n):
    pos = lax.broadcasted_iota(jnp.int32, (x.shape[0], 1), 0) % n
    step = 1
    while step < n:
        x = x + jnp.where(pos >= step, pltpu.roll(x, step, 0), 0.0)
        step *= 2
    return x


def _seq_kernel(*refs, tb, chunk, layer, is_last):
    n_w = len(W_NAMES)
    n_in = 5 if layer else 4
    h_ref, s0_ref, buf0_ref, cnt_ref = refs[:4]
    w = _weight_views(refs[n_in:n_in + n_w])
    hy_ref, sall_ref, bufn_ref, o_s, ext_s, sc_s, b_s, k_s = refs[n_in + n_w:]
    s_ref = sall_ref.at[layer]

    t = pl.program_id(1)

    @pl.when(t == 0)
    def _():
        s_ref[0] = s0_ref[0]
        ext_s[0:POOL_PAD, :] = buf0_ref[0]
        if layer:
            sall_ref[0:layer] = refs[4][...]

    x = h_ref[0]
    xn, q, k, v, u, a_gate = _gate_in(x, w)
    n_chunks = tb // chunk
    chunks = [(c, h) for c in range(n_chunks) for h in range(HEADS)]

    def sl(c, h):
        return (slice(c * chunk, (c + 1) * chunk), slice(h * DKH, (h + 1) * DKH),
                slice(h * DVH, (h + 1) * DVH))

    neg = jnp.minimum(a_gate, 0.0)
    steepest = jnp.sum(neg[0:chunk], axis=0, keepdims=True)
    for c in range(1, n_chunks):
        steepest = jnp.minimum(
            steepest, jnp.sum(neg[c * chunk:(c + 1) * chunk], axis=0, keepdims=True))
    decay_bound = (jnp.min(steepest) - chunk * np.log(2.0)) * (1.0 / GATE_TAU)
    factored_ok = decay_bound >= -MAX_FACTORED_LOG_DECAY
    la = _log_gate(a_gate)
    vb = v.astype(BF16)
    causal = (lax.broadcasted_iota(jnp.int32, (chunk, chunk), 0)
              >= lax.broadcasted_iota(jnp.int32, (chunk, chunk), 1))
    tril = causal.astype(BF16)
    eye = _eye(DKH)
    la_hi, la_lo = _split_hi_lo(la)
    b = jnp.concatenate(
        [_mm(tril, la_hi[c * chunk:(c + 1) * chunk])
         + _mm(tril, la_lo[c * chunk:(c + 1) * chunk]) for c in range(n_chunks)], axis=0)
    ga = _mm(xn, w["w_ga"][...])
    b_ends = [b[(c + 1) * chunk - 1:(c + 1) * chunk, :] for c in range(n_chunks)]
    to_end = jnp.concatenate(
        [b_ends[c] - b[c * chunk:(c + 1) * chunk, :] for c in range(n_chunks)], axis=0)
    qd = (q * jnp.exp(b)).astype(BF16)
    kd = (k * jnp.exp(to_end)).astype(BF16)

    kinv = (k * jnp.exp(-b)).astype(BF16)
    a_m = {}
    for c, h in chunks:
        rows, ks, _ = sl(c, h)
        a = _mm_nt(qd[rows, ks], kinv[rows, ks])
        a_m[c, h] = jnp.where(causal, a, 0.0).astype(BF16)
    gb = _mm(xn, w["w_gb"][...])
    d_s, dcol = {}, {}
    for c, h in chunks:
        rows, ks, vs = sl(c, h)
        d_s[c, h] = _mm_tn(kd[rows, ks], vb[rows, vs])
        dcol[c, h] = _row_to_col(jnp.exp(b_ends[c])[:, ks], eye)
    mg = _mm(xn, w["w_mg"][...])

    s_cur = [s_ref[0, h] for h in range(HEADS)]
    for c, h in chunks:
        rows, ks, vs = sl(c, h)
        s_in = s_cur[h].astype(BF16)
        sc_s[c, h] = s_in
        lhs = jnp.concatenate([a_m[c, h], qd[rows, ks]], axis=1)
        rhs = jnp.concatenate([vb[rows, vs], s_in], axis=0)
        o_s[rows, vs] = _mm(lhs, rhs)
        s_cur[h] = dcol[c, h] * s_cur[h] + d_s[c, h]
    for h in range(HEADS):
        s_ref[0, h] = s_cur[h]

    ext_s[POOL_PAD:POOL_PAD + tb, :] = u
    z = _window_sums(ext_s[...])[POOL_PAD:, :] * cnt_ref[...] - u
    bufn_ref[0] = ext_s[tb:tb + POOL_PAD, :]
    ext_s[0:POOL_PAD, :] = ext_s[tb:tb + POOL_PAD, :]

    h_new, y = _dense_out(x, xn, o_s, z, w, is_last, pre=(ga, gb, mg))
    hy_ref[0] = y if is_last else h_new

    @pl.when(jnp.logical_not(factored_ok))
    def _():
        b_s[...] = b
        k_s[...] = k
        col_id = lax.broadcasted_iota(jnp.int32, (chunk, chunk), 1)
        for c, h in chunks:
            rows, ks, vs = sl(c, h)
            qh = q[rows, ks]
            bh = b[rows, ks]

            def columns(g, acc, c=c, ks=ks, qh=qh, bh=bh):
                r0 = pl.multiple_of(c * chunk + g * SUBLANES, SUBLANES)
                b_tile = b_s[pl.ds(r0, SUBLANES), ks]
                k_tile = k_s[pl.ds(r0, SUBLANES), ks]
                for r in range(SUBLANES):
                    decay = jnp.exp(jnp.minimum(bh - b_tile[r:r + 1, :], 0.0))
                    col = jnp.sum(qh * k_tile[r:r + 1, :] * decay, axis=1, keepdims=True)
                    acc = jnp.where(col_id == g * SUBLANES + r, col, acc)
                return acc

            a = lax.fori_loop(0, chunk // SUBLANES, columns, jnp.zeros((chunk, chunk), F32))
            a = jnp.where(causal, a, 0.0).astype(BF16)
            o_s[rows, vs] = _mm(qd[rows, ks], sc_s[c, h]) + _mm(a, vb[rows, vs])
        h_fix, y_fix = _dense_out(x, xn, o_s, z, w, is_last, pre=(ga, gb, mg))
        hy_ref[0] = y_fix if is_last else h_fix


def _const_spec(shape):
    nd = len(shape)
    return pl.BlockSpec(shape, lambda *_: (0,) * nd, pipeline_mode=pl.Buffered(1))


def _weight_specs(ws, layer_of):
    specs = []
    for name in W_NAMES:
        shape = ws[name].shape
        if name == "final_g":
            specs.append(_const_spec(shape))
        else:
            zeros = (0,) * (len(shape) - 1)
            specs.append(pl.BlockSpec((None,) + shape[1:],
                                      lambda *g, zeros=zeros: (layer_of(*g),) + zeros,
                                      pipeline_mode=pl.Buffered(1)))
    return specs


def _seq_layer(h, s_meta, buf_meta, inv_cnt, s_prev, ws, layer, *, is_last):
    bn, t_len, _ = h.shape
    tb = min(SEQ_BLOCK, t_len)
    chunk = min(SEQ_CHUNK, tb)
    assert t_len % tb == 0 and tb % chunk == 0 and buf_meta.shape[1] == POOL_PAD
    weights = [ws[n] for n in W_NAMES]
    in_specs = [
        pl.BlockSpec((1, tb, D_MODEL), lambda b, t: (b, t, 0)),
        pl.BlockSpec((1, HEADS, DKH, DVH), lambda b, t: (layer, 0, 0, 0)),
        pl.BlockSpec((1, POOL_PAD, POOL_W), lambda b, t: (layer, 0, 0)),
        _const_spec(inv_cnt.shape),
    ]
    args = [h, s_meta, buf_meta, inv_cnt]
    if layer:
        in_specs.append(pl.BlockSpec((layer, 1, HEADS, DKH, DVH), lambda b, t: (0, b, 0, 0, 0)))
        args.append(s_prev)
    in_specs += _weight_specs(ws, lambda b, t: layer)
    out_shape = [jax.ShapeDtypeStruct(h.shape, F32),
                 jax.ShapeDtypeStruct((layer + 1, bn, HEADS, DKH, DVH), F32),
                 jax.ShapeDtypeStruct((bn, POOL_PAD, POOL_W), F32)]
    out_specs = [pl.BlockSpec((1, tb, D_MODEL), lambda b, t: (b, t, 0)),
                 pl.BlockSpec((layer + 1, 1, HEADS, DKH, DVH), lambda b, t: (0, b, 0, 0, 0)),
                 pl.BlockSpec((1, POOL_PAD, POOL_W), lambda b, t: (b, 0, 0))]
    kern = functools.partial(_seq_kernel, tb=tb, chunk=chunk, layer=layer, is_last=is_last)
    return pl.pallas_call(
        kern,
        grid=(bn, t_len // tb),
        in_specs=in_specs,
        out_specs=out_specs,
        out_shape=out_shape,
        scratch_shapes=[pltpu.VMEM((tb, DV), F32), pltpu.VMEM((tb + POOL_PAD, POOL_W), F32),
                        pltpu.VMEM((tb // chunk, HEADS, DKH, DVH), BF16),
                        pltpu.VMEM((tb, DK), F32), pltpu.VMEM((tb, DK), F32)],
        compiler_params=pltpu.CompilerParams(
            dimension_semantics=("arbitrary", "arbitrary"), vmem_limit_bytes=VMEM_LIMIT_BYTES),
        name="seq_layer",
    )(*args, *weights)


def _intra_by_distance(q, k, v, b, tok, n_tok):
    o_parts = []
    for h in range(HEADS):
        ks = slice(h * DKH, (h + 1) * DKH)
        vs = slice(h * DVH, (h + 1) * DVH)
        o = jnp.sum(q[:, ks] * k[:, ks], axis=1, keepdims=True) * v[:, vs]
        for d in range(1, n_tok):
            kj = pltpu.roll(k[:, ks], d, 0)
            bj = pltpu.roll(b[:, ks], d, 0)
            score = jnp.sum(q[:, ks] * kj * jnp.exp(jnp.minimum(b[:, ks] - bj, 0.0)), axis=1,
                            keepdims=True)
            o = o + jnp.where(tok >= d, score, 0.0) * pltpu.roll(v[:, vs], d, 0)
        o_parts.append(o)
    return o_parts


def _meta_layer(xm, cnt, w):
    n = xm.shape[0]
    xn, q, k, v, u, a_gate = _gate_in(xm, w)
    b = _cumsum_rows(_log_gate(a_gate), n)
    tok = lax.broadcasted_iota(jnp.int32, (n, 1), 0)
    o = jnp.concatenate(_intra_by_distance(q, k, v, b, tok, n), axis=-1)
    kd = (k * jnp.exp(b[n - 1:n, :] - b)).astype(BF16)
    vb = v.astype(BF16)
    state = [_mm_tn(kd[:, h * DKH:(h + 1) * DKH], vb[:, h * DVH:(h + 1) * DVH])
             for h in range(HEADS)]
    ext = jnp.concatenate([jnp.zeros((POOL_PAD, POOL_W), F32), u], axis=0)
    z = _window_sums(ext)[POOL_PAD:, :] / cnt - u
    h_new, _ = _dense_out(xm, xn, o, z, w, False)
    return h_new, state, u
def _step_kernel(*refs, n_tok, sub, depth):
    n_w = len(W_NAMES)
    n_in = 6
    x_ref, s0_ref, buf0_ref, cnt_ref, meta_ref, cntm_ref = refs[:n_in]
    w = _weight_views(refs[n_in:n_in + n_w])
    y_ref, s_ref, bufn_ref, smeta_ref, bufmeta_ref = refs[n_in + n_w:n_in + n_w + 5]
    h_s, hm_s, xin_s, xn_s, q_s, k_s, v_s, la_s, u_s, o_s, z_s = refs[n_in + n_w + 5:]

    layer = pl.program_id(0)
    blk = pl.program_id(1)
    j = pl.program_id(2)
    nj = pl.num_programs(2)
    brows = xin_s.shape[0]
    blk_rows = pl.ds(pl.multiple_of(blk * brows, brows), brows)
    assert sub % SUBLANES == 0

    @pl.when((blk == 0) & (j == 0))
    def _():
        @pl.when(layer == 0)
        def _():
            hm_s[...] = meta_ref[...]

        hm_new, state, u_meta = _meta_layer(hm_s[...], cntm_ref[...], w)
        hm_s[...] = hm_new
        for h in range(HEADS):
            smeta_ref[h] = state[h]
        bufmeta_ref[...] = u_meta

    @pl.when(j == 0)
    def _():
        @pl.when(layer == 0)
        def _():
            xin_s[...] = x_ref[...].reshape(xin_s.shape)

        @pl.when(layer > 0)
        def _():
            xin_s[...] = h_s[blk_rows, :]

        xn, q, k, v, u, a_gate = _gate_in(xin_s[...], w)
        la = _log_gate(a_gate)
        xn_s[...] = xn
        q_s[...] = q
        k_s[...] = k
        v_s[...] = v
        la_s[...] = la
        for g in range(len(POOL_WINDOWS)):
            u_s[g] = u[:, g * POOL_GW:(g + 1) * POOL_GW]

    nr = sub * n_tok
    rows = pl.ds(pl.multiple_of(j * nr, nr), nr)
    row = lax.broadcasted_iota(jnp.int32, (nr, 1), 0)
    tok = row % n_tok
    seq = row // n_tok
    eye = _eye(DKH)
    b = _cumsum_rows(la_s[rows, :], n_tok)
    b_end = b
    for d in range(1, n_tok):
        b_end = jnp.where(tok == n_tok - 1 - d, pltpu.roll(b, nr - d, 0), b_end)
    q = q_s[rows, :]
    k = k_s[rows, :]
    v = v_s[rows, :]
    qd = (q * jnp.exp(b)).astype(BF16)
    kd = k * jnp.exp(b_end - b)
    dec = jnp.exp(b_end)
    vb = v.astype(BF16)
    o_parts = _intra_by_distance(q, k, v, b, tok, n_tok)
    for h in range(HEADS):
        ks = slice(h * DKH, (h + 1) * DKH)
        vs = slice(h * DVH, (h + 1) * DVH)
        o = o_parts[h]
        for s in range(sub):
            mine = seq == s
            s_old = s0_ref[s, h]
            o = o + jnp.where(mine, _mm(qd[:, ks], s_old.astype(BF16)), 0.0)
            kd_s = jnp.where(mine, kd[:, ks], 0.0).astype(BF16)
            dcol = _row_to_col(dec[s * n_tok:s * n_tok + 1, ks], eye)
            s_ref[s, h] = dcol * s_old + _mm_tn(kd_s, vb[:, vs])
        o_parts[h] = o
    o_s[rows, :] = jnp.concatenate(o_parts, axis=-1)

    inv_cnt = cnt_ref[...]
    base = pl.multiple_of(j * nr, nr)
    hist = [buf0_ref[i] for i in range(POOL_BUF)]
    new = []
    for t in range(n_tok):
        rows_t = pl.ds(base + t, sub, stride=n_tok)
        new.append(jnp.concatenate([u_s[g, rows_t, :] for g in range(len(POOL_WINDOWS))], axis=-1))
    ext = hist + new
    for t in range(n_tok):
        last = POOL_BUF + t
        rows_t = pl.ds(base + t, sub, stride=n_tok)
        for g, width in enumerate(POOL_WINDOWS):
            cols = slice(g * POOL_GW, (g + 1) * POOL_GW)
            acc = ext[last][:, cols]
            for back in range(1, width):
                acc = acc + ext[last - back][:, cols]
            z_s[g, rows_t, :] = acc * inv_cnt[:, cols] - new[t][:, cols]
    for i in range(POOL_BUF):
        bufn_ref[i] = ext[i + n_tok]

    @pl.when(j == nj - 1)
    def _():
        z = jnp.concatenate([z_s[g] for g in range(len(POOL_WINDOWS))], axis=-1)
        h_new, y = _dense_out(xin_s[...], xn_s[...], o_s, z, w, True)

        @pl.when(layer < depth - 1)
        def _():
            h_s[blk_rows, :] = h_new

        @pl.when(layer == depth - 1)
        def _():
            y_ref[...] = y.reshape(y_ref.shape)


def _step_layers(x, s_all, buf_all, cnt, meta, cnt_meta, ws):
    bn, n_tok, _ = x.shape
    rows = bn * n_tok
    depth = s_all.shape[0]
    assert s_all.shape[1] == bn and buf_all.shape == (depth, POOL_BUF, bn, POOL_W)
    sub, blk = SAMPLE_SUB, SAMPLE_BLOCK
    assert bn % blk == 0 and blk % sub == 0
    n_blk, n_inner = bn // blk, blk // sub
    brows = blk * n_tok
    weights = [ws[n] for n in W_NAMES]

    def state_idx(l, i, j):
        return (l, i * n_inner + j, 0, 0, 0)

    def buf_idx(l, i, j):
        return (l, 0, i * n_inner + j, 0)

    in_specs = [
        pl.BlockSpec((blk, n_tok, D_MODEL), lambda l, i, j: (i, 0, 0)),
        pl.BlockSpec((None, sub, HEADS, DKH, DVH), state_idx),
        pl.BlockSpec((None, POOL_BUF, sub, POOL_W), buf_idx),
        _const_spec(cnt.shape),
        _const_spec(meta.shape),
        _const_spec(cnt_meta.shape),
    ] + _weight_specs(ws, lambda l, i, j: l)
    n_meta = meta.shape[0]
    out_shape = [jax.ShapeDtypeStruct(x.shape, F32),
                 jax.ShapeDtypeStruct(s_all.shape, F32),
                 jax.ShapeDtypeStruct(buf_all.shape, F32),
                 jax.ShapeDtypeStruct((depth, HEADS, DKH, DVH), F32),
                 jax.ShapeDtypeStruct((depth, n_meta, POOL_W), F32)]
    out_specs = [pl.BlockSpec((blk, n_tok, D_MODEL),
                              lambda l, i, j: (jnp.where(l == depth - 1, i, 0), 0, 0)),
                 pl.BlockSpec((None, sub, HEADS, DKH, DVH), state_idx),
                 pl.BlockSpec((None, POOL_BUF, sub, POOL_W), buf_idx),
                 pl.BlockSpec((None, HEADS, DKH, DVH), lambda l, i, j: (l, 0, 0, 0)),
                 pl.BlockSpec((None, n_meta, POOL_W), lambda l, i, j: (l, 0, 0))]
    kern = functools.partial(_step_kernel, n_tok=n_tok, sub=sub, depth=depth)
    scratch = [pltpu.VMEM((rows, D_MODEL), F32), pltpu.VMEM((n_meta, D_MODEL), F32),
               pltpu.VMEM((brows, D_MODEL), F32),
               pltpu.VMEM((brows, D_MODEL), BF16), pltpu.VMEM((brows, DK), F32),
               pltpu.VMEM((brows, DK), F32), pltpu.VMEM((brows, DV), F32),
               pltpu.VMEM((brows, DK), F32), pltpu.VMEM((len(POOL_WINDOWS), brows, POOL_GW), F32),
               pltpu.VMEM((brows, DV), F32), pltpu.VMEM((len(POOL_WINDOWS), brows, POOL_GW), F32)]
    return pl.pallas_call(
        kern,
        grid=(depth, n_blk, n_inner),
        in_specs=in_specs,
        out_specs=out_specs,
        out_shape=out_shape,
        scratch_shapes=scratch,
        compiler_params=pltpu.CompilerParams(
            dimension_semantics=("arbitrary", "arbitrary", "arbitrary"),
            vmem_limit_bytes=VMEM_LIMIT_BYTES),
        name="step_layers",
    )(x, s_all, buf_all, cnt, meta, cnt_meta, *weights)


def _prep_main_kernel(wt_ref, main_ref):
    main_ref[...] = wt_ref[...].T.astype(BF16)


def _prep_tail_kernel(wt_ref, al_ref, mg_ref):
    lane = lax.broadcasted_iota(jnp.int32, al_ref.shape, 1)
    al_ref[...] = jnp.where(lane < GATE_RANK, wt_ref[0, 0:LANES, :].T, 0.0).astype(BF16)
    mg_ref[...] = wt_ref[0, GATE_RANK:, :].T.astype(BF16)


def _prep_w_in(w_in):
    depth, rows, cols = w_in.shape
    n_tail = cols - MAIN_COLS
    wt = jnp.swapaxes(w_in, 1, 2)
    cb = PREP_COLS
    assert MAIN_COLS % cb == 0
    w_main = pl.pallas_call(
        _prep_main_kernel,
        grid=(depth, MAIN_COLS // cb),
        in_specs=[pl.BlockSpec((None, cb, rows), lambda l, i: (l, i, 0))],
        out_specs=pl.BlockSpec((None, rows, cb), lambda l, i: (l, 0, i)),
        out_shape=jax.ShapeDtypeStruct((depth, rows, MAIN_COLS), BF16),
        name="prep_w_main",
    )(wt)
    w_al, w_mg = pl.pallas_call(
        _prep_tail_kernel,
        grid=(depth,),
        in_specs=[pl.BlockSpec((pl.Element(1), pl.Element(n_tail), pl.Element(rows)),
                               lambda l: (l, MAIN_COLS, 0))],
        out_specs=[pl.BlockSpec((None, rows, LANES), lambda l: (l, 0, 0)),
                   pl.BlockSpec((None, rows, n_tail - GATE_RANK), lambda l: (l, 0, 0))],
        out_shape=[jax.ShapeDtypeStruct((depth, rows, LANES), BF16),
                   jax.ShapeDtypeStruct((depth, rows, n_tail - GATE_RANK), BF16)],
        compiler_params=pltpu.CompilerParams(vmem_limit_bytes=VMEM_LIMIT_BYTES),
        name="prep_w_tail",
    )(wt)
    return w_main, w_al, w_mg


def _stacked_weights(norm_g, w_in, w_alpha, b_alpha, gla_gain, w_a, pool_w, pool_scale, w_b,
                     b_merge, w_out, final_norm_g):
    w_main, w_al, w_mg = _prep_w_in(w_in)
    pad = LANES - GATE_RANK
    vectors = {"norm_g": norm_g, "b_alpha": b_alpha, "gla_gain": gla_gain,
               "pool_scale": pool_scale, "b_merge": b_merge}
    assert all(vectors[name].shape[-1] == width for name, width in VEC_SEGS)
    return {
        "vec": jnp.concatenate([vectors[name] for name, _ in VEC_SEGS], axis=-1)[:, None, :],
        "w_main": w_main, "w_al": w_al, "w_mg": w_mg,
        "w_alpha": jnp.pad(w_alpha.astype(BF16), ((0, 0), (0, pad), (0, 0))),
        "w_a": w_a.astype(BF16),
        "pool_w": pool_w.astype(BF16),
        "w_b": w_b.astype(BF16),
        "w_out": w_out.astype(BF16),
        "final_g": final_norm_g[None, :],
    }


def _window_row():
    return np.repeat(np.asarray(POOL_WINDOWS, np.float32), POOL_GW)[None, :]


def kernel(x_prompt, x_sample, state_gla, state_pool, meta_tokens, norm_g, w_in, w_alpha, b_alpha,
           gla_gain, w_a, pool_w, pool_scale, w_b, b_merge, w_out, final_norm_g):
    depth = w_in.shape[0]
    wrow = _window_row()
    cnt_meta = jnp.asarray(np.minimum(wrow, np.arange(1, N_META + 1, dtype=np.float32)[:, None]))
    inv_w = jnp.asarray(1.0 / wrow)

    ws = _stacked_weights(norm_g, w_in, w_alpha, b_alpha, gla_gain, w_a, pool_w, pool_scale, w_b,
                          b_merge, w_out, final_norm_g)
    buf_s = jnp.swapaxes(state_pool, 1, 2)
    y_s, sg_s, bufn_s, s_meta, buf_meta = _step_layers(
        x_sample, state_gla, buf_s, inv_w, meta_tokens.astype(F32), cnt_meta, ws)

    h_p = x_prompt
    sg_p, sp_p = None, []
    for l in range(depth):
        h_p, sg_p, b_p = _seq_layer(h_p, s_meta, buf_meta, inv_w, sg_p, ws, l,
                                    is_last=l == depth - 1)
        sp_p.append(b_p[:, POOL_PAD - POOL_BUF:, :])
    return (h_p, y_s, sg_p, jnp.stack(sp_p), sg_s, jnp.swapaxes(bufn_s, 1, 2))
```
